```python
import math
import jax, jax.numpy as jnp
from jax import lax
import numpy as np

D_MODEL = 1024
BATCH = 8
SEQ = 8192
DEPTH = 2
DEC_BATCH = 8
DEC_SEQ = 32
PAST_LEN = 1024

CHUNK = 64
HA = 4
DH_A = 64
DV_A = 2 * DH_A
W_A = HA * DV_A
HB = 8
DH_B = 64
W_B = HB * DH_B
BAND_CHUNKS = 8
BAND = BAND_CHUNKS * CHUNK
REL_CLIP = 128
N_REL = 2 * REL_CLIP + 1
GC = 16
N_GROUPS = 32
W_C = N_GROUPS * GC
P_STATE = 64
D_FF = 11 * D_MODEL // 4
N_EXPERTS = 8
TOP_K = 2
D_FF_E = D_FF // 2
N_DENSE = (DEPTH + 1) // 2
N_MOE = DEPTH // 2
Q_BLOCK = 128
EPS = 1e-6
NEG = -1e30
OFF_QA = 0
OFF_KA = OFF_QA + HA * 2 * DH_A
OFF_VA = OFF_KA + HA * 2 * DH_A
OFF_QB = OFF_VA + W_A
OFF_KB = OFF_QB + W_B
OFF_VB = OFF_KB + W_B
OFF_U = OFF_VB + W_B
OFF_G = OFF_U + W_C
N_IN = OFF_G + 3 * D_MODEL

kernel_name = 'hybrid_streaming_encoder_step'


def rmsnorm(x, g):
    xf = x.astype(jnp.float32)
    y = xf * lax.rsqrt(jnp.mean(xf * xf, axis=-1, keepdims=True) + EPS)
    return (y * g.astype(jnp.float32)).astype(x.dtype)


def diff_maps(q, k, v, lam, mask):
    b, nq = q.shape[0], q.shape[1]
    nk = k.shape[1]
    q2 = q.reshape(b, nq, HA, 2, DH_A)
    k2 = k.reshape(b, nk, HA, 2, DH_A)
    s = jnp.einsum('bqhmd,bkhmd->bmhqk', q2, k2, preferred_element_type=jnp.float32) * (DH_A ** -0.5)
    if mask is not None:
        s = jnp.where(mask, s, NEG)
    p = jax.nn.softmax(s, axis=-1)
    w = p[:, 0] - lam * p[:, 1]
    return jnp.einsum('bhqk,bkhe->bqhe', w.astype(v.dtype), v)


def diff_attn_prompt(q, k, v, lam):
    b, L = q.shape[0], q.shape[1]
    nqb = L // Q_BLOCK
    kchunk = jnp.arange(L) // CHUNK
    qb = q.reshape(b, nqb, Q_BLOCK, HA, 2 * DH_A).transpose(1, 0, 2, 3, 4)

    def one_block(args):
        qi, i = args
        qchunk = (i * Q_BLOCK + jnp.arange(Q_BLOCK)) // CHUNK
        mask = kchunk[None, :] <= qchunk[:, None]
        return diff_maps(qi, k, v, lam, mask)

    o = lax.map(one_block, (qb, jnp.arange(nqb)))
    return o.transpose(1, 0, 2, 3, 4).reshape(b, L, HA, DV_A)


def rel_bias(table, q_pos, k_pos):
    rel = q_pos[:, None] - k_pos[None, :]
    idx = jnp.clip(rel, -REL_CLIP, REL_CLIP) + REL_CLIP
    return table.astype(jnp.float32)[idx].transpose(2, 0, 1)


def chunk_attn_prompt(q, k, v, table):
    b, L = q.shape[0], q.shape[1]
    nc = L // CHUNK
    nb = (BAND_CHUNKS + 1) * CHUNK
    qc = q.reshape(b, nc, CHUNK, HB, DH_B)
    pad = ((0, 0), (BAND_CHUNKS, 0), (0, 0), (0, 0), (0, 0))
    kp = jnp.pad(k.reshape(b, nc, CHUNK, HB, DH_B), pad)
    vp = jnp.pad(v.reshape(b, nc, CHUNK, HB, DH_B), pad)
    band_idx = jnp.arange(nc)[:, None] + jnp.arange(BAND_CHUNKS + 1)[None, :]
    kb = kp[:, band_idx].reshape(b, nc, nb, HB, DH_B)
    vb = vp[:, band_idx].reshape(b, nc, nb, HB, DH_B)
    valid = jnp.repeat(band_idx >= BAND_CHUNKS, CHUNK, axis=1)
    bias = rel_bias(table, BAND + jnp.arange(CHUNK), jnp.arange(nb))
    s = jnp.einsum('bcqhd,bckhd->bchqk', qc, kb, preferred_element_type=jnp.float32) * (DH_B ** -0.5)
    s = jnp.where(valid[None, :, None, None, :], s + bias[None, None], NEG)
    p = jax.nn.softmax(s, axis=-1)
    o = jnp.einsum('bchqk,bckhd->bcqhd', p.astype(v.dtype), vb)
    return o.reshape(b, L, HB, DH_B)


def chunk_attn_sample(q, k, v, ck, cv, table):
    t = q.shape[1]
    bp = ck.shape[1]
    kk = jnp.concatenate([ck, k], axis=1)
    vv = jnp.concatenate([cv, v], axis=1)
    bias = rel_bias(table, bp + jnp.arange(t), jnp.arange(bp + t))
    s = jnp.einsum('bqhd,bkhd->bhqk', q, kk, preferred_element_type=jnp.float32) * (DH_B ** -0.5) + bias[None]
    p = jax.nn.softmax(s, axis=-1)
    return jnp.einsum('bhqk,bkhd->bqhd', p.astype(vv.dtype), vv)


def ssm_combine(e1, e2):
    a1r, a1i, b1r, b1i = e1
    a2r, a2i, b2r, b2i = e2
    return (a2r * a1r - a2i * a1i,
            a2r * a1i + a2i * a1r,
            a2r * b1r - a2i * b1i + b2r,
            a2r * b1i + a2i * b1r + b2i)


def s5_ssm(u, a_re, a_im, log_dt, b_re, b_im, c_re, c_im, d, h0_re, h0_im):
    f32 = jnp.float32
    bsz, L = u.shape[0], u.shape[1]
    uf = u.astype(f32).reshape(bsz, L, N_GROUPS, GC)
    ar, ai = a_re.astype(f32), a_im.astype(f32)
    dt = jnp.exp(log_dt.astype(f32))[:, None]
    mag = jnp.exp(ar * dt)
    abar_r, abar_i = mag * jnp.cos(ai * dt), mag * jnp.sin(ai * dt)
    den = ar * ar + ai * ai
    nr, ni = abar_r - 1.0, abar_i
    z_r = (nr * ar + ni * ai) / den
    z_i = (ni * ar - nr * ai) / den
    br, bi = b_re.astype(f32), b_im.astype(f32)
    bb_r = z_r[..., None] * br - z_i[..., None] * bi
    bb_i = z_r[..., None] * bi + z_i[..., None] * br
    x_r = jnp.einsum('gpc,blgc->blgp', bb_r, uf)
    x_i = jnp.einsum('gpc,blgc->blgp', bb_i, uf)
    a_r = jnp.broadcast_to(abar_r, (1, L, N_GROUPS, P_STATE))
    a_i = jnp.broadcast_to(abar_i, (1, L, N_GROUPS, P_STATE))
    cr, ci, hr, hi = lax.associative_scan(ssm_combine, (a_r, a_i, x_r, x_i), axis=1)
    if h0_re is not None:
        h0r = h0_re.astype(f32)[:, None]
        h0i = h0_im.astype(f32)[:, None]
        hr, hi = hr + cr * h0r - ci * h0i, hi + cr * h0i + ci * h0r
    y = (jnp.einsum('gcp,blgp->blgc', c_re.astype(f32), hr)
         - jnp.einsum('gcp,blgp->blgc', c_im.astype(f32), hi)).reshape(bsz, L, W_C)
    y = y + d.astype(f32) * u.astype(f32)
    return y.astype(u.dtype), hr[:, -1].astype(u.dtype), hi[:, -1].astype(u.dtype)


def swiglu(h, w1, w3, w2):
    return (jax.nn.silu(h @ w1) * (h @ w3)) @ w2


def moe_swiglu(h, router_w, w1, w3, w2):
    logits = (h @ router_w).astype(jnp.float32)
    top_v, top_i = lax.top_k(logits, TOP_K)
    g = jax.nn.softmax(top_v, axis=-1)
    dense_gate = jnp.sum(jax.nn.one_hot(top_i, N_EXPERTS, dtype=jnp.float32) * g[..., None], axis=-2)
    dense_gate = dense_gate.astype(h.dtype)
    out = jnp.zeros_like(h)
    for e in range(N_EXPERTS):
        out = out + dense_gate[..., e:e + 1] * swiglu(h, w1[e], w3[e], w2[e])
    return out


def block(x, l, w, cache):
    bsz, L = x.shape[0], x.shape[1]
    h = rmsnorm(x, w['norm1_g'])
    proj = h @ w['w_in']
    qa = proj[..., OFF_QA:OFF_KA].reshape(bsz, L, HA, 2 * DH_A)
    ka = proj[..., OFF_KA:OFF_VA].reshape(bsz, L, HA, 2 * DH_A)
    va = proj[..., OFF_VA:OFF_QB].reshape(bsz, L, HA, DV_A)
    qb = proj[..., OFF_QB:OFF_KB].reshape(bsz, L, HB, DH_B)
    kb = proj[..., OFF_KB:OFF_VB].reshape(bsz, L, HB, DH_B)
    vb = proj[..., OFF_VB:OFF_U].reshape(bsz, L, HB, DH_B)
    u = proj[..., OFF_U:OFF_G]
    gates = jax.nn.sigmoid(proj[..., OFF_G:].astype(jnp.float32)).astype(x.dtype).reshape(bsz, L, 3, D_MODEL)
    lam_init = 0.8 - 0.6 * math.exp(-0.3 * l)
    f32 = jnp.float32
    lam = (jnp.exp(jnp.sum(w['lam_q1'].astype(f32) * w['lam_k1'].astype(f32)))
           - jnp.exp(jnp.sum(w['lam_q2'].astype(f32) * w['lam_k2'].astype(f32))) + lam_init)
    if cache is None:
        oa = diff_attn_prompt(qa, ka, va, lam)
        ob = chunk_attn_prompt(qb, kb, vb, w['rel_bias'])
        n_keep = min(BAND, L)
        kb_new, vb_new = kb[:, L - n_keep:], vb[:, L - n_keep:]
        h0_re, h0_im = None, None
    else:
        ck_a, cv_a, ck_b, cv_b, h0_re, h0_im = cache
        oa = diff_maps(qa, jnp.concatenate([ck_a, ka], axis=1), jnp.concatenate([cv_a, va], axis=1), lam, None)
        ob = chunk_attn_sample(qb, kb, vb, ck_b, cv_b, w['rel_bias'])
        kb_new, vb_new = kb, vb
    oa = rmsnorm(oa, w['subln_g']) * (1.0 - lam_init)
    yc, hr, hi = s5_ssm(u, w['ssm_a_re'], w['ssm_a_im'], w['ssm_log_dt'], w['ssm_b_re'], w['ssm_b_im'],
                        w['ssm_c_re'], w['ssm_c_im'], w['ssm_d'], h0_re, h0_im)
    gl = jax.nn.gelu(yc) @ w['w_glu']
    oc = gl[..., :W_C] * jax.nn.sigmoid(gl[..., W_C:])
    merged = (gates[:, :, 0] * (oa.reshape(bsz, L, W_A) @ w['w_br_a'])
              + gates[:, :, 1] * (ob.reshape(bsz, L, W_B) @ w['w_br_b'])
              + gates[:, :, 2] * (oc @ w['w_br_c']))
    x = x + merged @ w['w_out']
    h2 = rmsnorm(x, w['norm2_g'])
    if l % 2 == 0:
        f = swiglu(h2, w['ffn_w1'], w['ffn_w3'], w['ffn_w2'])
    else:
        f = moe_swiglu(h2, w['router_w'], w['moe_w1'], w['moe_w3'], w['moe_w2'])
    x = x + f
    return x, (ka, va, kb_new, vb_new, hr, hi)


def setup_inputs(seed: int = 0) -> dict:
    key = jax.random.key(seed)
    ks = iter(jax.random.split(key, 48))
    f32 = jnp.float32

    def nrm(shape, scale):
        return scale * jax.random.normal(next(ks), shape, f32)

    bp = min(BAND, PAST_LEN)
    inp = {}
    inp['x_prompt'] = nrm((BATCH, SEQ, D_MODEL), 1.0)
    inp['x_sample'] = nrm((DEC_BATCH, DEC_SEQ, D_MODEL), 1.0)
    inp['cache_dk'] = nrm((DEPTH, DEC_BATCH, PAST_LEN, HA, 2 * DH_A), 1.0)
    inp['cache_dv'] = nrm((DEPTH, DEC_BATCH, PAST_LEN, HA, DV_A), 1.0)
    inp['cache_bk'] = nrm((DEPTH, DEC_BATCH, bp, HB, DH_B), 1.0)
    inp['cache_bv'] = nrm((DEPTH, DEC_BATCH, bp, HB, DH_B), 1.0)
    inp['state_ssm_re'] = nrm((DEPTH, DEC_BATCH, N_GROUPS, P_STATE), 0.1)
    inp['state_ssm_im'] = nrm((DEPTH, DEC_BATCH, N_GROUPS, P_STATE), 0.1)
    inp['norm1_g'] = 1.0 + nrm((DEPTH, D_MODEL), 0.02)
    inp['w_in'] = nrm((DEPTH, D_MODEL, N_IN), D_MODEL ** -0.5)
    inp['lam_q1'] = nrm((DEPTH, DH_A), 0.1)
    inp['lam_k1'] = nrm((DEPTH, DH_A), 0.1)
    inp['lam_q2'] = nrm((DEPTH, DH_A), 0.1)
    inp['lam_k2'] = nrm((DEPTH, DH_A), 0.1)
    inp['subln_g'] = 1.0 + nrm((DEPTH, HA, DV_A), 0.02)
    inp['rel_bias'] = nrm((DEPTH, N_REL, HB), 0.5)
    inp['ssm_a_re'] = -0.5 + nrm((DEPTH, N_GROUPS, P_STATE), 0.01)
    inp['ssm_a_im'] = jnp.pi * jnp.arange(P_STATE, dtype=f32) + nrm((DEPTH, N_GROUPS, P_STATE), 0.01)
    inp['ssm_log_dt'] = jax.random.uniform(next(ks), (DEPTH, N_GROUPS), f32, math.log(1e-3), math.log(1e-1))
    inp['ssm_b_re'] = nrm((DEPTH, N_GROUPS, P_STATE, GC), (2 * GC) ** -0.5)
    inp['ssm_b_im'] = nrm((DEPTH, N_GROUPS, P_STATE, GC), (2 * GC) ** -0.5)
    inp['ssm_c_re'] = nrm((DEPTH, N_GROUPS, GC, P_STATE), 0.5)
    inp['ssm_c_im'] = nrm((DEPTH, N_GROUPS, GC, P_STATE), 0.5)
    inp['ssm_d'] = nrm((DEPTH, W_C), 1.0)
    inp['w_glu'] = nrm((DEPTH, W_C, 2 * W_C), W_C ** -0.5)
    inp['w_br_a'] = nrm((DEPTH, W_A, D_MODEL), W_A ** -0.5)
    inp['w_br_b'] = nrm((DEPTH, W_B, D_MODEL), W_B ** -0.5)
    inp['w_br_c'] = nrm((DEPTH, W_C, D_MODEL), W_C ** -0.5)
    inp['w_out'] = nrm((DEPTH, D_MODEL, D_MODEL), D_MODEL ** -0.5)
    inp['norm2_g'] = 1.0 + nrm((DEPTH, D_MODEL), 0.02)
    inp['ffn_w1'] = nrm((N_DENSE, D_MODEL, D_FF), D_MODEL ** -0.5)
    inp['ffn_w3'] = nrm((N_DENSE, D_MODEL, D_FF), D_MODEL ** -0.5)
    inp['ffn_w2'] = nrm((N_DENSE, D_FF, D_MODEL), D_FF ** -0.5)
    inp['router_w'] = nrm((N_MOE, D_MODEL, N_EXPERTS), D_MODEL ** -0.5)
    inp['moe_w1'] = nrm((N_MOE, N_EXPERTS, D_MODEL, D_FF_E), D_MODEL ** -0.5)
    inp['moe_w3'] = nrm((N_MOE, N_EXPERTS, D_MODEL, D_FF_E), D_MODEL ** -0.5)
    inp['moe_w2'] = nrm((N_MOE, N_EXPERTS, D_FF_E, D_MODEL), D_FF_E ** -0.5)
    inp['final_g'] = 1.0 + nrm((D_MODEL,), 0.02)
    return inp


def reference(x_prompt, x_sample, cache_dk, cache_dv, cache_bk, cache_bv, state_ssm_re, state_ssm_im,
              norm1_g, w_in, lam_q1, lam_k1, lam_q2, lam_k2, subln_g, rel_bias, ssm_a_re, ssm_a_im,
              ssm_log_dt, ssm_b_re, ssm_b_im, ssm_c_re, ssm_c_im, ssm_d, w_glu, w_br_a, w_br_b, w_br_c,
              w_out, norm2_g, ffn_w1, ffn_w3, ffn_w2, router_w, moe_w1, moe_w3, moe_w2, final_g):
    yp, ys = x_prompt, x_sample
    st_p, st_s = [], []
    for l in range(DEPTH):
        w = {'norm1_g': norm1_g[l], 'w_in': w_in[l], 'lam_q1': lam_q1[l], 'lam_k1': lam_k1[l],
             'lam_q2': lam_q2[l], 'lam_k2': lam_k2[l], 'subln_g': subln_g[l], 'rel_bias': rel_bias[l],
             'ssm_a_re': ssm_a_re[l], 'ssm_a_im': ssm_a_im[l], 'ssm_log_dt': ssm_log_dt[l],
             'ssm_b_re': ssm_b_re[l], 'ssm_b_im': ssm_b_im[l], 'ssm_c_re': ssm_c_re[l], 'ssm_c_im': ssm_c_im[l],
             'ssm_d': ssm_d[l], 'w_glu': w_glu[l], 'w_br_a': w_br_a[l], 'w_br_b': w_br_b[l],
             'w_br_c': w_br_c[l], 'w_out': w_out[l], 'norm2_g': norm2_g[l]}
        if l % 2 == 0:
            w['ffn_w1'], w['ffn_w3'], w['ffn_w2'] = ffn_w1[l // 2], ffn_w3[l // 2], ffn_w2[l // 2]
        else:
            w['router_w'], w['moe_w1'] = router_w[l // 2], moe_w1[l // 2]
            w['moe_w3'], w['moe_w2'] = moe_w3[l // 2], moe_w2[l // 2]
        yp, sp = block(yp, l, w, None)
        ys, ss = block(ys, l, w, (cache_dk[l], cache_dv[l], cache_bk[l], cache_bv[l],
                                  state_ssm_re[l], state_ssm_im[l]))
        st_p.append(sp)
        st_s.append(ss)
    y_prompt = rmsnorm(yp, final_g)
    y_sample = rmsnorm(ys, final_g)
    new_dk_p = jnp.stack([s[0] for s in st_p])
    new_dv_p = jnp.stack([s[1] for s in st_p])
    new_bk_p = jnp.stack([s[2] for s in st_p])
    new_bv_p = jnp.stack([s[3] for s in st_p])
    new_sre_p = jnp.stack([s[4] for s in st_p])
    new_sim_p = jnp.stack([s[5] for s in st_p])
    new_dk_s = jnp.stack([s[0] for s in st_s])
    new_dv_s = jnp.stack([s[1] for s in st_s])
    new_bk_s = jnp.stack([s[2] for s in st_s])
    new_bv_s = jnp.stack([s[3] for s in st_s])
    new_sre_s = jnp.stack([s[4] for s in st_s])
    new_sim_s = jnp.stack([s[5] for s in st_s])
    return (y_prompt, y_sample, new_dk_p, new_dv_p, new_bk_p, new_bv_p, new_sre_p, new_sim_p,
            new_dk_s, new_dv_s, new_bk_s, new_bv_s, new_sre_s, new_sim_s)
```

```python
import functools
import math

import jax
import jax.numpy as jnp
from jax import lax
from jax.experimental import pallas as pl
from jax.experimental.pallas import tpu as pltpu

F32 = jnp.float32
BF16 = jnp.bfloat16

D_MODEL = 1024
CHUNK = 64
HA = 4
DH_A = 64
DV_A = 2 * DH_A
W_A = HA * DV_A
HB = 8
DH_B = 64
W_B = HB * DH_B
BAND_CHUNKS = 8
BAND = BAND_CHUNKS * CHUNK
REL_CLIP = 128
N_REL = 2 * REL_CLIP + 1
GC = 16
N_GROUPS = 32
W_C = N_GROUPS * GC
P_STATE = 64
N_STATE = N_GROUPS * P_STATE
D_FF = 11 * D_MODEL // 4
N_EXPERTS = 8
D_FF_E = D_FF // 2
EPS = 1e-6
NEG = -1e30
N_QKVU = 7 * 512

LANES = 128
SUBLANES = 8
MXU_DIM = 256
VMEM_LIMIT = 56 * 1024 * 1024


def _cparams(*sem):
    return pltpu.CompilerParams(dimension_semantics=sem, vmem_limit_bytes=VMEM_LIMIT)


def _rms(x, g):
    return x * lax.rsqrt(jnp.mean(x * x, axis=-1, keepdims=True) + EPS) * g


def _sigmoid(x):
    return 1.0 / (1.0 + jnp.exp(-x))


def _dot(a, b):
    return jnp.dot(a, b, preferred_element_type=F32)


def _dot_nt(a, b):
    return lax.dot_general(a, b, (((1,), (1,)), ((), ())), preferred_element_type=F32)


def _norm_proj_kernel(x_ref, g_ref, w_ref, qa_ref, ka_ref, va_ref, kab_ref, vab_ref,
                      qb_ref, kb_ref, vb_ref, kbb_ref, vbb_ref, u_ref):
    h = _rms(x_ref[...], g_ref[...]).astype(BF16)

    def proj(c):
        return _dot(h, w_ref[:, c * 512:(c + 1) * 512])

    qa_ref[...] = (proj(0) * (DH_A ** -0.5)).astype(BF16)
    ka = proj(1)
    ka_ref[...] = ka
    kab_ref[...] = ka.astype(BF16)
    va = proj(2)
    va_ref[...] = va
    vab_ref[...] = va.astype(BF16)
    qb_ref[...] = (proj(3) * (DH_B ** -0.5)).astype(BF16)
    kb = proj(4)
    kb_ref[...] = kb
    kbb_ref[...] = kb.astype(BF16)
    vb = proj(5)
    vb_ref[...] = vb
    vbb_ref[...] = vb.astype(BF16)
    u_ref[...] = proj(6)


def _norm_proj(x, g, w_qkvu, tm):
    B, L, D = x.shape
    nt = L // tm
    tok = lambda dt: jax.ShapeDtypeStruct((B, L, 512), dt)
    tspec = pl.BlockSpec((None, tm, 512), lambda b, t: (b, t, 0))
    out_shape = [tok(BF16), tok(F32), tok(F32), tok(BF16), tok(BF16),
                 tok(BF16), tok(F32), tok(F32), tok(BF16), tok(BF16),
                 jax.ShapeDtypeStruct((L, B * 512), F32)]
    out_specs = [tspec] * 10 + [pl.BlockSpec((tm, 512), lambda b, t: (t, b))]
    return pl.pallas_call(
        _norm_proj_kernel,
        grid=(B, nt),
        in_specs=[pl.BlockSpec((None, tm, D), lambda b, t: (b, t, 0)),
                  pl.BlockSpec((1, D), lambda b, t: (0, 0)),
                  pl.BlockSpec((D, N_QKVU), lambda b, t: (0, 0))],
        out_specs=out_specs,
        out_shape=out_shape,
        compiler_params=_cparams("parallel", "parallel"),
        name="norm_proj",
    )(x, g, w_qkvu)


def _lam_value(lam_ref, lam_init):
    lv = lam_ref[...]
    e1 = jnp.exp(jnp.sum(lv[0:1, :] * lv[1:2, :], axis=-1, keepdims=True))
    e2 = jnp.exp(jnp.sum(lv[2:3, :] * lv[3:4, :], axis=-1, keepdims=True))
    return e1 - e2 + lam_init


def _subln(o, g, lam_init):
    return _rms(o, g) * (1.0 - lam_init)


def _split_halves(q):
    lane = lax.broadcasted_iota(jnp.int32, q.shape, 1)
    zero = jnp.zeros_like(q)
    return jnp.where(lane < 64, q, zero), jnp.where(lane >= 64, q, zero)


def _diff_attn_kernel(lam_ref, q_ref, k_ref, v_ref, g_ref, o_ref,
                      m1_s, l1_s, a1_s, m2_s, l2_s, a2_s, *, tq, tk, lam_init):
    qi = pl.program_id(2)
    q1, q2 = _split_halves(q_ref[...])

    m1_s[...] = jnp.full_like(m1_s, -jnp.inf)
    m2_s[...] = jnp.full_like(m2_s, -jnp.inf)
    l1_s[...] = jnp.zeros_like(l1_s)
    l2_s[...] = jnp.zeros_like(l2_s)
    a1_s[...] = jnp.zeros_like(a1_s)
    a2_s[...] = jnp.zeros_like(a2_s)

    def update(s, v, m_s, l_s, a_s):
        m_prev = m_s[...]
        m_new = jnp.maximum(m_prev, jnp.max(s, axis=-1, keepdims=True))
        alpha = jnp.exp(m_prev - m_new)
        p = jnp.exp(s - m_new)
        l_s[...] = alpha * l_s[...] + jnp.sum(p, axis=-1, keepdims=True)
        a_s[...] = alpha * a_s[...] + _dot(p.astype(BF16), v)
        m_s[...] = m_new

    def block(kj, masked):
        k0 = pl.multiple_of(kj * tk, tk)
        k = k_ref[pl.ds(k0, tk), :]
        v = v_ref[pl.ds(k0, tk), :]
        s1 = _dot_nt(q1, k)
        s2 = _dot_nt(q2, k)
        if masked:
            qc = (qi * tq + lax.broadcasted_iota(jnp.int32, (tq, tk), 0)) // CHUNK
            kc = (k0 + lax.broadcasted_iota(jnp.int32, (tq, tk), 1)) // CHUNK
            keep = kc <= qc
            s1 = jnp.where(keep, s1, NEG)
            s2 = jnp.where(keep, s2, NEG)
        update(s1, v, m1_s, l1_s, a1_s)
        update(s2, v, m2_s, l2_s, a2_s)

    n_full = qi * (tq // tk)
    n_diag = tq // tk

    def full_body(kj, c):
        block(kj, False)
        return c

    lax.fori_loop(0, n_full, full_body, 0)
    for j in range(n_diag):
        block(n_full + j, True)

    lam = _lam_value(lam_ref, lam_init)
    o = a1_s[...] / l1_s[...] - lam * (a2_s[...] / l2_s[...])
    o_ref[...] = _subln(o, g_ref[...], lam_init).astype(BF16)


def _diff_attn_prompt(q, k, v, lam_vecs, subln_g, lam_init, tq, tk):
    B, L, _ = q.shape
    kern = functools.partial(_diff_attn_kernel, tq=tq, tk=tk, lam_init=lam_init)
    return pl.pallas_call(
        kern,
        grid=(B, HA, L // tq),
        in_specs=[pl.BlockSpec((4, DH_A), lambda b, h, i: (0, 0)),
                  pl.BlockSpec((None, tq, DV_A), lambda b, h, i: (b, i, h)),
                  pl.BlockSpec((None, L, DV_A), lambda b, h, i: (b, 0, h)),
                  pl.BlockSpec((None, L, DV_A), lambda b, h, i: (b, 0, h)),
                  pl.BlockSpec((None, 1, DV_A), lambda b, h, i: (h, 0, 0))],
        out_specs=pl.BlockSpec((None, tq, DV_A), lambda b, h, i: (b, i, h)),
        out_shape=jax.ShapeDtypeStruct((B, L, W_A), BF16),
        scratch_shapes=[pltpu.VMEM((tq, 1), F32), pltpu.VMEM((tq, 1), F32), pltpu.VMEM((tq, DV_A), F32),
                        pltpu.VMEM((tq, 1), F32), pltpu.VMEM((tq, 1), F32), pltpu.VMEM((tq, DV_A), F32)],
        compiler_params=_cparams("parallel", "parallel", "parallel"),
        name="diff_attn",
    )(lam_vecs, q, k, v, subln_g.reshape(HA, 1, DV_A))


def _diff_attn_sample_kernel(lam_ref, q_ref, ck_ref, cv_ref, k_ref, v_ref, g_ref, o_ref, *, lam_init):
    lam = _lam_value(lam_ref, lam_init)
    for h in range(HA):
        cols = slice(h * DV_A, (h + 1) * DV_A)
        q1, q2 = _split_halves(q_ref[:, cols])
        ck, cv = ck_ref[:, cols], cv_ref[:, cols]
        k, v = k_ref[:, cols], v_ref[:, cols]

        def one_map(qm):
            sc = _dot_nt(qm, ck)
            sn = _dot_nt(qm, k)
            m = jnp.maximum(jnp.max(sc, axis=-1, keepdims=True), jnp.max(sn, axis=-1, keepdims=True))
            pc = jnp.exp(sc - m)
            pn = jnp.exp(sn - m)
            l = jnp.sum(pc, axis=-1, keepdims=True) + jnp.sum(pn, axis=-1, keepdims=True)
            return (_dot(pc.astype(BF16), cv) + _dot(pn.astype(BF16), v)) / l

        o = one_map(q1) - lam * one_map(q2)
        o_ref[:, cols] = _subln(o, g_ref[h], lam_init).astype(BF16)


def _diff_attn_sample(q, ck, cv, k, v, lam_vecs, subln_g, lam_init):
    B, T, _ = q.shape
    P = ck.shape[1]
    kern = functools.partial(_diff_attn_sample_kernel, lam_init=lam_init)
    new = pl.BlockSpec((None, T, W_A), lambda b: (b, 0, 0))
    old = pl.BlockSpec((None, P, W_A), lambda b: (b, 0, 0))
    return pl.pallas_call(
        kern,
        grid=(B,),
        in_specs=[pl.BlockSpec((4, DH_A), lambda b: (0, 0)), new, old, old, new, new,
                  pl.BlockSpec((HA, 1, DV_A), lambda b: (0, 0, 0))],
        out_specs=new,
        out_shape=jax.ShapeDtypeStruct((B, T, W_A), BF16),
        compiler_params=_cparams("parallel"),
        name="diff_attn_sample",
    )(lam_vecs, q, ck, cv, k, v, subln_g.reshape(HA, 1, DV_A))


def _rel_bias_kernel(tab_ref, o_ref, *, rows, cols, off0, off_step, masked):
    h = pl.program_id(0)
    d = pl.program_id(1)
    r = lax.broadcasted_iota(jnp.int32, (rows, cols), 0)
    c = lax.broadcasted_iota(jnp.int32, (rows, cols), 1)
    idx = jnp.clip(off0 + d * off_step + r - c, -REL_CLIP, REL_CLIP) + REL_CLIP

    def body(j, acc):
        return jnp.where(idx == j, tab_ref[j * HB + h], acc)

    bias = lax.fori_loop(0, N_REL, body, jnp.zeros((rows, cols), F32))
    if masked:
        dc = d * (off_step // CHUNK) + r // CHUNK - c // CHUNK
        bias = jnp.where((dc >= 0) & (dc <= BAND_CHUNKS), bias, NEG)
    o_ref[...] = bias


def _rel_bias(table, n_off, rows, cols, off0, off_step, masked):
    kern = functools.partial(_rel_bias_kernel, rows=rows, cols=cols, off0=off0,
                             off_step=off_step, masked=masked)
    return pl.pallas_call(
        kern,
        grid=(HB, n_off),
        in_specs=[pl.BlockSpec(memory_space=pltpu.SMEM)],
        out_specs=pl.BlockSpec((None, None, rows, cols), lambda h, d: (h, d, 0, 0)),
        out_shape=jax.ShapeDtypeStruct((HB, n_off, rows, cols), F32),
        compiler_params=_cparams("parallel", "parallel"),
        name="rel_bias",
    )(table.reshape(N_REL * HB))


def _pair_softmax_out(q, ks, vs, bias_fn):
    lane = lax.broadcasted_iota(jnp.int32, (q.shape[0], LANES), 1)
    outs = []
    for hh, qm in enumerate(_split_halves(q)):
        ss = [_dot_nt(qm, k) + bias_fn(hh, j) for j, k in enumerate(ks)]
        m = functools.reduce(jnp.maximum, [jnp.max(s, axis=-1, keepdims=True) for s in ss])
        ps = [jnp.exp(s - m) for s in ss]
        l = functools.reduce(jnp.add, [jnp.sum(p, axis=-1, keepdims=True) for p in ps])
        o = functools.reduce(jnp.add, [_dot(p.astype(BF16), v) for p, v in zip(ps, vs)])
        outs.append(o / l)
    return jnp.where(lane < 64, outs[0], outs[1])


def _band_attn_kernel(q_ref, k0_ref, k1_ref, k2_ref, v0_ref, v1_ref, v2_ref, b_ref, o_ref):
    i = pl.program_id(2)
    ks = [k0_ref[...], k1_ref[...], k2_ref[...]]
    vs = [v0_ref[...], v1_ref[...], v2_ref[...]]

    def bias_fn(hh, d):
        return jnp.where(i >= d, b_ref[hh, d], NEG)

    o_ref[...] = _pair_softmax_out(q_ref[...], ks, vs, bias_fn).astype(BF16)


def _band_attn_prompt(q, k, v, bias):
    B, L, _ = q.shape
    t = 4 * CHUNK
    qspec = pl.BlockSpec((None, t, LANES), lambda b, hp, i: (b, i, hp))
    kspec = lambda d: pl.BlockSpec((None, t, LANES), lambda b, hp, i: (b, jnp.maximum(i - d, 0), hp))
    return pl.pallas_call(
        _band_attn_kernel,
        grid=(B, HB // 2, L // t),
        in_specs=[qspec, kspec(0), kspec(1), kspec(2), kspec(0), kspec(1), kspec(2),
                  pl.BlockSpec((2, 3, t, t), lambda b, hp, i: (hp, 0, 0, 0))],
        out_specs=qspec,
        out_shape=jax.ShapeDtypeStruct((B, L, W_B), BF16),
        compiler_params=_cparams("parallel", "parallel", "parallel"),
        name="band_attn",
    )(q, k, k, k, v, v, v, bias)


def _band_attn_sample_kernel(q_ref, ck_ref, cv_ref, k_ref, v_ref, bc_ref, bn_ref, o_ref):
    for hp in range(HB // 2):
        cols = slice(hp * LANES, (hp + 1) * LANES)
        ks = [ck_ref[:, cols], k_ref[:, cols]]
        vs = [cv_ref[:, cols], v_ref[:, cols]]

        def bias_fn(hh, j, hp=hp):
            return (bc_ref if j == 0 else bn_ref)[2 * hp + hh, 0]

        o_ref[:, cols] = _pair_softmax_out(q_ref[:, cols], ks, vs, bias_fn).astype(BF16)


def _band_attn_sample(q, ck, cv, k, v, bias_c, bias_n):
    B, T, _ = q.shape
    P = ck.shape[1]
    new = pl.BlockSpec((None, T, W_B), lambda b: (b, 0, 0))
    old = pl.BlockSpec((None, P, W_B), lambda b: (b, 0, 0))
    return pl.pallas_call(
        _band_attn_sample_kernel,
        grid=(B,),
        in_specs=[new, old, old, new, new,
                  pl.BlockSpec((HB, 1, T, P), lambda b: (0, 0, 0, 0)),
                  pl.BlockSpec((HB, 1, T, T), lambda b: (0, 0, 0, 0))],
        out_specs=new,
        out_shape=jax.ShapeDtypeStruct((B, T, W_B), BF16),
        compiler_params=_cparams("parallel"),
        name="band_attn_sample",
    )(q, ck, cv, k, v, bias_c, bias_n)


N_SLAB = N_STATE // MXU_DIM


def _ssm_kernel(u_ref, wxr_ref, wxi_ref, ar_ref, ai_ref, h0r_ref, h0i_ref, wcr_ref, wci_ref,
                d_ref, wglu_ref, oc_ref, hro_ref, hio_ref, xr_s, xi_s, hr_s, hi_s, *, tt, nb, half):
    i = pl.program_id(0)

    @pl.when(i == 0)
    def _():
        hr_s[...] = h0r_ref[...]
        hi_s[...] = h0i_ref[...]

    u = u_ref[...]
    ub = u.astype(BF16)
    for j in range(N_SLAB):
        us = ub[:, LANES * (j // 2):LANES * (j // 2 + 1)]
        xr_s[:, MXU_DIM * j:MXU_DIM * (j + 1)] = _dot(us, wxr_ref[j])
        xi_s[:, MXU_DIM * j:MXU_DIM * (j + 1)] = _dot(us, wxi_ref[j])

    for c in range(N_STATE // half):
        cols = slice(c * half, (c + 1) * half)
        ar = jnp.broadcast_to(ar_ref[:, cols], (nb, half))
        ai = jnp.broadcast_to(ai_ref[:, cols], (nb, half))

        def step(t, carry, cols=cols, ar=ar, ai=ai):
            hr, hi = carry
            r0 = pl.multiple_of(t * nb, nb)
            nhr = ar * hr - ai * hi + xr_s[pl.ds(r0, nb), cols]
            nhi = ar * hi + ai * hr + xi_s[pl.ds(r0, nb), cols]
            xr_s[pl.ds(r0, nb), cols] = nhr
            xi_s[pl.ds(r0, nb), cols] = nhi
            return nhr, nhi

        hr, hi = lax.fori_loop(0, tt, step, (hr_s[:, cols], hi_s[:, cols]))
        hr_s[:, cols] = hr
        hi_s[:, cols] = hi

    hro_ref[...] = hr_s[...]
    hio_ref[...] = hi_s[...]

    ys = []
    for s in range(W_C // LANES):
        acc = None
        for j in (2 * s, 2 * s + 1):
            hrb = xr_s[:, MXU_DIM * j:MXU_DIM * (j + 1)].astype(BF16)
            hib = xi_s[:, MXU_DIM * j:MXU_DIM * (j + 1)].astype(BF16)
            part = _dot(hrb, wcr_ref[j]) + _dot(hib, wci_ref[j])
            acc = part if acc is None else acc + part
        ys.append(acc)
    y = jnp.concatenate(ys, axis=-1) + d_ref[...] * u
    ge = 0.5 * y * (1.0 + jnp.tanh(math.sqrt(2.0 / math.pi) * (y + 0.044715 * (y * y * y))))
    gl = _dot(ge.astype(BF16), wglu_ref[...])
    oc_ref[...] = (gl[:, :W_C] * _sigmoid(gl[:, W_C:])).astype(BF16)


def _ssm(u_tm, wxr, wxi, ar, ai, h0r, h0i, wcr, wci, d, wglu, tt):
    n_rows = u_tm.shape[0]
    nb = h0r.shape[0]
    rows = tt * nb
    half = N_STATE // 2
    kern = functools.partial(_ssm_kernel, tt=tt, nb=nb, half=half)
    full = lambda shape: pl.BlockSpec(shape, lambda i: (0,) * len(shape))
    return pl.pallas_call(
        kern,
        grid=(n_rows // rows,),
        in_specs=[pl.BlockSpec((rows, W_C), lambda i: (i, 0)),
                  full((N_SLAB, LANES, MXU_DIM)), full((N_SLAB, LANES, MXU_DIM)),
                  full((1, N_STATE)), full((1, N_STATE)),
                  full((nb, N_STATE)), full((nb, N_STATE)),
                  full((N_SLAB, MXU_DIM, LANES)), full((N_SLAB, MXU_DIM, LANES)),
                  full((1, W_C)), full((W_C, 2 * W_C))],
        out_specs=[pl.BlockSpec((rows, W_C), lambda i: (i, 0)),
                   full((nb, N_STATE)), full((nb, N_STATE))],
        out_shape=[jax.ShapeDtypeStruct((n_rows, W_C), BF16),
                   jax.ShapeDtypeStruct((nb, N_STATE), F32),
                   jax.ShapeDtypeStruct((nb, N_STATE), F32)],
        scratch_shapes=[pltpu.VMEM((rows, N_STATE), F32), pltpu.VMEM((rows, N_STATE), F32),
                        pltpu.VMEM((nb, N_STATE), F32), pltpu.VMEM((nb, N_STATE), F32)],
        compiler_params=_cparams("arbitrary"),
        name="ssm",
    )(u_tm, wxr, wxi, ar, ai, h0r, h0i, wcr, wci, d, wglu)


def _ssm_weights(w):
    ar, ai = w['ssm_a_re'].astype(F32), w['ssm_a_im'].astype(F32)
    dt = jnp.exp(w['ssm_log_dt'].astype(F32))[:, None]
    mag = jnp.exp(ar * dt)
    abar_r, abar_i = mag * jnp.cos(ai * dt), mag * jnp.sin(ai * dt)
    den = ar * ar + ai * ai
    nr, ni = abar_r - 1.0, abar_i
    z_r = (nr * ar + ni * ai) / den
    z_i = (ni * ar - nr * ai) / den
    br, bi = w['ssm_b_re'].astype(F32), w['ssm_b_im'].astype(F32)
    bb_r = z_r[..., None] * br - z_i[..., None] * bi
    bb_i = z_r[..., None] * bi + z_i[..., None] * br

    gps = MXU_DIM // P_STATE
    eye = jnp.eye(gps, dtype=F32)

    def x_slabs(bb):
        b4 = bb.reshape(N_SLAB, gps, P_STATE, GC)
        blk = jnp.einsum('jgpc,gh->jhcgp', b4, eye).reshape(N_SLAB, gps * GC, MXU_DIM)
        zero = jnp.zeros_like(blk)
        even = jnp.concatenate([blk, zero], axis=1)
        odd = jnp.concatenate([zero, blk], axis=1)
        sel = (jnp.arange(N_SLAB) % 2 == 0)[:, None, None]
        return jnp.where(sel, even, odd).astype(BF16)

    def y_slabs(cc):
        c4 = cc.reshape(N_SLAB, gps, GC, P_STATE)
        blk = jnp.einsum('jgcp,gh->jgphc', c4, eye).reshape(N_SLAB, MXU_DIM, gps * GC)
        zero = jnp.zeros_like(blk)
        even = jnp.concatenate([blk, zero], axis=2)
        odd = jnp.concatenate([zero, blk], axis=2)
        sel = (jnp.arange(N_SLAB) % 2 == 0)[:, None, None]
        return jnp.where(sel, even, odd).astype(BF16)

    return dict(wxr=x_slabs(bb_r), wxi=x_slabs(bb_i),
                ar=abar_r.reshape(1, N_STATE), ai=abar_i.reshape(1, N_STATE),
                wcr=y_slabs(w['ssm_c_re'].astype(F32)), wci=y_slabs(-w['ssm_c_im'].astype(F32)))


def _merge_out_kernel(x_ref, oa_ref, ob_ref, oc_ref, g_ref, wg_ref, wa_ref, wb_ref, wc_ref, wo_ref, o_ref):
    x = x_ref[...]
    h = _rms(x, g_ref[...]).astype(BF16)
    merged = None
    for j, (br_ref, w_ref) in enumerate(((oa_ref, wa_ref), (ob_ref, wb_ref), (oc_ref, wc_ref))):
        gate = _sigmoid(_dot(h, wg_ref[:, j * D_MODEL:(j + 1) * D_MODEL]))
        term = gate * _dot(br_ref[...], w_ref[...])
        merged = term if merged is None else merged + term
    o_ref[...] = x + _dot(merged.astype(BF16), wo_ref[...])


def _merge_out(x, oa, ob, oc_tm, g, wg, wa, wb, wc, wo, tm):
    B, L, D = x.shape
    full = lambda shape: pl.BlockSpec(shape, lambda b, t: (0,) * len(shape))
    br = pl.BlockSpec((None, tm, 512), lambda b, t: (b, t, 0))
    return pl.pallas_call(
        _merge_out_kernel,
        grid=(B, L // tm),
        in_specs=[pl.BlockSpec((None, tm, D), lambda b, t: (b, t, 0)), br, br,
                  pl.BlockSpec((tm, 512), lambda b, t: (t, b)),
                  full((1, D)), full((D, 3 * D)), full((W_A, D)), full((W_B, D)), full((W_C, D)), full((D, D))],
        out_specs=pl.BlockSpec((None, tm, D), lambda b, t: (b, t, 0)),
        out_shape=jax.ShapeDtypeStruct((B, L, D), F32),
        compiler_params=_cparams("parallel", "parallel"),
        name="merge_out",
    )(x, oa, ob, oc_tm, g, wg, wa, wb, wc, wo)


FF_CHUNK = 2 * MXU_DIM


def _swiglu(h, w1_ref, w3_ref, w2_ref):
    n_ff = w1_ref.shape[1]
    acc = None
    for c0 in range(0, n_ff, FF_CHUNK):
        c1 = min(c0 + FF_CHUNK, n_ff)
        a = _dot(h, w1_ref[:, c0:c1])
        b = _dot(h, w3_ref[:, c0:c1])
        part = _dot((a * _sigmoid(a) * b).astype(BF16), w2_ref[c0:c1, :])
        acc = part if acc is None else acc + part
    return acc


def _ffn_kernel(x_ref, g_ref, w1_ref, w3_ref, w2_ref, fg_ref, o_ref, *, final_norm):
    x = x_ref[...]
    h = _rms(x, g_ref[...]).astype(BF16)
    y = x + _swiglu(h, w1_ref, w3_ref, w2_ref)
    o_ref[...] = _rms(y, fg_ref[...]) if final_norm else y


def _ffn(x, g, w1, w3, w2, final_g, final_norm, tm):
    B, L, D = x.shape
    full = lambda shape: pl.BlockSpec(shape, lambda b, t: (0,) * len(shape), pipeline_mode=pl.Buffered(1))
    tok = pl.BlockSpec((None, tm, D), lambda b, t: (b, t, 0))
    return pl.pallas_call(
        functools.partial(_ffn_kernel, final_norm=final_norm),
        grid=(B, L // tm),
        in_specs=[tok, full((1, D)), full((D, D_FF)), full((D, D_FF)), full((D_FF, D)), full((1, D))],
        out_specs=tok,
        out_shape=jax.ShapeDtypeStruct((B, L, D), F32),
        compiler_params=_cparams("parallel", "parallel"),
        name="ffn",
    )(x, g, w1, w3, w2, final_g)


def _moe_kernel(x_ref, g_ref, rw_ref, w1_ref, w3_ref, w2_ref, fg_ref, o_ref, h_s, gate_s, acc_s, *, final_norm):
    e = pl.program_id(2)
    lane = lax.broadcasted_iota(jnp.int32, gate_s.shape, 1)

    @pl.when(e == 0)
    def _():
        hf = _rms(x_ref[...], g_ref[...])
        h_s[...] = hf.astype(BF16)
        logits = jnp.dot(hf, rw_ref[...], preferred_element_type=F32, precision=lax.Precision.HIGHEST)
        logits = jnp.where(lane < N_EXPERTS, logits, -jnp.inf)
        v1 = jnp.max(logits, axis=-1, keepdims=True)
        i1 = jnp.min(jnp.where(logits == v1, lane, LANES), axis=-1, keepdims=True)
        rest = jnp.where(lane == i1, -jnp.inf, logits)
        v2 = jnp.max(rest, axis=-1, keepdims=True)
        i2 = jnp.min(jnp.where(rest == v2, lane, LANES), axis=-1, keepdims=True)
        ex = jnp.exp(v2 - v1)
        g1 = 1.0 / (1.0 + ex)
        g2 = ex / (1.0 + ex)
        gate_s[...] = jnp.where(lane == i1, g1, 0.0) + jnp.where(lane == i2, g2, 0.0)
        acc_s[...] = x_ref[...]

    ge = jnp.sum(jnp.where(lane == e, gate_s[...], 0.0), axis=-1, keepdims=True)
    acc_s[...] += ge * _swiglu(h_s[...], w1_ref, w3_ref, w2_ref)

    @pl.when(e == N_EXPERTS - 1)
    def _():
        y = acc_s[...]
        o_ref[...] = _rms(y, fg_ref[...]) if final_norm else y


def _moe(x, g, rw, w1, w3, w2, final_g, final_norm, tm):
    B, L, D = x.shape
    full = lambda shape: pl.BlockSpec(shape, lambda b, t, e: (0,) * len(shape))
    tok = pl.BlockSpec((None, tm, D), lambda b, t, e: (b, t, 0))
    return pl.pallas_call(
        functools.partial(_moe_kernel, final_norm=final_norm),
        grid=(B, L // tm, N_EXPERTS),
        in_specs=[tok, full((1, D)), full((D, LANES)),
                  pl.BlockSpec((None, D, D_FF_E), lambda b, t, e: (e, 0, 0)),
                  pl.BlockSpec((None, D, D_FF_E), lambda b, t, e: (e, 0, 0)),
                  pl.BlockSpec((None, D_FF_E, D), lambda b, t, e: (e, 0, 0)),
                  full((1, D))],
        out_specs=tok,
        out_shape=jax.ShapeDtypeStruct((B, L, D), F32),
        scratch_shapes=[pltpu.VMEM((tm, D), BF16), pltpu.VMEM((tm, LANES), F32), pltpu.VMEM((tm, D), F32)],
        compiler_params=_cparams("parallel", "parallel", "arbitrary"),
        name="moe",
    )(x, g, rw, w1, w3, w2, final_g)


def _pick(n, cands):
    for c in cands:
        if n % c == 0:
            return c
    return n


def _layer(x, l, w, cache, final_g, last):
    B, L, D = x.shape
    lam_init = 0.8 - 0.6 * math.exp(-0.3 * l)
    tm = _pick(L, (512, 256, 128, 64, 32))
    row = lambda a: a.reshape(1, -1).astype(F32)

    w_in = w['w_in'].astype(BF16)
    (qa, ka, va, ka_b, va_b, qb, kb, vb, kb_b, vb_b, u_tm) = _norm_proj(
        x, row(w['norm1_g']), w_in[:, :N_QKVU], tm)

    lam_vecs = jnp.stack([w['lam_q1'], w['lam_k1'], w['lam_q2'], w['lam_k2']]).astype(F32)
    table = w['rel_bias'].astype(F32)
    if cache is None:
        tq = _pick(L, (512, 256))
        oa = _diff_attn_prompt(qa, ka_b, va_b, lam_vecs, w['subln_g'].astype(F32), lam_init, tq, 4 * CHUNK)
        bias = _rel_bias(table, 3, 4 * CHUNK, 4 * CHUNK, 0, 4 * CHUNK, True)
        ob = _band_attn_prompt(qb, kb_b, vb_b, bias)
        n_keep = min(BAND, L)
        kb_new, vb_new = kb[:, L - n_keep:], vb[:, L - n_keep:]
        h0r = jnp.zeros((B, N_STATE), F32)
        h0i = jnp.zeros((B, N_STATE), F32)
    else:
        ck_a, cv_a, ck_b, cv_b, h0_re, h0_im = cache
        P = ck_a.shape[1]
        bp = ck_b.shape[1]
        oa = _diff_attn_sample(qa, ck_a.reshape(B, P, W_A).astype(BF16), cv_a.reshape(B, P, W_A).astype(BF16),
                               ka_b, va_b, lam_vecs, w['subln_g'].astype(F32), lam_init)
        bias_c = _rel_bias(table, 1, L, bp, bp, 0, False)
        bias_n = _rel_bias(table, 1, L, L, 0, 0, False)
        ob = _band_attn_sample(qb, ck_b.reshape(B, bp, W_B).astype(BF16), cv_b.reshape(B, bp, W_B).astype(BF16),
                               kb_b, vb_b, bias_c, bias_n)
        kb_new, vb_new = kb, vb
        h0r = h0_re.reshape(B, N_STATE).astype(F32)
        h0i = h0_im.reshape(B, N_STATE).astype(F32)

    sw = _ssm_weights(w)
    oc_tm, hr, hi = _ssm(u_tm.reshape(L * B, W_C), sw['wxr'], sw['wxi'], sw['ar'], sw['ai'], h0r, h0i,
                         sw['wcr'], sw['wci'], row(w['ssm_d']), w['w_glu'].astype(BF16), _pick(L, (64, 32)))

    x = _merge_out(x, oa, ob, oc_tm.reshape(L, B * W_C), row(w['norm1_g']), w_in[:, N_QKVU:],
                   w['w_br_a'].astype(BF16), w['w_br_b'].astype(BF16), w['w_br_c'].astype(BF16),
                   w['w_out'].astype(BF16), tm)

    fg = row(final_g)
    if l % 2 == 0:
        x = _ffn(x, row(w['norm2_g']), w['ffn_w1'].astype(BF16), w['ffn_w3'].astype(BF16),
                 w['ffn_w2'].astype(BF16), fg, last, tm)
    else:
        rw = jnp.pad(w['router_w'].astype(F32), ((0, 0), (0, LANES - N_EXPERTS)))
        x = _moe(x, row(w['norm2_g']), rw, w['moe_w1'].astype(BF16),
                 w['moe_w3'].astype(BF16), w['moe_w2'].astype(BF16), fg, last, tm)

    state = (ka.reshape(B, L, HA, DV_A), va.reshape(B, L, HA, DV_A),
             kb_new.reshape(B, -1, HB, DH_B), vb_new.reshape(B, -1, HB, DH_B),
             hr.reshape(B, N_GROUPS, P_STATE), hi.reshape(B, N_GROUPS, P_STATE))
    return x, state


_PER_LAYER = ('norm1_g', 'w_in', 'lam_q1', 'lam_k1', 'lam_q2', 'lam_k2', 'subln_g', 'rel_bias',
              'ssm_a_re', 'ssm_a_im', 'ssm_log_dt', 'ssm_b_re', 'ssm_b_im', 'ssm_c_re', 'ssm_c_im',
              'ssm_d', 'w_glu', 'w_br_a', 'w_br_b', 'w_br_c', 'w_out', 'norm2_g')


def kernel(x_prompt, x_sample, cache_dk, cache_dv, cache_bk, cache_bv, state_ssm_re, state_ssm_im,
           norm1_g, w_in, lam_q1, lam_k1, lam_q2, lam_k2, subln_g, rel_bias, ssm_a_re, ssm_a_im,
           ssm_log_dt, ssm_b_re, ssm_b_im, ssm_c_re, ssm_c_im, ssm_d, w_glu, w_br_a, w_br_b, w_br_c,
           w_out, norm2_g, ffn_w1, ffn_w3, ffn_w2, router_w, moe_w1, moe_w3, moe_w2, final_g):
    stacked = dict(norm1_g=norm1_g, w_in=w_in, lam_q1=lam_q1, lam_k1=lam_k1, lam_q2=lam_q2, lam_k2=lam_k2,
                   subln_g=subln_g, rel_bias=rel_bias, ssm_a_re=ssm_a_re, ssm_a_im=ssm_a_im,
                   ssm_log_dt=ssm_log_dt, ssm_b_re=ssm_b_re, ssm_b_im=ssm_b_im, ssm_c_re=ssm_c_re,
                   ssm_c_im=ssm_c_im, ssm_d=ssm_d, w_glu=w_glu, w_br_a=w_br_a, w_br_b=w_br_b,
                   w_br_c=w_br_c, w_out=w_out, norm2_g=norm2_g)
    depth = w_in.shape[0]
    yp, ys = x_prompt, x_sample
    st_p, st_s = [], []
    for l in range(depth):
        w = {name: stacked[name][l] for name in _PER_LAYER}
        if l % 2 == 0:
            w['ffn_w1'], w['ffn_w3'], w['ffn_w2'] = ffn_w1[l // 2], ffn_w3[l // 2], ffn_w2[l // 2]
        else:
            w['router_w'], w['moe_w1'] = router_w[l // 2], moe_w1[l // 2]
            w['moe_w3'], w['moe_w2'] = moe_w3[l // 2], moe_w2[l // 2]
        last = l == depth - 1
        yp, sp = _layer(yp, l, w, None, final_g, last)
        ys, ss = _layer(ys, l, w, (cache_dk[l], cache_dv[l], cache_bk[l], cache_bv[l],
                                   state_ssm_re[l], state_ssm_im[l]), final_g, last)
        st_p.append(sp)
        st_s.append(ss)
    outs = [yp, ys]
    for states in (st_p, st_s):
        for j in range(6):
            outs.append(jnp.stack([s[j] for s in states]))
    return tuple(outs)
```

```python
import functools
import math

import jax
import jax.numpy as jnp
from jax import lax
from jax.experimental import pallas as pl
from jax.experimental.pallas import tpu as pltpu

F32 = jnp.float32
BF16 = jnp.bfloat16

D_MODEL = 1024
CHUNK = 64
HA = 4
DH_A = 64
DV_A = 2 * DH_A
W_A = HA * DV_A
HB = 8
DH_B = 64
W_B = HB * DH_B
BAND_CHUNKS = 8
BAND = BAND_CHUNKS * CHUNK
REL_CLIP = 128
N_REL = 2 * REL_CLIP + 1
GC = 16
N_GROUPS = 32
W_C = N_GROUPS * GC
P_STATE = 64
N_STATE = N_GROUPS * P_STATE
D_FF = 11 * D_MODEL // 4
N_EXPERTS = 8
D_FF_E = D_FF // 2
EPS = 1e-6
NEG = -1e30
LOG2E = math.log2(math.e)
N_QKVU = 7 * 512

LANES = 128
SUBLANES = 8
MXU_DIM = 256
VMEM_LIMIT = 56 * 1024 * 1024


def _cparams(*sem):
    return pltpu.CompilerParams(dimension_semantics=sem, vmem_limit_bytes=VMEM_LIMIT)


def _rms(x, g):
    return x * lax.rsqrt(jnp.mean(x * x, axis=-1, keepdims=True) + EPS) * g


def _sigmoid(x):
    return 1.0 / (1.0 + jnp.exp(-x))


def _dot(a, b):
    return jnp.dot(a, b, preferred_element_type=F32)


def _dot_nt(a, b):
    return lax.dot_general(a, b, (((1,), (1,)), ((), ())), preferred_element_type=F32)


def _norm_proj_kernel(x_ref, g_ref, w_ref, qa_ref, ka_ref, va_ref, kab_ref, vab_ref,
                      qb_ref, kb_ref, vb_ref, kbb_ref, vbb_ref, u_ref, *, transpose_v):
    h = _rms(x_ref[...], g_ref[...]).astype(BF16)

    def proj(c):
        return _dot(h, w_ref[:, c * 512:(c + 1) * 512])

    qa_ref[...] = (proj(0) * (DH_A ** -0.5 * LOG2E)).astype(BF16)
    ka = proj(1)
    ka_ref[...] = ka
    kab_ref[...] = ka.astype(BF16)
    va = proj(2)
    va_ref[...] = va
    vab_ref[...] = (va.T if transpose_v else va).astype(BF16)
    qb_ref[...] = (proj(3) * (DH_B ** -0.5)).astype(BF16)
    kb = proj(4)
    kb_ref[...] = kb
    kbb_ref[...] = kb.astype(BF16)
    vb = proj(5)
    vb_ref[...] = vb
    vbb_ref[...] = vb.astype(BF16)
    u_ref[...] = proj(6)


def _norm_proj(x, g, w_qkvu, tm, transpose_v):
    B, L, D = x.shape
    nt = L // tm
    tok = lambda dt: jax.ShapeDtypeStruct((B, L, 512), dt)
    tspec = pl.BlockSpec((None, tm, 512), lambda b, t: (b, t, 0))
    out_shape = [tok(BF16), tok(F32), tok(F32), tok(BF16), tok(BF16),
                 tok(BF16), tok(F32), tok(F32), tok(BF16), tok(BF16),
                 jax.ShapeDtypeStruct((L, B * 512), F32)]
    out_specs = [tspec] * 10 + [pl.BlockSpec((tm, 512), lambda b, t: (t, b))]
    if transpose_v:
        out_shape[4] = jax.ShapeDtypeStruct((B, nt, 512, tm), BF16)
        out_specs[4] = pl.BlockSpec((None, None, 512, tm), lambda b, t: (b, t, 0, 0))
    return pl.pallas_call(
        functools.partial(_norm_proj_kernel, transpose_v=transpose_v),
        grid=(B, nt),
        in_specs=[pl.BlockSpec((None, tm, D), lambda b, t: (b, t, 0)),
                  pl.BlockSpec((1, D), lambda b, t: (0, 0)),
                  pl.BlockSpec((D, N_QKVU), lambda b, t: (0, 0))],
        out_specs=out_specs,
        out_shape=out_shape,
        compiler_params=_cparams("parallel", "parallel"),
        name="norm_proj",
    )(x, g, w_qkvu)


def _lam_value(lam_ref, lam_init):
    lv = lam_ref[...]
    e1 = jnp.exp(jnp.sum(lv[0:1, :] * lv[1:2, :], axis=-1, keepdims=True))
    e2 = jnp.exp(jnp.sum(lv[2:3, :] * lv[3:4, :], axis=-1, keepdims=True))
    return e1 - e2 + lam_init


def _subln(o, g, lam_init):
    return _rms(o, g) * (1.0 - lam_init)


def _split_halves(q):
    lane = lax.broadcasted_iota(jnp.int32, q.shape, 1)
    zero = jnp.zeros_like(q)
    return jnp.where(lane < 64, q, zero), jnp.where(lane >= 64, q, zero)


def _diff_attn_kernel(lam_ref, q_ref, k_ref, vt_ref, g_ref, o_ref,
                      m1_s, l1_s, a1_s, m2_s, l2_s, a2_s, sa1_s, sa2_s, sb1_s, sb2_s, *, t, lam_init):
    qi = pl.program_id(2)
    q1, q2 = _split_halves(q_ref[...])

    m1_s[...] = jnp.full_like(m1_s, -jnp.inf)
    m2_s[...] = jnp.full_like(m2_s, -jnp.inf)
    l1_s[...] = jnp.zeros_like(l1_s)
    l2_s[...] = jnp.zeros_like(l2_s)
    a1_s[...] = jnp.zeros_like(a1_s)
    a2_s[...] = jnp.zeros_like(a2_s)

    def update(st, vt, m_s, l_s, a_s):
        m_prev = m_s[...]
        m_new = jnp.maximum(m_prev, jnp.max(st, axis=0, keepdims=True))
        alpha = jnp.exp2(m_prev - m_new)
        p = jnp.exp2(st - m_new)
        l_s[...] = alpha * l_s[...] + jnp.sum(p, axis=0, keepdims=True)
        a_s[...] = alpha * a_s[...] + _dot(vt, p.astype(BF16))
        m_s[...] = m_new

    buf_a, buf_b = (sa1_s, sa2_s), (sb1_s, sb2_s)

    def produce(kj, buf):
        k = k_ref[pl.ds(pl.multiple_of(kj * t, t), t), :]
        buf[0][...] = _dot_nt(k, q1)
        buf[1][...] = _dot_nt(k, q2)

    def consume(kj, buf, masked):
        vt = vt_ref[kj]
        s1, s2 = buf[0][...], buf[1][...]
        if masked:
            kc = lax.broadcasted_iota(jnp.int32, (t, t), 0) // CHUNK
            qc = lax.broadcasted_iota(jnp.int32, (t, t), 1) // CHUNK
            keep = kc <= qc
            s1 = jnp.where(keep, s1, NEG)
            s2 = jnp.where(keep, s2, NEG)
        update(s1, vt, m1_s, l1_s, a1_s)
        update(s2, vt, m2_s, l2_s, a2_s)

    produce(0, buf_a)

    def pair(jj, c):
        kj = 2 * jj
        produce(kj + 1, buf_b)
        consume(kj, buf_a, False)
        produce(kj + 2, buf_a)
        consume(kj + 1, buf_b, False)
        return c

    lax.fori_loop(0, qi // 2, pair, 0)

    @pl.when(qi % 2 == 1)
    def _():
        produce(qi, buf_b)
        consume(qi - 1, buf_a, False)
        consume(qi, buf_b, True)

    @pl.when(qi % 2 == 0)
    def _():
        consume(qi, buf_a, True)

    lam = _lam_value(lam_ref, lam_init)
    ot = a1_s[...] / l1_s[...] - lam * (a2_s[...] / l2_s[...])
    ot = ot * lax.rsqrt(jnp.mean(ot * ot, axis=0, keepdims=True) + EPS) * g_ref[...] * (1.0 - lam_init)
    o_ref[...] = ot.T.astype(BF16)


def _diff_attn_prompt(q, k, vt, lam_vecs, subln_g, lam_init):
    B, L, _ = q.shape
    nt, t = vt.shape[1], vt.shape[3]
    kern = functools.partial(_diff_attn_kernel, t=t, lam_init=lam_init)
    return pl.pallas_call(
        kern,
        grid=(B, HA, nt),
        in_specs=[pl.BlockSpec((4, DH_A), lambda b, h, i: (0, 0)),
                  pl.BlockSpec((None, t, DV_A), lambda b, h, i: (b, i, h)),
                  pl.BlockSpec((None, L, DV_A), lambda b, h, i: (b, 0, h)),
                  pl.BlockSpec((None, nt, DV_A, t), lambda b, h, i: (b, 0, h, 0)),
                  pl.BlockSpec((None, DV_A, 1), lambda b, h, i: (h, 0, 0))],
        out_specs=pl.BlockSpec((None, t, DV_A), lambda b, h, i: (b, i, h)),
        out_shape=jax.ShapeDtypeStruct((B, L, W_A), BF16),
        scratch_shapes=[pltpu.VMEM((1, t), F32), pltpu.VMEM((1, t), F32), pltpu.VMEM((DV_A, t), F32),
                        pltpu.VMEM((1, t), F32), pltpu.VMEM((1, t), F32), pltpu.VMEM((DV_A, t), F32)]
                       + [pltpu.VMEM((t, t), F32)] * 4,
        compiler_params=_cparams("parallel", "parallel", "parallel"),
        name="diff_attn",
    )(lam_vecs, q, k, vt, subln_g.reshape(HA, DV_A, 1))


def _diff_attn_sample_kernel(lam_ref, q_ref, ck_ref, cv_ref, k_ref, v_ref, g_ref, o_ref, *, lam_init):
    lam = _lam_value(lam_ref, lam_init)
    for h in range(HA):
        cols = slice(h * DV_A, (h + 1) * DV_A)
        q1, q2 = _split_halves(q_ref[:, cols])
        ck, cv = ck_ref[:, cols], cv_ref[:, cols]
        k, v = k_ref[:, cols], v_ref[:, cols]

        def one_map(qm):
            sc = _dot_nt(qm, ck)
            sn = _dot_nt(qm, k)
            m = jnp.maximum(jnp.max(sc, axis=-1, keepdims=True), jnp.max(sn, axis=-1, keepdims=True))
            pc = jnp.exp2(sc - m)
            pn = jnp.exp2(sn - m)
            l = jnp.sum(pc, axis=-1, keepdims=True) + jnp.sum(pn, axis=-1, keepdims=True)
            return (_dot(pc.astype(BF16), cv) + _dot(pn.astype(BF16), v)) / l

        o = one_map(q1) - lam * one_map(q2)
        o_ref[:, cols] = _subln(o, g_ref[h], lam_init).astype(BF16)


def _diff_attn_sample(q, ck, cv, k, v, lam_vecs, subln_g, lam_init):
    B, T, _ = q.shape
    P = ck.shape[1]
    kern = functools.partial(_diff_attn_sample_kernel, lam_init=lam_init)
    new = pl.BlockSpec((None, T, W_A), lambda b: (b, 0, 0))
    old = pl.BlockSpec((None, P, W_A), lambda b: (b, 0, 0))
    return pl.pallas_call(
        kern,
        grid=(B,),
        in_specs=[pl.BlockSpec((4, DH_A), lambda b: (0, 0)), new, old, old, new, new,
                  pl.BlockSpec((HA, 1, DV_A), lambda b: (0, 0, 0))],
        out_specs=new,
        out_shape=jax.ShapeDtypeStruct((B, T, W_A), BF16),
        compiler_params=_cparams("parallel"),
        name="diff_attn_sample",
    )(lam_vecs, q, ck, cv, k, v, subln_g.reshape(HA, 1, DV_A))


def _rel_bias_kernel(tab_ref, o_ref, *, rows, cols, off0, off_step, masked):
    h = pl.program_id(0)
    d = pl.program_id(1)
    r = lax.broadcasted_iota(jnp.int32, (rows, cols), 0)
    c = lax.broadcasted_iota(jnp.int32, (rows, cols), 1)
    idx = jnp.clip(off0 + d * off_step + r - c, -REL_CLIP, REL_CLIP) + REL_CLIP

    def body(j, acc):
        return jnp.where(idx == j, tab_ref[j * HB + h], acc)

    bias = lax.fori_loop(0, N_REL, body, jnp.zeros((rows, cols), F32))
    if masked:
        dc = d * (off_step // CHUNK) + r // CHUNK - c // CHUNK
        bias = jnp.where((dc >= 0) & (dc <= BAND_CHUNKS), bias, NEG)
    o_ref[...] = bias


def _rel_bias(table, n_off, rows, cols, off0, off_step, masked):
    kern = functools.partial(_rel_bias_kernel, rows=rows, cols=cols, off0=off0,
                             off_step=off_step, masked=masked)
    return pl.pallas_call(
        kern,
        grid=(HB, n_off),
        in_specs=[pl.BlockSpec(memory_space=pltpu.SMEM)],
        out_specs=pl.BlockSpec((None, None, rows, cols), lambda h, d: (h, d, 0, 0)),
        out_shape=jax.ShapeDtypeStruct((HB, n_off, rows, cols), F32),
        compiler_params=_cparams("parallel", "parallel"),
        name="rel_bias",
    )(table.reshape(N_REL * HB))


def _pair_softmax_out(q, ks, vs, bias_fn):
    lane = lax.broadcasted_iota(jnp.int32, (q.shape[0], LANES), 1)
    outs = []
    for hh, qm in enumerate(_split_halves(q)):
        ss = [_dot_nt(qm, k) + bias_fn(hh, j) for j, k in enumerate(ks)]
        m = functools.reduce(jnp.maximum, [jnp.max(s, axis=-1, keepdims=True) for s in ss])
        ps = [jnp.exp(s - m) for s in ss]
        l = functools.reduce(jnp.add, [jnp.sum(p, axis=-1, keepdims=True) for p in ps])
        o = functools.reduce(jnp.add, [_dot(p.astype(BF16), v) for p, v in zip(ps, vs)])
        outs.append(o / l)
    return jnp.where(lane < 64, outs[0], outs[1])


def _band_attn_kernel(q_ref, k0_ref, k1_ref, k2_ref, v0_ref, v1_ref, v2_ref, b_ref, o_ref):
    i = pl.program_id(2)
    ks = [k0_ref[...], k1_ref[...], k2_ref[...]]
    vs = [v0_ref[...], v1_ref[...], v2_ref[...]]

    def bias_fn(hh, d):
        return jnp.where(i >= d, b_ref[hh, d], NEG)

    o_ref[...] = _pair_softmax_out(q_ref[...], ks, vs, bias_fn).astype(BF16)


def _band_attn_prompt(q, k, v, bias):
    B, L, _ = q.shape
    t = 4 * CHUNK
    qspec = pl.BlockSpec((None, t, LANES), lambda b, hp, i: (b, i, hp))
    kspec = lambda d: pl.BlockSpec((None, t, LANES), lambda b, hp, i: (b, jnp.maximum(i - d, 0), hp))
    return pl.pallas_call(
        _band_attn_kernel,
        grid=(B, HB // 2, L // t),
        in_specs=[qspec, kspec(0), kspec(1), kspec(2), kspec(0), kspec(1), kspec(2),
                  pl.BlockSpec((2, 3, t, t), lambda b, hp, i: (hp, 0, 0, 0))],
        out_specs=qspec,
        out_shape=jax.ShapeDtypeStruct((B, L, W_B), BF16),
        compiler_params=_cparams("parallel", "parallel", "parallel"),
        name="band_attn",
    )(q, k, k, k, v, v, v, bias)


def _band_attn_sample_kernel(q_ref, ck_ref, cv_ref, k_ref, v_ref, bc_ref, bn_ref, o_ref):
    for hp in range(HB // 2):
        cols = slice(hp * LANES, (hp + 1) * LANES)
        ks = [ck_ref[:, cols], k_ref[:, cols]]
        vs = [cv_ref[:, cols], v_ref[:, cols]]

        def bias_fn(hh, j, hp=hp):
            return (bc_ref if j == 0 else bn_ref)[2 * hp + hh, 0]

        o_ref[:, cols] = _pair_softmax_out(q_ref[:, cols], ks, vs, bias_fn).astype(BF16)


def _band_attn_sample(q, ck, cv, k, v, bias_c, bias_n):
    B, T, _ = q.shape
    P = ck.shape[1]
    new = pl.BlockSpec((None, T, W_B), lambda b: (b, 0, 0))
    old = pl.BlockSpec((None, P, W_B), lambda b: (b, 0, 0))
    return pl.pallas_call(
        _band_attn_sample_kernel,
        grid=(B,),
        in_specs=[new, old, old, new, new,
                  pl.BlockSpec((HB, 1, T, P), lambda b: (0, 0, 0, 0)),
                  pl.BlockSpec((HB, 1, T, T), lambda b: (0, 0, 0, 0))],
        out_specs=new,
        out_shape=jax.ShapeDtypeStruct((B, T, W_B), BF16),
        compiler_params=_cparams("parallel"),
        name="band_attn_sample",
    )(q, ck, cv, k, v, bias_c, bias_n)


N_SLAB = N_STATE // MXU_DIM


def _ssm_kernel(u_ref, wxr_ref, wxi_ref, ar_ref, ai_ref, h0r_ref, h0i_ref, wcr_ref, wci_ref,
                d_ref, wglu_ref, oc_ref, hro_ref, hio_ref, xr_s, xi_s, hr_s, hi_s, *, tt, nb, half):
    i = pl.program_id(0)

    @pl.when(i == 0)
    def _():
        hr_s[...] = h0r_ref[...]
        hi_s[...] = h0i_ref[...]

    u = u_ref[...]
    ub = u.astype(BF16)
    for j in range(N_SLAB):
        us = ub[:, LANES * (j // 2):LANES * (j // 2 + 1)]
        xr_s[:, MXU_DIM * j:MXU_DIM * (j + 1)] = _dot(us, wxr_ref[j])
        xi_s[:, MXU_DIM * j:MXU_DIM * (j + 1)] = _dot(us, wxi_ref[j])

    for c in range(N_STATE // half):
        cols = slice(c * half, (c + 1) * half)
        ar = jnp.broadcast_to(ar_ref[:, cols], (nb, half))
        ai = jnp.broadcast_to(ai_ref[:, cols], (nb, half))

        def step(t, carry, cols=cols, ar=ar, ai=ai):
            hr, hi = carry
            r0 = pl.multiple_of(t * nb, nb)
            nhr = ar * hr - ai * hi + xr_s[pl.ds(r0, nb), cols]
            nhi = ar * hi + ai * hr + xi_s[pl.ds(r0, nb), cols]
            xr_s[pl.ds(r0, nb), cols] = nhr
            xi_s[pl.ds(r0, nb), cols] = nhi
            return nhr, nhi

        hr, hi = lax.fori_loop(0, tt, step, (hr_s[:, cols], hi_s[:, cols]))
        hr_s[:, cols] = hr
        hi_s[:, cols] = hi

    hro_ref[...] = hr_s[...]
    hio_ref[...] = hi_s[...]

    ys = []
    for s in range(W_C // LANES):
        acc = None
        for j in (2 * s, 2 * s + 1):
            hrb = xr_s[:, MXU_DIM * j:MXU_DIM * (j + 1)].astype(BF16)
            hib = xi_s[:, MXU_DIM * j:MXU_DIM * (j + 1)].astype(BF16)
            part = _dot(hrb, wcr_ref[j]) + _dot(hib, wci_ref[j])
            acc = part if acc is None else acc + part
        ys.append(acc)
    y = jnp.concatenate(ys, axis=-1) + d_ref[...] * u
    ge = 0.5 * y * (1.0 + jnp.tanh(math.sqrt(2.0 / math.pi) * (y + 0.044715 * (y * y * y))))
    gl = _dot(ge.astype(BF16), wglu_ref[...])
    oc_ref[...] = (gl[:, :W_C] * _sigmoid(gl[:, W_C:])).astype(BF16)


def _ssm(u_tm, wxr, wxi, ar, ai, h0r, h0i, wcr, wci, d, wglu, tt):
    n_rows = u_tm.shape[0]
    nb = h0r.shape[0]
    rows = tt * nb
    half = N_STATE // 2
    kern = functools.partial(_ssm_kernel, tt=tt, nb=nb, half=half)
    full = lambda shape: pl.BlockSpec(shape, lambda i: (0,) * len(shape))
    return pl.pallas_call(
        kern,
        grid=(n_rows // rows,),
        in_specs=[pl.BlockSpec((rows, W_C), lambda i: (i, 0)),
                  full((N_SLAB, LANES, MXU_DIM)), full((N_SLAB, LANES, MXU_DIM)),
                  full((1, N_STATE)), full((1, N_STATE)),
                  full((nb, N_STATE)), full((nb, N_STATE)),
                  full((N_SLAB, MXU_DIM, LANES)), full((N_SLAB, MXU_DIM, LANES)),
                  full((1, W_C)), full((W_C, 2 * W_C))],
        out_specs=[pl.BlockSpec((rows, W_C), lambda i: (i, 0)),
                   full((nb, N_STATE)), full((nb, N_STATE))],
        out_shape=[jax.ShapeDtypeStruct((n_rows, W_C), BF16),
                   jax.ShapeDtypeStruct((nb, N_STATE), F32),
                   jax.ShapeDtypeStruct((nb, N_STATE), F32)],
        scratch_shapes=[pltpu.VMEM((rows, N_STATE), F32), pltpu.VMEM((rows, N_STATE), F32),
                        pltpu.VMEM((nb, N_STATE), F32), pltpu.VMEM((nb, N_STATE), F32)],
        compiler_params=_cparams("arbitrary"),
        name="ssm",
    )(u_tm, wxr, wxi, ar, ai, h0r, h0i, wcr, wci, d, wglu)


def _ssm_weights(w):
    ar, ai = w['ssm_a_re'].astype(F32), w['ssm_a_im'].astype(F32)
    dt = jnp.exp(w['ssm_log_dt'].astype(F32))[:, None]
    mag = jnp.exp(ar * dt)
    abar_r, abar_i = mag * jnp.cos(ai * dt), mag * jnp.sin(ai * dt)
    den = ar * ar + ai * ai
    nr, ni = abar_r - 1.0, abar_i
    z_r = (nr * ar + ni * ai) / den
    z_i = (ni * ar - nr * ai) / den
    br, bi = w['ssm_b_re'].astype(F32), w['ssm_b_im'].astype(F32)
    bb_r = z_r[..., None] * br - z_i[..., None] * bi
    bb_i = z_r[..., None] * bi + z_i[..., None] * br

    gps = MXU_DIM // P_STATE
    eye = jnp.eye(gps, dtype=F32)

    def x_slabs(bb):
        b4 = bb.reshape(N_SLAB, gps, P_STATE, GC)
        blk = jnp.einsum('jgpc,gh->jhcgp', b4, eye).reshape(N_SLAB, gps * GC, MXU_DIM)
        zero = jnp.zeros_like(blk)
        even = jnp.concatenate([blk, zero], axis=1)
        odd = jnp.concatenate([zero, blk], axis=1)
        sel = (jnp.arange(N_SLAB) % 2 == 0)[:, None, None]
        return jnp.where(sel, even, odd).astype(BF16)

    def y_slabs(cc):
        c4 = cc.reshape(N_SLAB, gps, GC, P_STATE)
        blk = jnp.einsum('jgcp,gh->jgphc', c4, eye).reshape(N_SLAB, MXU_DIM, gps * GC)
        zero = jnp.zeros_like(blk)
        even = jnp.concatenate([blk, zero], axis=2)
        odd = jnp.concatenate([zero, blk], axis=2)
        sel = (jnp.arange(N_SLAB) % 2 == 0)[:, None, None]
        return jnp.where(sel, even, odd).astype(BF16)

    return dict(wxr=x_slabs(bb_r), wxi=x_slabs(bb_i),
                ar=abar_r.reshape(1, N_STATE), ai=abar_i.reshape(1, N_STATE),
                wcr=y_slabs(w['ssm_c_re'].astype(F32)), wci=y_slabs(-w['ssm_c_im'].astype(F32)))


def _merge_out_kernel(x_ref, oa_ref, ob_ref, oc_ref, g_ref, wg_ref, wa_ref, wb_ref, wc_ref, wo_ref, o_ref):
    x = x_ref[...]
    h = _rms(x, g_ref[...]).astype(BF16)
    merged = None
    for j, (br_ref, w_ref) in enumerate(((oa_ref, wa_ref), (ob_ref, wb_ref), (oc_ref, wc_ref))):
        gate = _sigmoid(_dot(h, wg_ref[:, j * D_MODEL:(j + 1) * D_MODEL]))
        term = gate * _dot(br_ref[...], w_ref[...])
        merged = term if merged is None else merged + term
    o_ref[...] = x + _dot(merged.astype(BF16), wo_ref[...])


def _merge_out(x, oa, ob, oc_tm, g, wg, wa, wb, wc, wo, tm):
    B, L, D = x.shape
    full = lambda shape: pl.BlockSpec(shape, lambda b, t: (0,) * len(shape))
    br = pl.BlockSpec((None, tm, 512), lambda b, t: (b, t, 0))
    return pl.pallas_call(
        _merge_out_kernel,
        grid=(B, L // tm),
        in_specs=[pl.BlockSpec((None, tm, D), lambda b, t: (b, t, 0)), br, br,
                  pl.BlockSpec((tm, 512), lambda b, t: (t, b)),
                  full((1, D)), full((D, 3 * D)), full((W_A, D)), full((W_B, D)), full((W_C, D)), full((D, D))],
        out_specs=pl.BlockSpec((None, tm, D), lambda b, t: (b, t, 0)),
        out_shape=jax.ShapeDtypeStruct((B, L, D), F32),
        compiler_params=_cparams("parallel", "parallel"),
        name="merge_out",
    )(x, oa, ob, oc_tm, g, wg, wa, wb, wc, wo)


FF_CHUNK = 2 * MXU_DIM


def _swiglu(h, w1_ref, w3_ref, w2_ref):
    n_ff = w1_ref.shape[1]
    acc = None
    for c0 in range(0, n_ff, FF_CHUNK):
        c1 = min(c0 + FF_CHUNK, n_ff)
        a = _dot(h, w1_ref[:, c0:c1])
        b = _dot(h, w3_ref[:, c0:c1])
        part = _dot((a * _sigmoid(a) * b).astype(BF16), w2_ref[c0:c1, :])
        acc = part if acc is None else acc + part
    return acc


def _ffn_kernel(x_ref, g_ref, w1_ref, w3_ref, w2_ref, fg_ref, o_ref, *, final_norm):
    x = x_ref[...]
    h = _rms(x, g_ref[...]).astype(BF16)
    y = x + _swiglu(h, w1_ref, w3_ref, w2_ref)
    o_ref[...] = _rms(y, fg_ref[...]) if final_norm else y


def _ffn(x, g, w1, w3, w2, final_g, final_norm, tm):
    B, L, D = x.shape
    full = lambda shape: pl.BlockSpec(shape, lambda b, t: (0,) * len(shape), pipeline_mode=pl.Buffered(1))
    tok = pl.BlockSpec((None, tm, D), lambda b, t: (b, t, 0))
    return pl.pallas_call(
        functools.partial(_ffn_kernel, final_norm=final_norm),
        grid=(B, L // tm),
        in_specs=[tok, full((1, D)), full((D, D_FF)), full((D, D_FF)), full((D_FF, D)), full((1, D))],
        out_specs=tok,
        out_shape=jax.ShapeDtypeStruct((B, L, D), F32),
        compiler_params=_cparams("parallel", "parallel"),
        name="ffn",
    )(x, g, w1, w3, w2, final_g)


def _moe_kernel(x_ref, g_ref, rw_ref, w1_ref, w3_ref, w2_ref, fg_ref, o_ref, h_s, gate_s, acc_s, *, final_norm):
    e = pl.program_id(2)
    lane = lax.broadcasted_iota(jnp.int32, gate_s.shape, 1)

    @pl.when(e == 0)
    def _():
        hf = _rms(x_ref[...], g_ref[...])
        h_s[...] = hf.astype(BF16)
        logits = jnp.dot(hf, rw_ref[...], preferred_element_type=F32, precision=lax.Precision.HIGHEST)
        logits = jnp.where(lane < N_EXPERTS, logits, -jnp.inf)
        v1 = jnp.max(logits, axis=-1, keepdims=True)
        i1 = jnp.min(jnp.where(logits == v1, lane, LANES), axis=-1, keepdims=True)
        rest = jnp.where(lane == i1, -jnp.inf, logits)
        v2 = jnp.max(rest, axis=-1, keepdims=True)
        i2 = jnp.min(jnp.where(rest == v2, lane, LANES), axis=-1, keepdims=True)
        ex = jnp.exp(v2 - v1)
        g1 = 1.0 / (1.0 + ex)
        g2 = ex / (1.0 + ex)
        gate_s[...] = jnp.where(lane == i1, g1, 0.0) + jnp.where(lane == i2, g2, 0.0)
        acc_s[...] = x_ref[...]

    ge = jnp.sum(jnp.where(lane == e, gate_s[...], 0.0), axis=-1, keepdims=True)
    acc_s[...] += ge * _swiglu(h_s[...], w1_ref, w3_ref, w2_ref)

    @pl.when(e == N_EXPERTS - 1)
    def _():
        y = acc_s[...]
        o_ref[...] = _rms(y, fg_ref[...]) if final_norm else y


def _moe(x, g, rw, w1, w3, w2, final_g, final_norm, tm):
    B, L, D = x.shape
    full = lambda shape: pl.BlockSpec(shape, lambda b, t, e: (0,) * len(shape))
    tok = pl.BlockSpec((None, tm, D), lambda b, t, e: (b, t, 0))
    return pl.pallas_call(
        functools.partial(_moe_kernel, final_norm=final_norm),
        grid=(B, L // tm, N_EXPERTS),
        in_specs=[tok, full((1, D)), full((D, LANES)),
                  pl.BlockSpec((None, D, D_FF_E), lambda b, t, e: (e, 0, 0)),
                  pl.BlockSpec((None, D, D_FF_E), lambda b, t, e: (e, 0, 0)),
                  pl.BlockSpec((None, D_FF_E, D), lambda b, t, e: (e, 0, 0)),
                  full((1, D))],
        out_specs=tok,
        out_shape=jax.ShapeDtypeStruct((B, L, D), F32),
        scratch_shapes=[pltpu.VMEM((tm, D), BF16), pltpu.VMEM((tm, LANES), F32), pltpu.VMEM((tm, D), F32)],
        compiler_params=_cparams("parallel", "parallel", "arbitrary"),
        name="moe",
    )(x, g, rw, w1, w3, w2, final_g)


def _pick(n, cands):
    for c in cands:
        if n % c == 0:
            return c
    return n


def _layer(x, l, w, cache, final_g, last):
    B, L, D = x.shape
    lam_init = 0.8 - 0.6 * math.exp(-0.3 * l)
    tm = _pick(L, (512, 256, 128, 64, 32))
    row = lambda a: a.reshape(1, -1).astype(F32)

    w_in = w['w_in'].astype(BF16)
    (qa, ka, va, ka_b, va_b, qb, kb, vb, kb_b, vb_b, u_tm) = _norm_proj(
        x, row(w['norm1_g']), w_in[:, :N_QKVU], tm, cache is None)

    lam_vecs = jnp.stack([w['lam_q1'], w['lam_k1'], w['lam_q2'], w['lam_k2']]).astype(F32)
    table = w['rel_bias'].astype(F32)
    if cache is None:
        oa = _diff_attn_prompt(qa, ka_b, va_b, lam_vecs, w['subln_g'].astype(F32), lam_init)
        bias = _rel_bias(table, 3, 4 * CHUNK, 4 * CHUNK, 0, 4 * CHUNK, True)
        ob = _band_attn_prompt(qb, kb_b, vb_b, bias)
        n_keep = min(BAND, L)
        kb_new, vb_new = kb[:, L - n_keep:], vb[:, L - n_keep:]
        h0r = jnp.zeros((B, N_STATE), F32)
        h0i = jnp.zeros((B, N_STATE), F32)
    else:
        ck_a, cv_a, ck_b, cv_b, h0_re, h0_im = cache
        P = ck_a.shape[1]
        bp = ck_b.shape[1]
        oa = _diff_attn_sample(qa, ck_a.reshape(B, P, W_A).astype(BF16), cv_a.reshape(B, P, W_A).astype(BF16),
                               ka_b, va_b, lam_vecs, w['subln_g'].astype(F32), lam_init)
        bias_c = _rel_bias(table, 1, L, bp, bp, 0, False)
        bias_n = _rel_bias(table, 1, L, L, 0, 0, False)
        ob = _band_attn_sample(qb, ck_b.reshape(B, bp, W_B).astype(BF16), cv_b.reshape(B, bp, W_B).astype(BF16),
                               kb_b, vb_b, bias_c, bias_n)
        kb_new, vb_new = kb, vb
        h0r = h0_re.reshape(B, N_STATE).astype(F32)
        h0i = h0_im.reshape(B, N_STATE).astype(F32)

    sw = _ssm_weights(w)
    oc_tm, hr, hi = _ssm(u_tm.reshape(L * B, W_C), sw['wxr'], sw['wxi'], sw['ar'], sw['ai'], h0r, h0i,
                         sw['wcr'], sw['wci'], row(w['ssm_d']), w['w_glu'].astype(BF16), _pick(L, (64, 32)))

    x = _merge_out(x, oa, ob, oc_tm.reshape(L, B * W_C), row(w['norm1_g']), w_in[:, N_QKVU:],
                   w['w_br_a'].astype(BF16), w['w_br_b'].astype(BF16), w['w_br_c'].astype(BF16),
                   w['w_out'].astype(BF16), tm)

    fg = row(final_g)
    if l % 2 == 0:
        x = _ffn(x, row(w['norm2_g']), w['ffn_w1'].astype(BF16), w['ffn_w3'].astype(BF16),
                 w['ffn_w2'].astype(BF16), fg, last, tm)
    else:
        rw = jnp.pad(w['router_w'].astype(F32), ((0, 0), (0, LANES - N_EXPERTS)))
        x = _moe(x, row(w['norm2_g']), rw, w['moe_w1'].astype(BF16),
                 w['moe_w3'].astype(BF16), w['moe_w2'].astype(BF16), fg, last, tm)

    state = (ka.reshape(B, L, HA, DV_A), va.reshape(B, L, HA, DV_A),
             kb_new.reshape(B, -1, HB, DH_B), vb_new.reshape(B, -1, HB, DH_B),
             hr.reshape(B, N_GROUPS, P_STATE), hi.reshape(B, N_GROUPS, P_STATE))
    return x, state


_PER_LAYER = ('norm1_g', 'w_in', 'lam_q1', 'lam_k1', 'lam_q2', 'lam_k2', 'subln_g', 'rel_bias',
              'ssm_a_re', 'ssm_a_im', 'ssm_log_dt', 'ssm_b_re', 'ssm_b_im', 'ssm_c_re', 'ssm_c_im',
              'ssm_d', 'w_glu', 'w_br_a', 'w_br_b', 'w_br_c', 'w_out', 'norm2_g')


def kernel(x_prompt, x_sample, cache_dk, cache_dv, cache_bk, cache_bv, state_ssm_re, state_ssm_im,
           norm1_g, w_in, lam_q1, lam_k1, lam_q2, lam_k2, subln_g, rel_bias, ssm_a_re, ssm_a_im,
           ssm_log_dt, ssm_b_re, ssm_b_im, ssm_c_re, ssm_c_im, ssm_d, w_glu, w_br_a, w_br_b, w_br_c,
           w_out, norm2_g, ffn_w1, ffn_w3, ffn_w2, router_w, moe_w1, moe_w3, moe_w2, final_g):
    stacked = dict(norm1_g=norm1_g, w_in=w_in, lam_q1=lam_q1, lam_k1=lam_k1, lam_q2=lam_q2, lam_k2=lam_k2,
                   subln_g=subln_g, rel_bias=rel_bias, ssm_a_re=ssm_a_re, ssm_a_im=ssm_a_im,
                   ssm_log_dt=ssm_log_dt, ssm_b_re=ssm_b_re, ssm_b_im=ssm_b_im, ssm_c_re=ssm_c_re,
                   ssm_c_im=ssm_c_im, ssm_d=ssm_d, w_glu=w_glu, w_br_a=w_br_a, w_br_b=w_br_b,
                   w_br_c=w_br_c, w_out=w_out, norm2_g=norm2_g)
    depth = w_in.shape[0]
    yp, ys = x_prompt, x_sample
    st_p, st_s = [], []
    for l in range(depth):
        w = {name: stacked[name][l] for name in _PER_LAYER}
        if l % 2 == 0:
            w['ffn_w1'], w['ffn_w3'], w['ffn_w2'] = ffn_w1[l // 2], ffn_w3[l // 2], ffn_w2[l // 2]
        else:
            w['router_w'], w['moe_w1'] = router_w[l // 2], moe_w1[l // 2]
            w['moe_w3'], w['moe_w2'] = moe_w3[l // 2], moe_w2[l // 2]
        last = l == depth - 1
        yp, sp = _layer(yp, l, w, None, final_g, last)
        ys, ss = _layer(ys, l, w, (cache_dk[l], cache_dv[l], cache_bk[l], cache_bv[l],
                                   state_ssm_re[l], state_ssm_im[l]), final_g, last)
        st_p.append(sp)
        st_s.append(ss)
    outs = [yp, ys]
    for states in (st_p, st_s):
        for j in range(6):
            outs.append(jnp.stack([s[j] for s in states]))
    return tuple(outs)
```

```python
import functools
import math

import jax
import jax.numpy as jnp
from jax import lax
from jax.experimental import pallas as pl
from jax.experimental.pallas import tpu as pltpu

F32 = jnp.float32
BF16 = jnp.bfloat16

D_MODEL = 1024
CHUNK = 64
HA = 4
DH_A = 64
DV_A = 2 * DH_A
W_A = HA * DV_A
HB = 8
DH_B = 64
W_B = HB * DH_B
BAND_CHUNKS = 8
BAND = BAND_CHUNKS * CHUNK
REL_CLIP = 128
N_REL = 2 * REL_CLIP + 1
GC = 16
N_GROUPS = 32
W_C = N_GROUPS * GC
P_STATE = 64
N_STATE = N_GROUPS * P_STATE
D_FF = 11 * D_MODEL // 4
N_EXPERTS = 8
D_FF_E = D_FF // 2
EPS = 1e-6
NEG = -1e30
LOG2E = math.log2(math.e)
N_QKVU = 7 * 512

LANES = 128
SUBLANES = 8
MXU_DIM = 256
VMEM_LIMIT = 56 * 1024 * 1024


def _cparams(*sem):
    return pltpu.CompilerParams(dimension_semantics=sem, vmem_limit_bytes=VMEM_LIMIT)


def _rms(x, g):
    return x * lax.rsqrt(jnp.mean(x * x, axis=-1, keepdims=True) + EPS) * g


def _sigmoid(x):
    return 1.0 / (1.0 + jnp.exp(-x))


def _dot(a, b):
    return jnp.dot(a, b, preferred_element_type=F32)


def _dot_nt(a, b):
    return lax.dot_general(a, b, (((1,), (1,)), ((), ())), preferred_element_type=F32)


def _norm_proj_kernel(*refs, transpose_v, n_alias):
    x_ref, g_ref, w_ref = refs[:3]
    (qa_ref, dk_ref, dv_ref, kab_ref, vab_ref,
     qb_ref, kbl_ref, vbl_ref, kbb_ref, vbb_ref, u_ref) = refs[3 + n_alias:]
    h = _rms(x_ref[...], g_ref[...]).astype(BF16)

    def proj(c):
        return _dot(h, w_ref[:, c * 512:(c + 1) * 512])

    def store_heads(o_ref, y):
        for hd in range(HA):
            o_ref[:, hd, :] = y[:, hd * DV_A:(hd + 1) * DV_A]

    qa_ref[...] = (proj(0) * (DH_A ** -0.5 * LOG2E)).astype(BF16)
    ka = proj(1)
    store_heads(dk_ref, ka)
    kab_ref[...] = ka.astype(BF16)
    va = proj(2)
    store_heads(dv_ref, va)
    vab_ref[...] = (va.T if transpose_v else va).astype(BF16)
    qb_ref[...] = (proj(3) * (DH_B ** -0.5 * LOG2E)).astype(BF16)
    kb = proj(4)
    kbb_ref[...] = kb.astype(BF16)
    vb = proj(5)
    vbb_ref[...] = (vb.T if transpose_v else vb).astype(BF16)
    u_ref[...] = proj(6)

    @pl.when(pl.program_id(1) == pl.num_programs(1) - 1)
    def _():
        kbl_ref[...] = kb
        vbl_ref[...] = vb


def _norm_proj(x, g, w_qkvu, tm, transpose_v, layer, depth, dk, dv):
    B, L, D = x.shape
    nt = L // tm
    tok = jax.ShapeDtypeStruct((B, L, 512), BF16)
    tspec = pl.BlockSpec((None, tm, 512), lambda b, t: (b, t, 0))
    vshape, vspec = tok, tspec
    if transpose_v:
        vshape = jax.ShapeDtypeStruct((B, nt, 512, tm), BF16)
        vspec = pl.BlockSpec((None, None, 512, tm), lambda b, t: (b, t, 0, 0))
    cache = jax.ShapeDtypeStruct((depth, B, L, HA, DV_A), F32)
    cspec = pl.BlockSpec((None, None, tm, HA, DV_A), lambda b, t: (layer, b, t, 0, 0))
    last = jax.ShapeDtypeStruct((B, tm, 512), F32)
    lspec = pl.BlockSpec((None, tm, 512), lambda b, t: (b, 0, 0))
    out_shape = [tok, cache, cache, tok, vshape, tok, last, last, tok, vshape,
                 jax.ShapeDtypeStruct((L, B * 512), F32)]
    out_specs = [tspec, cspec, cspec, tspec, vspec, tspec, lspec, lspec, tspec, vspec,
                 pl.BlockSpec((tm, 512), lambda b, t: (t, b))]
    in_specs = [pl.BlockSpec((None, tm, D), lambda b, t: (b, t, 0)),
                pl.BlockSpec((1, D), lambda b, t: (0, 0)),
                pl.BlockSpec((D, N_QKVU), lambda b, t: (0, 0))]
    args = [x, g, w_qkvu]
    aliases = {}
    if dk is not None:
        in_specs += [pl.BlockSpec(memory_space=pl.ANY)] * 2
        args += [dk, dv]
        aliases = {3: 1, 4: 2}
    outs = pl.pallas_call(
        functools.partial(_norm_proj_kernel, transpose_v=transpose_v, n_alias=len(aliases)),
        grid=(B, nt),
        in_specs=in_specs,
        out_specs=out_specs,
        out_shape=out_shape,
        input_output_aliases=aliases,
        compiler_params=_cparams("parallel", "arbitrary"),
        name="norm_proj",
    )(*args)
    names = ('qa', 'dk', 'dv', 'ka_b', 'va_b', 'qb', 'kb_last', 'vb_last', 'kb_b', 'vb_b', 'u_tm')
    return dict(zip(names, outs))


def _lam_value(lam_ref, lam_init):
    lv = lam_ref[...]
    e1 = jnp.exp(jnp.sum(lv[0:1, :] * lv[1:2, :], axis=-1, keepdims=True))
    e2 = jnp.exp(jnp.sum(lv[2:3, :] * lv[3:4, :], axis=-1, keepdims=True))
    return e1 - e2 + lam_init


def _subln(o, g, lam_init):
    return _rms(o, g) * (1.0 - lam_init)


def _split_halves(q):
    lane = lax.broadcasted_iota(jnp.int32, q.shape, 1)
    zero = jnp.zeros_like(q)
    return jnp.where(lane < 64, q, zero), jnp.where(lane >= 64, q, zero)


def _diff_attn_kernel(lam_ref, q_ref, k_ref, vt_ref, g_ref, o_ref,
                      m1_s, l1_s, a1_s, m2_s, l2_s, a2_s, sa1_s, sa2_s, sb1_s, sb2_s, *, t, lam_init):
    qi = pl.program_id(2)
    q1, q2 = _split_halves(q_ref[...])

    m1_s[...] = jnp.full_like(m1_s, -jnp.inf)
    m2_s[...] = jnp.full_like(m2_s, -jnp.inf)
    l1_s[...] = jnp.zeros_like(l1_s)
    l2_s[...] = jnp.zeros_like(l2_s)
    a1_s[...] = jnp.zeros_like(a1_s)
    a2_s[...] = jnp.zeros_like(a2_s)

    def update(st, vt, m_s, l_s, a_s):
        m_prev = m_s[...]
        m_new = jnp.maximum(m_prev, jnp.max(st, axis=0, keepdims=True))
        alpha = jnp.exp2(m_prev - m_new)
        p = jnp.exp2(st - m_new)
        l_s[...] = alpha * l_s[...] + jnp.sum(p, axis=0, keepdims=True)
        a_s[...] = alpha * a_s[...] + _dot(vt, p.astype(BF16))
        m_s[...] = m_new

    buf_a, buf_b = (sa1_s, sa2_s), (sb1_s, sb2_s)

    def produce(kj, buf):
        k = k_ref[pl.ds(pl.multiple_of(kj * t, t), t), :]
        buf[0][...] = _dot_nt(k, q1)
        buf[1][...] = _dot_nt(k, q2)

    def consume(kj, buf, masked):
        vt = vt_ref[kj]
        s1, s2 = buf[0][...], buf[1][...]
        if masked:
            kc = lax.broadcasted_iota(jnp.int32, (t, t), 0) // CHUNK
            qc = lax.broadcasted_iota(jnp.int32, (t, t), 1) // CHUNK
            keep = kc <= qc
            s1 = jnp.where(keep, s1, NEG)
            s2 = jnp.where(keep, s2, NEG)
        update(s1, vt, m1_s, l1_s, a1_s)
        update(s2, vt, m2_s, l2_s, a2_s)

    produce(0, buf_a)

    def pair(jj, c):
        kj = 2 * jj
        produce(kj + 1, buf_b)
        consume(kj, buf_a, False)
        produce(kj + 2, buf_a)
        consume(kj + 1, buf_b, False)
        return c

    lax.fori_loop(0, qi // 2, pair, 0)

    @pl.when(qi % 2 == 1)
    def _():
        produce(qi, buf_b)
        consume(qi - 1, buf_a, False)
        consume(qi, buf_b, True)

    @pl.when(qi % 2 == 0)
    def _():
        consume(qi, buf_a, True)

    lam = _lam_value(lam_ref, lam_init)
    ot = a1_s[...] / l1_s[...] - lam * (a2_s[...] / l2_s[...])
    ot = ot * lax.rsqrt(jnp.mean(ot * ot, axis=0, keepdims=True) + EPS) * g_ref[...] * (1.0 - lam_init)
    o_ref[...] = ot.T.astype(BF16)


def _diff_attn_prompt(q, k, vt, lam_vecs, subln_g, lam_init):
    B, L, _ = q.shape
    nt, t = vt.shape[1], vt.shape[3]
    kern = functools.partial(_diff_attn_kernel, t=t, lam_init=lam_init)
    return pl.pallas_call(
        kern,
        grid=(B, HA, nt),
        in_specs=[pl.BlockSpec((4, DH_A), lambda b, h, i: (0, 0)),
                  pl.BlockSpec((None, t, DV_A), lambda b, h, i: (b, i, h)),
                  pl.BlockSpec((None, L, DV_A), lambda b, h, i: (b, 0, h)),
                  pl.BlockSpec((None, nt, DV_A, t), lambda b, h, i: (b, 0, h, 0)),
                  pl.BlockSpec((None, DV_A, 1), lambda b, h, i: (h, 0, 0))],
        out_specs=pl.BlockSpec((None, t, DV_A), lambda b, h, i: (b, i, h)),
        out_shape=jax.ShapeDtypeStruct((B, L, W_A), BF16),
        scratch_shapes=[pltpu.VMEM((1, t), F32), pltpu.VMEM((1, t), F32), pltpu.VMEM((DV_A, t), F32),
                        pltpu.VMEM((1, t), F32), pltpu.VMEM((1, t), F32), pltpu.VMEM((DV_A, t), F32)]
                       + [pltpu.VMEM((t, t), F32)] * 4,
        compiler_params=_cparams("parallel", "parallel", "parallel"),
        name="diff_attn",
    )(lam_vecs, q, k, vt, subln_g.reshape(HA, DV_A, 1))


def _diff_attn_sample_kernel(lam_ref, q_ref, ck_ref, cv_ref, k_ref, v_ref, g_ref, o_ref, *, lam_init):
    lam = _lam_value(lam_ref, lam_init)
    for h in range(HA):
        cols = slice(h * DV_A, (h + 1) * DV_A)
        q1, q2 = _split_halves(q_ref[:, cols])
        ck, cv = ck_ref[:, cols], cv_ref[:, cols]
        k, v = k_ref[:, cols], v_ref[:, cols]

        def one_map(qm):
            sc = _dot_nt(qm, ck)
            sn = _dot_nt(qm, k)
            m = jnp.maximum(jnp.max(sc, axis=-1, keepdims=True), jnp.max(sn, axis=-1, keepdims=True))
            pc = jnp.exp2(sc - m)
            pn = jnp.exp2(sn - m)
            l = jnp.sum(pc, axis=-1, keepdims=True) + jnp.sum(pn, axis=-1, keepdims=True)
            return (_dot(pc.astype(BF16), cv) + _dot(pn.astype(BF16), v)) / l

        o = one_map(q1) - lam * one_map(q2)
        o_ref[:, cols] = _subln(o, g_ref[h], lam_init).astype(BF16)


def _diff_attn_sample(q, ck, cv, k, v, lam_vecs, subln_g, lam_init):
    B, T, _ = q.shape
    P = ck.shape[1]
    kern = functools.partial(_diff_attn_sample_kernel, lam_init=lam_init)
    new = pl.BlockSpec((None, T, W_A), lambda b: (b, 0, 0))
    old = pl.BlockSpec((None, P, W_A), lambda b: (b, 0, 0))
    return pl.pallas_call(
        kern,
        grid=(B,),
        in_specs=[pl.BlockSpec((4, DH_A), lambda b: (0, 0)), new, old, old, new, new,
                  pl.BlockSpec((HA, 1, DV_A), lambda b: (0, 0, 0))],
        out_specs=new,
        out_shape=jax.ShapeDtypeStruct((B, T, W_A), BF16),
        compiler_params=_cparams("parallel"),
        name="diff_attn_sample",
    )(lam_vecs, q, ck, cv, k, v, subln_g.reshape(HA, 1, DV_A))


def _rel_bias_kernel(tab_ref, o_ref, *, rows, cols, off0, off_step, masked, transposed):
    h = pl.program_id(0)
    d = pl.program_id(1)
    r = lax.broadcasted_iota(jnp.int32, (rows, cols), 0)
    c = lax.broadcasted_iota(jnp.int32, (rows, cols), 1)
    qi, ki = (c, r) if transposed else (r, c)
    nq, nk = (cols, rows) if transposed else (rows, cols)
    off = off0 + d * off_step
    idx = jnp.clip(off + qi - ki, -REL_CLIP, REL_CLIP) + REL_CLIP
    lo = jnp.clip(off - (nk - 1), -REL_CLIP, REL_CLIP) + REL_CLIP
    hi = jnp.clip(off + (nq - 1), -REL_CLIP, REL_CLIP) + REL_CLIP

    def body(j, acc):
        return jnp.where(idx == j, tab_ref[j * HB + h], acc)

    bias = lax.fori_loop(lo, hi + 1, body, jnp.zeros((rows, cols), F32)) * LOG2E
    if masked:
        dc = d * (off_step // CHUNK) + qi // CHUNK - ki // CHUNK
        bias = jnp.where((dc >= 0) & (dc <= BAND_CHUNKS), bias, NEG)
    o_ref[...] = bias


def _rel_bias(table, n_off, rows, cols, off0, off_step, masked, transposed):
    kern = functools.partial(_rel_bias_kernel, rows=rows, cols=cols, off0=off0,
                             off_step=off_step, masked=masked, transposed=transposed)
    return pl.pallas_call(
        kern,
        grid=(HB, n_off),
        in_specs=[pl.BlockSpec(memory_space=pltpu.SMEM)],
        out_specs=pl.BlockSpec((None, None, rows, cols), lambda h, d: (h, d, 0, 0)),
        out_shape=jax.ShapeDtypeStruct((HB, n_off, rows, cols), F32),
        compiler_params=_cparams("parallel", "parallel"),
        name="rel_bias",
    )(table.reshape(N_REL * HB))


def _pair_softmax_out(q, ks, vs, bias_fn):
    lane = lax.broadcasted_iota(jnp.int32, (q.shape[0], LANES), 1)
    outs = []
    for hh, qm in enumerate(_split_halves(q)):
        ss = [_dot_nt(qm, k) + bias_fn(hh, j) for j, k in enumerate(ks)]
        m = functools.reduce(jnp.maximum, [jnp.max(s, axis=-1, keepdims=True) for s in ss])
        ps = [jnp.exp2(s - m) for s in ss]
        l = functools.reduce(jnp.add, [jnp.sum(p, axis=-1, keepdims=True) for p in ps])
        o = functools.reduce(jnp.add, [_dot(p.astype(BF16), v) for p, v in zip(ps, vs)])
        outs.append(o / l)
    return jnp.where(lane < 64, outs[0], outs[1])


BAND_TILE = 4 * CHUNK
N_BAND_TILES = BAND // BAND_TILE + 1


def _band_attn_kernel(q_ref, k0_ref, k1_ref, k2_ref, v0_ref, v1_ref, v2_ref, b_ref, o_ref, sa_s, sb_s):
    i = pl.program_id(1)
    k_refs = (k0_ref, k1_ref, k2_ref)
    vt_refs = (v0_ref, v1_ref, v2_ref)
    row = lax.broadcasted_iota(jnp.int32, (LANES, BAND_TILE), 0)

    def produce(h, buf):
        cols = slice((h // 2) * LANES, (h // 2 + 1) * LANES)
        qm = _split_halves(q_ref[:, cols])[h % 2]
        for d, k_ref in enumerate(k_refs):
            buf[d] = jnp.where(i >= d, _dot_nt(k_ref[:, cols], qm) + b_ref[h, d], NEG)

    def consume(h, buf):
        cols = slice((h // 2) * LANES, (h // 2 + 1) * LANES)
        ss = [buf[d] for d in range(N_BAND_TILES)]
        m = functools.reduce(jnp.maximum, [jnp.max(s, axis=0, keepdims=True) for s in ss])
        ps = [jnp.exp2(s - m) for s in ss]
        l = functools.reduce(jnp.add, [jnp.sum(p, axis=0, keepdims=True) for p in ps])
        ot = functools.reduce(jnp.add, [_dot(r[cols, :], p.astype(BF16)) for r, p in zip(vt_refs, ps)])
        return ot / l

    bufs = (sa_s, sb_s)
    produce(0, bufs[0])
    prev = None
    for h in range(HB):
        if h + 1 < HB:
            produce(h + 1, bufs[(h + 1) % 2])
        ot = consume(h, bufs[h % 2])
        if h % 2 == 1:
            cols = slice((h // 2) * LANES, (h // 2 + 1) * LANES)
            o_ref[:, cols] = jnp.where(row < DH_B, prev, ot).T.astype(BF16)
        prev = ot


def _band_attn_prompt(q, k, vt, bias_t):
    B, L, _ = q.shape
    t = BAND_TILE
    per = vt.shape[3] // t
    qspec = pl.BlockSpec((None, t, W_B), lambda b, i: (b, i, 0))
    kspec = lambda d: pl.BlockSpec((None, t, W_B), lambda b, i: (b, jnp.maximum(i - d, 0), 0))
    vspec = lambda d: pl.BlockSpec(
        (None, None, W_B, t),
        lambda b, i: (b, jnp.maximum(i - d, 0) // per, 0, jnp.maximum(i - d, 0) % per))
    return pl.pallas_call(
        _band_attn_kernel,
        grid=(B, L // t),
        in_specs=[qspec, kspec(0), kspec(1), kspec(2), vspec(0), vspec(1), vspec(2),
                  pl.BlockSpec((HB, N_BAND_TILES, t, t), lambda b, i: (0, 0, 0, 0),
                               pipeline_mode=pl.Buffered(1))],
        out_specs=qspec,
        out_shape=jax.ShapeDtypeStruct((B, L, W_B), BF16),
        scratch_shapes=[pltpu.VMEM((N_BAND_TILES, t, t), F32)] * 2,
        compiler_params=_cparams("parallel", "parallel"),
        name="band_attn",
    )(q, k, k, k, vt, vt, vt, bias_t)


def _band_attn_sample_kernel(q_ref, ck_ref, cv_ref, k_ref, v_ref, bc_ref, bn_ref, o_ref):
    for hp in range(HB // 2):
        cols = slice(hp * LANES, (hp + 1) * LANES)
        ks = [ck_ref[:, cols], k_ref[:, cols]]
        vs = [cv_ref[:, cols], v_ref[:, cols]]

        def bias_fn(hh, j, hp=hp):
            return (bc_ref if j == 0 else bn_ref)[2 * hp + hh, 0]

        o_ref[:, cols] = _pair_softmax_out(q_ref[:, cols], ks, vs, bias_fn).astype(BF16)


def _band_attn_sample(q, ck, cv, k, v, bias_c, bias_n):
    B, T, _ = q.shape
    P = ck.shape[1]
    new = pl.BlockSpec((None, T, W_B), lambda b: (b, 0, 0))
    old = pl.BlockSpec((None, P, W_B), lambda b: (b, 0, 0))
    return pl.pallas_call(
        _band_attn_sample_kernel,
        grid=(B,),
        in_specs=[new, old, old, new, new,
                  pl.BlockSpec((HB, 1, T, P), lambda b: (0, 0, 0, 0)),
                  pl.BlockSpec((HB, 1, T, T), lambda b: (0, 0, 0, 0))],
        out_specs=new,
        out_shape=jax.ShapeDtypeStruct((B, T, W_B), BF16),
        compiler_params=_cparams("parallel"),
        name="band_attn_sample",
    )(q, ck, cv, k, v, bias_c, bias_n)


N_SLAB = N_STATE // MXU_DIM


def _ssm_kernel(u_ref, wxr_ref, wxi_ref, ar_ref, ai_ref, h0r_ref, h0i_ref, wcr_ref, wci_ref,
                d_ref, wglu_ref, oc_ref, hro_ref, hio_ref, xr_s, xi_s, hr_s, hi_s, *, tt, nb, half):
    i = pl.program_id(0)

    @pl.when(i == 0)
    def _():
        hr_s[...] = h0r_ref[...]
        hi_s[...] = h0i_ref[...]

    u = u_ref[...]
    ub = u.astype(BF16)
    for j in range(N_SLAB):
        us = ub[:, LANES * (j // 2):LANES * (j // 2 + 1)]
        xr_s[:, MXU_DIM * j:MXU_DIM * (j + 1)] = _dot(us, wxr_ref[j])
        xi_s[:, MXU_DIM * j:MXU_DIM * (j + 1)] = _dot(us, wxi_ref[j])

    for c in range(N_STATE // half):
        cols = slice(c * half, (c + 1) * half)
        ar = jnp.broadcast_to(ar_ref[:, cols], (nb, half))
        ai = jnp.broadcast_to(ai_ref[:, cols], (nb, half))

        def step(t, carry, cols=cols, ar=ar, ai=ai):
            hr, hi = carry
            r0 = pl.multiple_of(t * nb, nb)
            nhr = ar * hr - ai * hi + xr_s[pl.ds(r0, nb), cols]
            nhi = ar * hi + ai * hr + xi_s[pl.ds(r0, nb), cols]
            xr_s[pl.ds(r0, nb), cols] = nhr
            xi_s[pl.ds(r0, nb), cols] = nhi
            return nhr, nhi

        hr, hi = lax.fori_loop(0, tt, step, (hr_s[:, cols], hi_s[:, cols]))
        hr_s[:, cols] = hr
        hi_s[:, cols] = hi

    hro_ref[...] = hr_s[...]
    hio_ref[...] = hi_s[...]

    ys = []
    for s in range(W_C // LANES):
        acc = None
        for j in (2 * s, 2 * s + 1):
            hrb = xr_s[:, MXU_DIM * j:MXU_DIM * (j + 1)].astype(BF16)
            hib = xi_s[:, MXU_DIM * j:MXU_DIM * (j + 1)].astype(BF16)
            part = _dot(hrb, wcr_ref[j]) + _dot(hib, wci_ref[j])
            acc = part if acc is None else acc + part
        ys.append(acc)
    y = jnp.concatenate(ys, axis=-1) + d_ref[...] * u
    ge = 0.5 * y * (1.0 + jnp.tanh(math.sqrt(2.0 / math.pi) * (y + 0.044715 * (y * y * y))))
    gl = _dot(ge.astype(BF16), wglu_ref[...])
    oc_ref[...] = (gl[:, :W_C] * _sigmoid(gl[:, W_C:])).astype(BF16)


def _ssm(u_tm, wxr, wxi, ar, ai, h0r, h0i, wcr, wci, d, wglu, tt):
    n_rows = u_tm.shape[0]
    nb = h0r.shape[0]
    rows = tt * nb
    half = N_STATE // 2
    kern = functools.partial(_ssm_kernel, tt=tt, nb=nb, half=half)
    full = lambda shape: pl.BlockSpec(shape, lambda i: (0,) * len(shape))
    return pl.pallas_call(
        kern,
        grid=(n_rows // rows,),
        in_specs=[pl.BlockSpec((rows, W_C), lambda i: (i, 0)),
                  full((N_SLAB, LANES, MXU_DIM)), full((N_SLAB, LANES, MXU_DIM)),
                  full((1, N_STATE)), full((1, N_STATE)),
                  full((nb, N_STATE)), full((nb, N_STATE)),
                  full((N_SLAB, MXU_DIM, LANES)), full((N_SLAB, MXU_DIM, LANES)),
                  full((1, W_C)), full((W_C, 2 * W_C))],
        out_specs=[pl.BlockSpec((rows, W_C), lambda i: (i, 0)),
                   full((nb, N_STATE)), full((nb, N_STATE))],
        out_shape=[jax.ShapeDtypeStruct((n_rows, W_C), BF16),
                   jax.ShapeDtypeStruct((nb, N_STATE), F32),
                   jax.ShapeDtypeStruct((nb, N_STATE), F32)],
        scratch_shapes=[pltpu.VMEM((rows, N_STATE), F32), pltpu.VMEM((rows, N_STATE), F32),
                        pltpu.VMEM((nb, N_STATE), F32), pltpu.VMEM((nb, N_STATE), F32)],
        compiler_params=_cparams("arbitrary"),
        name="ssm",
    )(u_tm, wxr, wxi, ar, ai, h0r, h0i, wcr, wci, d, wglu)


def _ssm_weights(w):
    ar, ai = w['ssm_a_re'].astype(F32), w['ssm_a_im'].astype(F32)
    dt = jnp.exp(w['ssm_log_dt'].astype(F32))[:, None]
    mag = jnp.exp(ar * dt)
    abar_r, abar_i = mag * jnp.cos(ai * dt), mag * jnp.sin(ai * dt)
    den = ar * ar + ai * ai
    nr, ni = abar_r - 1.0, abar_i
    z_r = (nr * ar + ni * ai) / den
    z_i = (ni * ar - nr * ai) / den
    br, bi = w['ssm_b_re'].astype(F32), w['ssm_b_im'].astype(F32)
    bb_r = z_r[..., None] * br - z_i[..., None] * bi
    bb_i = z_r[..., None] * bi + z_i[..., None] * br

    gps = MXU_DIM // P_STATE
    eye = jnp.eye(gps, dtype=F32)

    def x_slabs(bb):
        b4 = bb.reshape(N_SLAB, gps, P_STATE, GC)
        blk = jnp.einsum('jgpc,gh->jhcgp', b4, eye).reshape(N_SLAB, gps * GC, MXU_DIM)
        zero = jnp.zeros_like(blk)
        even = jnp.concatenate([blk, zero], axis=1)
        odd = jnp.concatenate([zero, blk], axis=1)
        sel = (jnp.arange(N_SLAB) % 2 == 0)[:, None, None]
        return jnp.where(sel, even, odd).astype(BF16)

    def y_slabs(cc):
        c4 = cc.reshape(N_SLAB, gps, GC, P_STATE)
        blk = jnp.einsum('jgcp,gh->jgphc', c4, eye).reshape(N_SLAB, MXU_DIM, gps * GC)
        zero = jnp.zeros_like(blk)
        even = jnp.concatenate([blk, zero], axis=2)
        odd = jnp.concatenate([zero, blk], axis=2)
        sel = (jnp.arange(N_SLAB) % 2 == 0)[:, None, None]
        return jnp.where(sel, even, odd).astype(BF16)

    return dict(wxr=x_slabs(bb_r), wxi=x_slabs(bb_i),
                ar=abar_r.reshape(1, N_STATE), ai=abar_i.reshape(1, N_STATE),
                wcr=y_slabs(w['ssm_c_re'].astype(F32)), wci=y_slabs(-w['ssm_c_im'].astype(F32)))


def _merge_out_kernel(x_ref, oa_ref, ob_ref, oc_ref, g_ref, wg_ref, wa_ref, wb_ref, wc_ref, wo_ref, o_ref):
    x = x_ref[...]
    h = _rms(x, g_ref[...]).astype(BF16)
    merged = None
    for j, (br_ref, w_ref) in enumerate(((oa_ref, wa_ref), (ob_ref, wb_ref), (oc_ref, wc_ref))):
        gate = _sigmoid(_dot(h, wg_ref[:, j * D_MODEL:(j + 1) * D_MODEL]))
        term = gate * _dot(br_ref[...], w_ref[...])
        merged = term if merged is None else merged + term
    o_ref[...] = x + _dot(merged.astype(BF16), wo_ref[...])


def _merge_out(x, oa, ob, oc_tm, g, wg, wa, wb, wc, wo, tm):
    B, L, D = x.shape
    full = lambda shape: pl.BlockSpec(shape, lambda b, t: (0,) * len(shape))
    br = pl.BlockSpec((None, tm, 512), lambda b, t: (b, t, 0))
    return pl.pallas_call(
        _merge_out_kernel,
        grid=(B, L // tm),
        in_specs=[pl.BlockSpec((None, tm, D), lambda b, t: (b, t, 0)), br, br,
                  pl.BlockSpec((tm, 512), lambda b, t: (t, b)),
                  full((1, D)), full((D, 3 * D)), full((W_A, D)), full((W_B, D)), full((W_C, D)), full((D, D))],
        out_specs=pl.BlockSpec((None, tm, D), lambda b, t: (b, t, 0)),
        out_shape=jax.ShapeDtypeStruct((B, L, D), F32),
        compiler_params=_cparams("parallel", "parallel"),
        name="merge_out",
    )(x, oa, ob, oc_tm, g, wg, wa, wb, wc, wo)


FF_CHUNK = 2 * MXU_DIM


def _swiglu(h, w1_ref, w3_ref, w2_ref):
    n_ff = w1_ref.shape[1]
    acc = None
    for c0 in range(0, n_ff, FF_CHUNK):
        c1 = min(c0 + FF_CHUNK, n_ff)
        a = _dot(h, w1_ref[:, c0:c1])
        b = _dot(h, w3_ref[:, c0:c1])
        part = _dot((a * _sigmoid(a) * b).astype(BF16), w2_ref[c0:c1, :])
        acc = part if acc is None else acc + part
    return acc


def _ffn_kernel(x_ref, g_ref, w1_ref, w3_ref, w2_ref, fg_ref, o_ref, *, final_norm):
    x = x_ref[...]
    h = _rms(x, g_ref[...]).astype(BF16)
    y = x + _swiglu(h, w1_ref, w3_ref, w2_ref)
    o_ref[...] = _rms(y, fg_ref[...]) if final_norm else y


def _ffn(x, g, w1, w3, w2, final_g, final_norm, tm):
    B, L, D = x.shape
    full = lambda shape: pl.BlockSpec(shape, lambda b, t: (0,) * len(shape), pipeline_mode=pl.Buffered(1))
    tok = pl.BlockSpec((None, tm, D), lambda b, t: (b, t, 0))
    return pl.pallas_call(
        functools.partial(_ffn_kernel, final_norm=final_norm),
        grid=(B, L // tm),
        in_specs=[tok, full((1, D)), full((D, D_FF)), full((D, D_FF)), full((D_FF, D)), full((1, D))],
        out_specs=tok,
        out_shape=jax.ShapeDtypeStruct((B, L, D), F32),
        compiler_params=_cparams("parallel", "parallel"),
        name="ffn",
    )(x, g, w1, w3, w2, final_g)


def _moe_kernel(x_ref, g_ref, rw_ref, w1_ref, w3_ref, w2_ref, fg_ref, o_ref, h_s, gate_s, acc_s, *, final_norm):
    e = pl.program_id(2)
    lane = lax.broadcasted_iota(jnp.int32, gate_s.shape, 1)

    @pl.when(e == 0)
    def _():
        hf = _rms(x_ref[...], g_ref[...])
        h_s[...] = hf.astype(BF16)
        logits = jnp.dot(hf, rw_ref[...], preferred_element_type=F32, precision=lax.Precision.HIGHEST)
        logits = jnp.where(lane < N_EXPERTS, logits, -jnp.inf)
        v1 = jnp.max(logits, axis=-1, keepdims=True)
        i1 = jnp.min(jnp.where(logits == v1, lane, LANES), axis=-1, keepdims=True)
        rest = jnp.where(lane == i1, -jnp.inf, logits)
        v2 = jnp.max(rest, axis=-1, keepdims=True)
        i2 = jnp.min(jnp.where(rest == v2, lane, LANES), axis=-1, keepdims=True)
        ex = jnp.exp(v2 - v1)
        g1 = 1.0 / (1.0 + ex)
        g2 = ex / (1.0 + ex)
        gate_s[...] = jnp.where(lane == i1, g1, 0.0) + jnp.where(lane == i2, g2, 0.0)
        acc_s[...] = x_ref[...]

    ge = jnp.sum(jnp.where(lane == e, gate_s[...], 0.0), axis=-1, keepdims=True)
    acc_s[...] += ge * _swiglu(h_s[...], w1_ref, w3_ref, w2_ref)

    @pl.when(e == N_EXPERTS - 1)
    def _():
        y = acc_s[...]
        o_ref[...] = _rms(y, fg_ref[...]) if final_norm else y


def _moe(x, g, rw, w1, w3, w2, final_g, final_norm, tm):
    B, L, D = x.shape
    full = lambda shape: pl.BlockSpec(shape, lambda b, t, e: (0,) * len(shape))
    tok = pl.BlockSpec((None, tm, D), lambda b, t, e: (b, t, 0))
    return pl.pallas_call(
        functools.partial(_moe_kernel, final_norm=final_norm),
        grid=(B, L // tm, N_EXPERTS),
        in_specs=[tok, full((1, D)), full((D, LANES)),
                  pl.BlockSpec((None, D, D_FF_E), lambda b, t, e: (e, 0, 0)),
                  pl.BlockSpec((None, D, D_FF_E), lambda b, t, e: (e, 0, 0)),
                  pl.BlockSpec((None, D_FF_E, D), lambda b, t, e: (e, 0, 0)),
                  full((1, D))],
        out_specs=tok,
        out_shape=jax.ShapeDtypeStruct((B, L, D), F32),
        scratch_shapes=[pltpu.VMEM((tm, D), BF16), pltpu.VMEM((tm, LANES), F32), pltpu.VMEM((tm, D), F32)],
        compiler_params=_cparams("parallel", "parallel", "arbitrary"),
        name="moe",
    )(x, g, rw, w1, w3, w2, final_g)


def _pick(n, cands):
    for c in cands:
        if n % c == 0:
            return c
    return n


def _layer(x, l, w, cache, final_g, last, dkv, depth):
    B, L, D = x.shape
    assert B == SUBLANES, "the SSM kernel puts the streams of one time step on the 8 sublanes"
    lam_init = 0.8 - 0.6 * math.exp(-0.3 * l)
    tm = _pick(L, (512, 256, 128, 64, 32))
    n_keep = min(BAND, L)
    assert n_keep == tm, "the band cache rows must be exactly the last token tile"
    row = lambda a: a.reshape(1, -1).astype(F32)
    prompt = cache is None

    w_in = w['w_in'].astype(BF16)
    dk, dv = dkv if dkv is not None else (None, None)
    p = _norm_proj(x, row(w['norm1_g']), w_in[:, :N_QKVU], tm, prompt, l, depth, dk, dv)

    lam_vecs = jnp.stack([w['lam_q1'], w['lam_k1'], w['lam_q2'], w['lam_k2']]).astype(F32)
    subln_g = w['subln_g'].astype(F32)
    table = w['rel_bias'].astype(F32)
    if prompt:
        oa = _diff_attn_prompt(p['qa'], p['ka_b'], p['va_b'], lam_vecs, subln_g, lam_init)
        bias_t = _rel_bias(table, N_BAND_TILES, BAND_TILE, BAND_TILE, 0, BAND_TILE, True, True)
        ob = _band_attn_prompt(p['qb'], p['kb_b'], p['vb_b'], bias_t)
        h0r = jnp.zeros((B, N_STATE), F32)
        h0i = jnp.zeros((B, N_STATE), F32)
    else:
        ck_a, cv_a, ck_b, cv_b, h0_re, h0_im = cache
        P = ck_a.shape[1]
        bp = ck_b.shape[1]
        oa = _diff_attn_sample(p['qa'], ck_a.reshape(B, P, W_A).astype(BF16), cv_a.reshape(B, P, W_A).astype(BF16),
                               p['ka_b'], p['va_b'], lam_vecs, subln_g, lam_init)
        bias_c = _rel_bias(table, 1, L, bp, bp, 0, False, False)
        bias_n = _rel_bias(table, 1, L, L, 0, 0, False, False)
        ob = _band_attn_sample(p['qb'], ck_b.reshape(B, bp, W_B).astype(BF16), cv_b.reshape(B, bp, W_B).astype(BF16),
                               p['kb_b'], p['vb_b'], bias_c, bias_n)
        h0r = h0_re.reshape(B, N_STATE).astype(F32)
        h0i = h0_im.reshape(B, N_STATE).astype(F32)

    sw = _ssm_weights(w)
    oc_tm, hr, hi = _ssm(p['u_tm'].reshape(L * B, W_C), sw['wxr'], sw['wxi'], sw['ar'], sw['ai'], h0r, h0i,
                         sw['wcr'], sw['wci'], row(w['ssm_d']), w['w_glu'].astype(BF16), _pick(L, (64, 32)))

    x = _merge_out(x, oa, ob, oc_tm.reshape(L, B * W_C), row(w['norm1_g']), w_in[:, N_QKVU:],
                   w['w_br_a'].astype(BF16), w['w_br_b'].astype(BF16), w['w_br_c'].astype(BF16),
                   w['w_out'].astype(BF16), tm)

    fg = row(final_g)
    if l % 2 == 0:
        x = _ffn(x, row(w['norm2_g']), w['ffn_w1'].astype(BF16), w['ffn_w3'].astype(BF16),
                 w['ffn_w2'].astype(BF16), fg, last, tm)
    else:
        rw = jnp.pad(w['router_w'].astype(F32), ((0, 0), (0, LANES - N_EXPERTS)))
        x = _moe(x, row(w['norm2_g']), rw, w['moe_w1'].astype(BF16),
                 w['moe_w3'].astype(BF16), w['moe_w2'].astype(BF16), fg, last, tm)

    small = (p['kb_last'].reshape(B, n_keep, HB, DH_B), p['vb_last'].reshape(B, n_keep, HB, DH_B),
             hr.reshape(B, N_GROUPS, P_STATE), hi.reshape(B, N_GROUPS, P_STATE))
    return x, (p['dk'], p['dv']), small


_PER_LAYER = ('norm1_g', 'w_in', 'lam_q1', 'lam_k1', 'lam_q2', 'lam_k2', 'subln_g', 'rel_bias',
              'ssm_a_re', 'ssm_a_im', 'ssm_log_dt', 'ssm_b_re', 'ssm_b_im', 'ssm_c_re', 'ssm_c_im',
              'ssm_d', 'w_glu', 'w_br_a', 'w_br_b', 'w_br_c', 'w_out', 'norm2_g')


def kernel(x_prompt, x_sample, cache_dk, cache_dv, cache_bk, cache_bv, state_ssm_re, state_ssm_im,
           norm1_g, w_in, lam_q1, lam_k1, lam_q2, lam_k2, subln_g, rel_bias, ssm_a_re, ssm_a_im,
           ssm_log_dt, ssm_b_re, ssm_b_im, ssm_c_re, ssm_c_im, ssm_d, w_glu, w_br_a, w_br_b, w_br_c,
           w_out, norm2_g, ffn_w1, ffn_w3, ffn_w2, router_w, moe_w1, moe_w3, moe_w2, final_g):
    stacked = dict(norm1_g=norm1_g, w_in=w_in, lam_q1=lam_q1, lam_k1=lam_k1, lam_q2=lam_q2, lam_k2=lam_k2,
                   subln_g=subln_g, rel_bias=rel_bias, ssm_a_re=ssm_a_re, ssm_a_im=ssm_a_im,
                   ssm_log_dt=ssm_log_dt, ssm_b_re=ssm_b_re, ssm_b_im=ssm_b_im, ssm_c_re=ssm_c_re,
                   ssm_c_im=ssm_c_im, ssm_d=ssm_d, w_glu=w_glu, w_br_a=w_br_a, w_br_b=w_br_b,
                   w_br_c=w_br_c, w_out=w_out, norm2_g=norm2_g)
    depth = w_in.shape[0]
    yp, ys = x_prompt, x_sample
    dkv_p, dkv_s = None, None
    small_p, small_s = [], []
    for l in range(depth):
        w = {name: stacked[name][l] for name in _PER_LAYER}
        if l % 2 == 0:
            w['ffn_w1'], w['ffn_w3'], w['ffn_w2'] = ffn_w1[l // 2], ffn_w3[l // 2], ffn_w2[l // 2]
        else:
            w['router_w'], w['moe_w1'] = router_w[l // 2], moe_w1[l // 2]
            w['moe_w3'], w['moe_w2'] = moe_w3[l // 2], moe_w2[l // 2]
        last = l == depth - 1
        yp, dkv_p, sp = _layer(yp, l, w, None, final_g, last, dkv_p, depth)
        ys, dkv_s, ss = _layer(ys, l, w, (cache_dk[l], cache_dv[l], cache_bk[l], cache_bv[l],
                                          state_ssm_re[l], state_ssm_im[l]), final_g, last, dkv_s, depth)
        small_p.append(sp)
        small_s.append(ss)
    outs = [yp, ys]
    for dkv, small in ((dkv_p, small_p), (dkv_s, small_s)):
        outs += list(dkv)
        for j in range(4):
            outs.append(jnp.stack([s[j] for s in small]))
    return tuple(outs)
```

```python
import functools
import math

import jax
import jax.numpy as jnp
from jax import lax
from jax.experimental import pallas as pl
from jax.experimental.pallas import tpu as pltpu

F32 = jnp.float32
BF16 = jnp.bfloat16

D_MODEL = 1024
CHUNK = 64
HA = 4
DH_A = 64
DV_A = 2 * DH_A
W_A = HA * DV_A
HB = 8
DH_B = 64
W_B = HB * DH_B
BAND_CHUNKS = 8
BAND = BAND_CHUNKS * CHUNK
REL_CLIP = 128
N_REL = 2 * REL_CLIP + 1
GC = 16
N_GROUPS = 32
W_C = N_GROUPS * GC
P_STATE = 64
N_STATE = N_GROUPS * P_STATE
D_FF = 11 * D_MODEL // 4
N_EXPERTS = 8
D_FF_E = D_FF // 2
EPS = 1e-6
NEG = -1e30
LOG2E = math.log2(math.e)
N_QKVU = 7 * 512

LANES = 128
SUBLANES = 8
MXU_DIM = 256
VMEM_LIMIT = 56 * 1024 * 1024


def _cparams(*sem):
    return pltpu.CompilerParams(dimension_semantics=sem, vmem_limit_bytes=VMEM_LIMIT)


def _rms(x, g):
    return x * lax.rsqrt(jnp.mean(x * x, axis=-1, keepdims=True) + EPS) * g


def _sigmoid(x):
    return 1.0 / (1.0 + jnp.exp(-x))


def _dot(a, b):
    return jnp.dot(a, b, preferred_element_type=F32)


def _dot_nt(a, b):
    return lax.dot_general(a, b, (((1,), (1,)), ((), ())), preferred_element_type=F32)


def _norm_proj_kernel(*refs, transpose_v, n_alias):
    x_ref, g_ref, w_ref = refs[:3]
    (qa_ref, dk_ref, dv_ref, kab_ref, vab_ref,
     qb_ref, kbl_ref, vbl_ref, kbb_ref, vbb_ref, u_ref) = refs[3 + n_alias:]
    h = _rms(x_ref[...], g_ref[...]).astype(BF16)

    def proj(c):
        return _dot(h, w_ref[:, c * 512:(c + 1) * 512])

    def store_heads(o_ref, y):
        for hd in range(HA):
            o_ref[:, hd, :] = y[:, hd * DV_A:(hd + 1) * DV_A]

    qa_ref[...] = (proj(0) * (DH_A ** -0.5 * LOG2E)).astype(BF16)
    ka = proj(1)
    store_heads(dk_ref, ka)
    kab_ref[...] = ka.astype(BF16)
    va = proj(2)
    store_heads(dv_ref, va)
    vab_ref[...] = (va.T if transpose_v else va).astype(BF16)
    qb_ref[...] = (proj(3) * (DH_B ** -0.5 * LOG2E)).astype(BF16)
    kb = proj(4)
    kbb_ref[...] = kb.astype(BF16)
    vb = proj(5)
    vbb_ref[...] = (vb.T if transpose_v else vb).astype(BF16)
    u_ref[...] = proj(6)

    @pl.when(pl.program_id(1) == pl.num_programs(1) - 1)
    def _():
        kbl_ref[...] = kb
        vbl_ref[...] = vb


def _norm_proj(x, g, w_qkvu, tm, transpose_v, layer, depth, dk, dv):
    B, L, D = x.shape
    nt = L // tm
    tok = jax.ShapeDtypeStruct((B, L, 512), BF16)
    tspec = pl.BlockSpec((None, tm, 512), lambda b, t: (b, t, 0))
    vshape, vspec = tok, tspec
    if transpose_v:
        vshape = jax.ShapeDtypeStruct((B, nt, 512, tm), BF16)
        vspec = pl.BlockSpec((None, None, 512, tm), lambda b, t: (b, t, 0, 0))
    cache = jax.ShapeDtypeStruct((depth, B, L, HA, DV_A), F32)
    cspec = pl.BlockSpec((None, None, tm, HA, DV_A), lambda b, t: (layer, b, t, 0, 0))
    last = jax.ShapeDtypeStruct((B, tm, 512), F32)
    lspec = pl.BlockSpec((None, tm, 512), lambda b, t: (b, 0, 0))
    out_shape = [tok, cache, cache, tok, vshape, tok, last, last, tok, vshape,
                 jax.ShapeDtypeStruct((L, B * 512), F32)]
    out_specs = [tspec, cspec, cspec, tspec, vspec, tspec, lspec, lspec, tspec, vspec,
                 pl.BlockSpec((tm, 512), lambda b, t: (t, b))]
    in_specs = [pl.BlockSpec((None, tm, D), lambda b, t: (b, t, 0)),
                pl.BlockSpec((1, D), lambda b, t: (0, 0)),
                pl.BlockSpec((D, N_QKVU), lambda b, t: (0, 0))] + [pl.BlockSpec(memory_space=pl.ANY)] * 2
    if dk is None:
        dk, dv = jnp.zeros(cache.shape, F32), jnp.zeros(cache.shape, F32)
    aliases = {3: 1, 4: 2}
    outs = pl.pallas_call(
        functools.partial(_norm_proj_kernel, transpose_v=transpose_v, n_alias=len(aliases)),
        grid=(B, nt),
        in_specs=in_specs,
        out_specs=out_specs,
        out_shape=out_shape,
        input_output_aliases=aliases,
        compiler_params=_cparams("parallel", "arbitrary"),
        name="norm_proj",
    )(x, g, w_qkvu, dk, dv)
    names = ('qa', 'dk', 'dv', 'ka_b', 'va_b', 'qb', 'kb_last', 'vb_last', 'kb_b', 'vb_b', 'u_tm')
    return dict(zip(names, outs))


def _lam_value(lam_ref, lam_init):
    lv = lam_ref[...]
    e1 = jnp.exp(jnp.sum(lv[0:1, :] * lv[1:2, :], axis=-1, keepdims=True))
    e2 = jnp.exp(jnp.sum(lv[2:3, :] * lv[3:4, :], axis=-1, keepdims=True))
    return e1 - e2 + lam_init


def _subln(o, g, lam_init):
    return _rms(o, g) * (1.0 - lam_init)


def _split_halves(q):
    lane = lax.broadcasted_iota(jnp.int32, q.shape, 1)
    zero = jnp.zeros_like(q)
    return jnp.where(lane < 64, q, zero), jnp.where(lane >= 64, q, zero)


def _diff_attn_kernel(lam_ref, q_ref, k_ref, vt_ref, g_ref, o_ref,
                      m1_s, l1_s, a1_s, m2_s, l2_s, a2_s, sa1_s, sa2_s, sb1_s, sb2_s, *, t, lam_init):
    qi = pl.program_id(2)
    q1, q2 = _split_halves(q_ref[...])

    m1_s[...] = jnp.full_like(m1_s, -jnp.inf)
    m2_s[...] = jnp.full_like(m2_s, -jnp.inf)
    l1_s[...] = jnp.zeros_like(l1_s)
    l2_s[...] = jnp.zeros_like(l2_s)
    a1_s[...] = jnp.zeros_like(a1_s)
    a2_s[...] = jnp.zeros_like(a2_s)

    def update(st, vt, m_s, l_s, a_s):
        m_prev = m_s[...]
        m_new = jnp.maximum(m_prev, jnp.max(st, axis=0, keepdims=True))
        alpha = jnp.exp2(m_prev - m_new)
        p = jnp.exp2(st - m_new)
        l_s[...] = alpha * l_s[...] + jnp.sum(p, axis=0, keepdims=True)
        a_s[...] = alpha * a_s[...] + _dot(vt, p.astype(BF16))
        m_s[...] = m_new

    buf_a, buf_b = (sa1_s, sa2_s), (sb1_s, sb2_s)

    def produce(kj, buf):
        k = k_ref[pl.ds(pl.multiple_of(kj * t, t), t), :]
        buf[0][...] = _dot_nt(k, q1)
        buf[1][...] = _dot_nt(k, q2)

    def consume(kj, buf, masked):
        vt = vt_ref[kj]
        s1, s2 = buf[0][...], buf[1][...]
        if masked:
            kc = lax.broadcasted_iota(jnp.int32, (t, t), 0) // CHUNK
            qc = lax.broadcasted_iota(jnp.int32, (t, t), 1) // CHUNK
            keep = kc <= qc
            s1 = jnp.where(keep, s1, NEG)
            s2 = jnp.where(keep, s2, NEG)
        update(s1, vt, m1_s, l1_s, a1_s)
        update(s2, vt, m2_s, l2_s, a2_s)

    produce(0, buf_a)

    def pair(jj, c):
        kj = 2 * jj
        produce(kj + 1, buf_b)
        consume(kj, buf_a, False)
        produce(kj + 2, buf_a)
        consume(kj + 1, buf_b, False)
        return c

    lax.fori_loop(0, qi // 2, pair, 0)

    @pl.when(qi % 2 == 1)
    def _():
        produce(qi, buf_b)
        consume(qi - 1, buf_a, False)
        consume(qi, buf_b, True)

    @pl.when(qi % 2 == 0)
    def _():
        consume(qi, buf_a, True)

    lam = _lam_value(lam_ref, lam_init)
    ot = a1_s[...] / l1_s[...] - lam * (a2_s[...] / l2_s[...])
    ot = ot * lax.rsqrt(jnp.mean(ot * ot, axis=0, keepdims=True) + EPS) * g_ref[...] * (1.0 - lam_init)
    o_ref[...] = ot.T.astype(BF16)


def _diff_attn_prompt(q, k, vt, lam_vecs, subln_g, lam_init):
    B, L, _ = q.shape
    nt, t = vt.shape[1], vt.shape[3]
    kern = functools.partial(_diff_attn_kernel, t=t, lam_init=lam_init)
    return pl.pallas_call(
        kern,
        grid=(B, HA, nt),
        in_specs=[pl.BlockSpec((4, DH_A), lambda b, h, i: (0, 0)),
                  pl.BlockSpec((None, t, DV_A), lambda b, h, i: (b, i, h)),
                  pl.BlockSpec((None, L, DV_A), lambda b, h, i: (b, 0, h)),
                  pl.BlockSpec((None, nt, DV_A, t), lambda b, h, i: (b, 0, h, 0)),
                  pl.BlockSpec((None, DV_A, 1), lambda b, h, i: (h, 0, 0))],
        out_specs=pl.BlockSpec((None, t, DV_A), lambda b, h, i: (b, i, h)),
        out_shape=jax.ShapeDtypeStruct((B, L, W_A), BF16),
        scratch_shapes=[pltpu.VMEM((1, t), F32), pltpu.VMEM((1, t), F32), pltpu.VMEM((DV_A, t), F32),
                        pltpu.VMEM((1, t), F32), pltpu.VMEM((1, t), F32), pltpu.VMEM((DV_A, t), F32)]
                       + [pltpu.VMEM((t, t), F32)] * 4,
        compiler_params=_cparams("parallel", "parallel", "parallel"),
        name="diff_attn",
    )(lam_vecs, q, k, vt, subln_g.reshape(HA, DV_A, 1))


def _diff_attn_sample_kernel(lam_ref, q_ref, ck_ref, cv_ref, k_ref, v_ref, g_ref, o_ref, *, lam_init):
    lam = _lam_value(lam_ref, lam_init)
    for h in range(HA):
        cols = slice(h * DV_A, (h + 1) * DV_A)
        q1, q2 = _split_halves(q_ref[:, cols])
        ck, cv = ck_ref[:, cols], cv_ref[:, cols]
        k, v = k_ref[:, cols], v_ref[:, cols]

        def one_map(qm):
            sc = _dot_nt(qm, ck)
            sn = _dot_nt(qm, k)
            m = jnp.maximum(jnp.max(sc, axis=-1, keepdims=True), jnp.max(sn, axis=-1, keepdims=True))
            pc = jnp.exp2(sc - m)
            pn = jnp.exp2(sn - m)
            l = jnp.sum(pc, axis=-1, keepdims=True) + jnp.sum(pn, axis=-1, keepdims=True)
            return (_dot(pc.astype(BF16), cv) + _dot(pn.astype(BF16), v)) / l

        o = one_map(q1) - lam * one_map(q2)
        o_ref[:, cols] = _subln(o, g_ref[h], lam_init).astype(BF16)


def _diff_attn_sample(q, ck, cv, k, v, lam_vecs, subln_g, lam_init):
    B, T, _ = q.shape
    P = ck.shape[1]
    kern = functools.partial(_diff_attn_sample_kernel, lam_init=lam_init)
    new = pl.BlockSpec((None, T, W_A), lambda b: (b, 0, 0))
    old = pl.BlockSpec((None, P, W_A), lambda b: (b, 0, 0))
    return pl.pallas_call(
        kern,
        grid=(B,),
        in_specs=[pl.BlockSpec((4, DH_A), lambda b: (0, 0)), new, old, old, new, new,
                  pl.BlockSpec((HA, 1, DV_A), lambda b: (0, 0, 0))],
        out_specs=new,
        out_shape=jax.ShapeDtypeStruct((B, T, W_A), BF16),
        compiler_params=_cparams("parallel"),
        name="diff_attn_sample",
    )(lam_vecs, q, ck, cv, k, v, subln_g.reshape(HA, 1, DV_A))


def _rel_bias_kernel(tab_ref, o_ref, *, rows, cols, off0, off_step, masked, transposed):
    h = pl.program_id(0)
    d = pl.program_id(1)
    r = lax.broadcasted_iota(jnp.int32, (rows, cols), 0)
    c = lax.broadcasted_iota(jnp.int32, (rows, cols), 1)
    qi, ki = (c, r) if transposed else (r, c)
    nq, nk = (cols, rows) if transposed else (rows, cols)
    off = off0 + d * off_step
    idx = jnp.clip(off + qi - ki, -REL_CLIP, REL_CLIP) + REL_CLIP
    lo = jnp.clip(off - (nk - 1), -REL_CLIP, REL_CLIP) + REL_CLIP
    hi = jnp.clip(off + (nq - 1), -REL_CLIP, REL_CLIP) + REL_CLIP

    def body(j, acc):
        return jnp.where(idx == j, tab_ref[j * HB + h], acc)

    bias = lax.fori_loop(lo, hi + 1, body, jnp.zeros((rows, cols), F32)) * LOG2E
    if masked:
        dc = d * (off_step // CHUNK) + qi // CHUNK - ki // CHUNK
        bias = jnp.where((dc >= 0) & (dc <= BAND_CHUNKS), bias, NEG)
    o_ref[...] = bias


def _rel_bias(table, n_off, rows, cols, off0, off_step, masked, transposed):
    kern = functools.partial(_rel_bias_kernel, rows=rows, cols=cols, off0=off0,
                             off_step=off_step, masked=masked, transposed=transposed)
    return pl.pallas_call(
        kern,
        grid=(HB, n_off),
        in_specs=[pl.BlockSpec(memory_space=pltpu.SMEM)],
        out_specs=pl.BlockSpec((None, None, rows, cols), lambda h, d: (h, d, 0, 0)),
        out_shape=jax.ShapeDtypeStruct((HB, n_off, rows, cols), F32),
        compiler_params=_cparams("parallel", "parallel"),
        name="rel_bias",
    )(table.reshape(N_REL * HB))


def _pair_softmax_out(q, ks, vs, bias_fn):
    lane = lax.broadcasted_iota(jnp.int32, (q.shape[0], LANES), 1)
    outs = []
    for hh, qm in enumerate(_split_halves(q)):
        ss = [_dot_nt(qm, k) + bias_fn(hh, j) for j, k in enumerate(ks)]
        m = functools.reduce(jnp.maximum, [jnp.max(s, axis=-1, keepdims=True) for s in ss])
        ps = [jnp.exp2(s - m) for s in ss]
        l = functools.reduce(jnp.add, [jnp.sum(p, axis=-1, keepdims=True) for p in ps])
        o = functools.reduce(jnp.add, [_dot(p.astype(BF16), v) for p, v in zip(ps, vs)])
        outs.append(o / l)
    return jnp.where(lane < 64, outs[0], outs[1])


BAND_TILE = 4 * CHUNK
N_BAND_TILES = BAND // BAND_TILE + 1


def _band_attn_kernel(q_ref, k0_ref, k1_ref, k2_ref, v0_ref, v1_ref, v2_ref, b_ref, o_ref, sa_s, sb_s):
    i = pl.program_id(1)
    k_refs = (k0_ref, k1_ref, k2_ref)
    vt_refs = (v0_ref, v1_ref, v2_ref)
    row = lax.broadcasted_iota(jnp.int32, (LANES, BAND_TILE), 0)

    def produce(h, buf):
        cols = slice((h // 2) * LANES, (h // 2 + 1) * LANES)
        qm = _split_halves(q_ref[:, cols])[h % 2]
        for d, k_ref in enumerate(k_refs):
            buf[d] = jnp.where(i >= d, _dot_nt(k_ref[:, cols], qm) + b_ref[h, d], NEG)

    def consume(h, buf):
        cols = slice((h // 2) * LANES, (h // 2 + 1) * LANES)
        ss = [buf[d] for d in range(N_BAND_TILES)]
        m = functools.reduce(jnp.maximum, [jnp.max(s, axis=0, keepdims=True) for s in ss])
        ps = [jnp.exp2(s - m) for s in ss]
        l = functools.reduce(jnp.add, [jnp.sum(p, axis=0, keepdims=True) for p in ps])
        ot = functools.reduce(jnp.add, [_dot(r[cols, :], p.astype(BF16)) for r, p in zip(vt_refs, ps)])
        return ot / l

    bufs = (sa_s, sb_s)
    produce(0, bufs[0])
    prev = None
    for h in range(HB):
        if h + 1 < HB:
            produce(h + 1, bufs[(h + 1) % 2])
        ot = consume(h, bufs[h % 2])
        if h % 2 == 1:
            cols = slice((h // 2) * LANES, (h // 2 + 1) * LANES)
            o_ref[:, cols] = jnp.where(row < DH_B, prev, ot).T.astype(BF16)
        prev = ot


def _band_attn_prompt(q, k, vt, bias_t):
    B, L, _ = q.shape
    t = BAND_TILE
    per = vt.shape[3] // t
    qspec = pl.BlockSpec((None, t, W_B), lambda b, i: (b, i, 0))
    kspec = lambda d: pl.BlockSpec((None, t, W_B), lambda b, i: (b, jnp.maximum(i - d, 0), 0))
    vspec = lambda d: pl.BlockSpec(
        (None, None, W_B, t),
        lambda b, i: (b, jnp.maximum(i - d, 0) // per, 0, jnp.maximum(i - d, 0) % per))
    return pl.pallas_call(
        _band_attn_kernel,
        grid=(B, L // t),
        in_specs=[qspec, kspec(0), kspec(1), kspec(2), vspec(0), vspec(1), vspec(2),
                  pl.BlockSpec((HB, N_BAND_TILES, t, t), lambda b, i: (0, 0, 0, 0),
                               pipeline_mode=pl.Buffered(1))],
        out_specs=qspec,
        out_shape=jax.ShapeDtypeStruct((B, L, W_B), BF16),
        scratch_shapes=[pltpu.VMEM((N_BAND_TILES, t, t), F32)] * 2,
        compiler_params=_cparams("parallel", "parallel"),
        name="band_attn",
    )(q, k, k, k, vt, vt, vt, bias_t)


def _band_attn_sample_kernel(q_ref, ck_ref, cv_ref, k_ref, v_ref, bc_ref, bn_ref, o_ref):
    for hp in range(HB // 2):
        cols = slice(hp * LANES, (hp + 1) * LANES)
        ks = [ck_ref[:, cols], k_ref[:, cols]]
        vs = [cv_ref[:, cols], v_ref[:, cols]]

        def bias_fn(hh, j, hp=hp):
            return (bc_ref if j == 0 else bn_ref)[2 * hp + hh, 0]

        o_ref[:, cols] = _pair_softmax_out(q_ref[:, cols], ks, vs, bias_fn).astype(BF16)


def _band_attn_sample(q, ck, cv, k, v, bias_c, bias_n):
    B, T, _ = q.shape
    P = ck.shape[1]
    new = pl.BlockSpec((None, T, W_B), lambda b: (b, 0, 0))
    old = pl.BlockSpec((None, P, W_B), lambda b: (b, 0, 0))
    return pl.pallas_call(
        _band_attn_sample_kernel,
        grid=(B,),
        in_specs=[new, old, old, new, new,
                  pl.BlockSpec((HB, 1, T, P), lambda b: (0, 0, 0, 0)),
                  pl.BlockSpec((HB, 1, T, T), lambda b: (0, 0, 0, 0))],
        out_specs=new,
        out_shape=jax.ShapeDtypeStruct((B, T, W_B), BF16),
        compiler_params=_cparams("parallel"),
        name="band_attn_sample",
    )(q, ck, cv, k, v, bias_c, bias_n)


N_SLAB = N_STATE // MXU_DIM


def _ssm_kernel(u_ref, wxr_ref, wxi_ref, ar_ref, ai_ref, h0r_ref, h0i_ref, wcr_ref, wci_ref,
                d_ref, wglu_ref, oc_ref, hro_ref, hio_ref, xr_s, xi_s, hr_s, hi_s, *, tt, nb, half):
    i = pl.program_id(0)

    @pl.when(i == 0)
    def _():
        hr_s[...] = h0r_ref[...]
        hi_s[...] = h0i_ref[...]

    u = u_ref[...]
    ub = u.astype(BF16)
    for j in range(N_SLAB):
        us = ub[:, LANES * (j // 2):LANES * (j // 2 + 1)]
        xr_s[:, MXU_DIM * j:MXU_DIM * (j + 1)] = _dot(us, wxr_ref[j])
        xi_s[:, MXU_DIM * j:MXU_DIM * (j + 1)] = _dot(us, wxi_ref[j])

    for c in range(N_STATE // half):
        cols = slice(c * half, (c + 1) * half)
        ar = jnp.broadcast_to(ar_ref[:, cols], (nb, half))
        ai = jnp.broadcast_to(ai_ref[:, cols], (nb, half))

        def step(t, carry, cols=cols, ar=ar, ai=ai):
            hr, hi = carry
            r0 = pl.multiple_of(t * nb, nb)
            nhr = ar * hr - ai * hi + xr_s[pl.ds(r0, nb), cols]
            nhi = ar * hi + ai * hr + xi_s[pl.ds(r0, nb), cols]
            xr_s[pl.ds(r0, nb), cols] = nhr
            xi_s[pl.ds(r0, nb), cols] = nhi
            return nhr, nhi

        hr, hi = lax.fori_loop(0, tt, step, (hr_s[:, cols], hi_s[:, cols]))
        hr_s[:, cols] = hr
        hi_s[:, cols] = hi

    hro_ref[...] = hr_s[...]
    hio_ref[...] = hi_s[...]

    ys = []
    for s in range(W_C // LANES):
        acc = None
        for j in (2 * s, 2 * s + 1):
            hrb = xr_s[:, MXU_DIM * j:MXU_DIM * (j + 1)].astype(BF16)
            hib = xi_s[:, MXU_DIM * j:MXU_DIM * (j + 1)].astype(BF16)
            part = _dot(hrb, wcr_ref[j]) + _dot(hib, wci_ref[j])
            acc = part if acc is None else acc + part
        ys.append(acc)
    y = jnp.concatenate(ys, axis=-1) + d_ref[...] * u
    ge = 0.5 * y * (1.0 + jnp.tanh(math.sqrt(2.0 / math.pi) * (y + 0.044715 * (y * y * y))))
    gl = _dot(ge.astype(BF16), wglu_ref[...])
    oc_ref[...] = (gl[:, :W_C] * _sigmoid(gl[:, W_C:])).astype(BF16)


def _ssm(u_tm, wxr, wxi, ar, ai, h0r, h0i, wcr, wci, d, wglu, tt):
    n_rows = u_tm.shape[0]
    nb = h0r.shape[0]
    rows = tt * nb
    half = N_STATE // 2
    kern = functools.partial(_ssm_kernel, tt=tt, nb=nb, half=half)
    full = lambda shape: pl.BlockSpec(shape, lambda i: (0,) * len(shape))
    return pl.pallas_call(
        kern,
        grid=(n_rows // rows,),
        in_specs=[pl.BlockSpec((rows, W_C), lambda i: (i, 0)),
                  full((N_SLAB, LANES, MXU_DIM)), full((N_SLAB, LANES, MXU_DIM)),
                  full((1, N_STATE)), full((1, N_STATE)),
                  full((nb, N_STATE)), full((nb, N_STATE)),
                  full((N_SLAB, MXU_DIM, LANES)), full((N_SLAB, MXU_DIM, LANES)),
                  full((1, W_C)), full((W_C, 2 * W_C))],
        out_specs=[pl.BlockSpec((rows, W_C), lambda i: (i, 0)),
                   full((nb, N_STATE)), full((nb, N_STATE))],
        out_shape=[jax.ShapeDtypeStruct((n_rows, W_C), BF16),
                   jax.ShapeDtypeStruct((nb, N_STATE), F32),
                   jax.ShapeDtypeStruct((nb, N_STATE), F32)],
        scratch_shapes=[pltpu.VMEM((rows, N_STATE), F32), pltpu.VMEM((rows, N_STATE), F32),
                        pltpu.VMEM((nb, N_STATE), F32), pltpu.VMEM((nb, N_STATE), F32)],
        compiler_params=_cparams("arbitrary"),
        name="ssm",
    )(u_tm, wxr, wxi, ar, ai, h0r, h0i, wcr, wci, d, wglu)


def _ssm_weights(w):
    ar, ai = w['ssm_a_re'].astype(F32), w['ssm_a_im'].astype(F32)
    dt = jnp.exp(w['ssm_log_dt'].astype(F32))[:, None]
    mag = jnp.exp(ar * dt)
    abar_r, abar_i = mag * jnp.cos(ai * dt), mag * jnp.sin(ai * dt)
    den = ar * ar + ai * ai
    nr, ni = abar_r - 1.0, abar_i
    z_r = (nr * ar + ni * ai) / den
    z_i = (ni * ar - nr * ai) / den
    br, bi = w['ssm_b_re'].astype(F32), w['ssm_b_im'].astype(F32)
    bb_r = z_r[..., None] * br - z_i[..., None] * bi
    bb_i = z_r[..., None] * bi + z_i[..., None] * br

    gps = MXU_DIM // P_STATE
    eye = jnp.eye(gps, dtype=F32)

    def x_slabs(bb):
        b4 = bb.reshape(N_SLAB, gps, P_STATE, GC)
        blk = jnp.einsum('jgpc,gh->jhcgp', b4, eye).reshape(N_SLAB, gps * GC, MXU_DIM)
        zero = jnp.zeros_like(blk)
        even = jnp.concatenate([blk, zero], axis=1)
        odd = jnp.concatenate([zero, blk], axis=1)
        sel = (jnp.arange(N_SLAB) % 2 == 0)[:, None, None]
        return jnp.where(sel, even, odd).astype(BF16)

    def y_slabs(cc):
        c4 = cc.reshape(N_SLAB, gps, GC, P_STATE)
        blk = jnp.einsum('jgcp,gh->jgphc', c4, eye).reshape(N_SLAB, MXU_DIM, gps * GC)
        zero = jnp.zeros_like(blk)
        even = jnp.concatenate([blk, zero], axis=2)
        odd = jnp.concatenate([zero, blk], axis=2)
        sel = (jnp.arange(N_SLAB) % 2 == 0)[:, None, None]
        return jnp.where(sel, even, odd).astype(BF16)

    return dict(wxr=x_slabs(bb_r), wxi=x_slabs(bb_i),
                ar=abar_r.reshape(1, N_STATE), ai=abar_i.reshape(1, N_STATE),
                wcr=y_slabs(w['ssm_c_re'].astype(F32)), wci=y_slabs(-w['ssm_c_im'].astype(F32)))


def _merge_out_kernel(x_ref, oa_ref, ob_ref, oc_ref, g_ref, wg_ref, wa_ref, wb_ref, wc_ref, wo_ref, o_ref):
    x = x_ref[...]
    h = _rms(x, g_ref[...]).astype(BF16)
    merged = None
    for j, (br_ref, w_ref) in enumerate(((oa_ref, wa_ref), (ob_ref, wb_ref), (oc_ref, wc_ref))):
        gate = _sigmoid(_dot(h, wg_ref[:, j * D_MODEL:(j + 1) * D_MODEL]))
        term = gate * _dot(br_ref[...], w_ref[...])
        merged = term if merged is None else merged + term
    o_ref[...] = x + _dot(merged.astype(BF16), wo_ref[...])


def _merge_out(x, oa, ob, oc_tm, g, wg, wa, wb, wc, wo, tm):
    B, L, D = x.shape
    full = lambda shape: pl.BlockSpec(shape, lambda b, t: (0,) * len(shape))
    br = pl.BlockSpec((None, tm, 512), lambda b, t: (b, t, 0))
    return pl.pallas_call(
        _merge_out_kernel,
        grid=(B, L // tm),
        in_specs=[pl.BlockSpec((None, tm, D), lambda b, t: (b, t, 0)), br, br,
                  pl.BlockSpec((tm, 512), lambda b, t: (t, b)),
                  full((1, D)), full((D, 3 * D)), full((W_A, D)), full((W_B, D)), full((W_C, D)), full((D, D))],
        out_specs=pl.BlockSpec((None, tm, D), lambda b, t: (b, t, 0)),
        out_shape=jax.ShapeDtypeStruct((B, L, D), F32),
        compiler_params=_cparams("parallel", "parallel"),
        name="merge_out",
    )(x, oa, ob, oc_tm, g, wg, wa, wb, wc, wo)


FF_CHUNK = 2 * MXU_DIM


def _swiglu(h, w1_ref, w3_ref, w2_ref):
    n_ff = w1_ref.shape[1]
    acc = None
    for c0 in range(0, n_ff, FF_CHUNK):
        c1 = min(c0 + FF_CHUNK, n_ff)
        a = _dot(h, w1_ref[:, c0:c1])
        b = _dot(h, w3_ref[:, c0:c1])
        part = _dot((a * _sigmoid(a) * b).astype(BF16), w2_ref[c0:c1, :])
        acc = part if acc is None else acc + part
    return acc


def _ffn_kernel(x_ref, g_ref, w1_ref, w3_ref, w2_ref, fg_ref, o_ref, *, final_norm):
    x = x_ref[...]
    h = _rms(x, g_ref[...]).astype(BF16)
    y = x + _swiglu(h, w1_ref, w3_ref, w2_ref)
    o_ref[...] = _rms(y, fg_ref[...]) if final_norm else y


def _ffn(x, g, w1, w3, w2, final_g, final_norm, tm):
    B, L, D = x.shape
    full = lambda shape: pl.BlockSpec(shape, lambda b, t: (0,) * len(shape), pipeline_mode=pl.Buffered(1))
    tok = pl.BlockSpec((None, tm, D), lambda b, t: (b, t, 0))
    return pl.pallas_call(
        functools.partial(_ffn_kernel, final_norm=final_norm),
        grid=(B, L // tm),
        in_specs=[tok, full((1, D)), full((D, D_FF)), full((D, D_FF)), full((D_FF, D)), full((1, D))],
        out_specs=tok,
        out_shape=jax.ShapeDtypeStruct((B, L, D), F32),
        compiler_params=_cparams("parallel", "parallel"),
        name="ffn",
    )(x, g, w1, w3, w2, final_g)


R_I1, R_I2, R_R1, R_R2, R_G1, R_G2 = range(6)


def _moe_route_kernel(x_ref, g_ref, rw_ref, tri_ref, route_ref, cnt_ref, cnt_s):
    @pl.when(pl.program_id(0) == 0)
    def _():
        cnt_s[...] = jnp.zeros_like(cnt_s)

    lane = lax.broadcasted_iota(jnp.int32, route_ref.shape, 1)
    hf = _rms(x_ref[...], g_ref[...])
    logits = jnp.dot(hf, rw_ref[...], preferred_element_type=F32, precision=lax.Precision.HIGHEST)
    logits = jnp.where(lane < N_EXPERTS, logits, -jnp.inf)
    v1 = jnp.max(logits, axis=-1, keepdims=True)
    i1 = jnp.min(jnp.where(logits == v1, lane, LANES), axis=-1, keepdims=True)
    rest = jnp.where(lane == i1, -jnp.inf, logits)
    v2 = jnp.max(rest, axis=-1, keepdims=True)
    i2 = jnp.min(jnp.where(rest == v2, lane, LANES), axis=-1, keepdims=True)
    ex = jnp.exp(v2 - v1)
    g1 = 1.0 / (1.0 + ex)
    g2 = ex / (1.0 + ex)

    oh1 = (lane == i1).astype(F32)
    oh2 = (lane == i2).astype(F32)
    oh = oh1 + oh2
    incl = _dot(tri_ref[...], oh.astype(BF16))
    rank = cnt_s[...] + incl - oh
    r1 = jnp.sum(oh1 * rank, axis=-1, keepdims=True)
    r2 = jnp.sum(oh2 * rank, axis=-1, keepdims=True)
    cnt_s[...] += jnp.sum(oh, axis=0, keepdims=True)

    rec = jnp.zeros(route_ref.shape, F32)
    for ln, val in ((R_I1, i1.astype(F32)), (R_I2, i2.astype(F32)), (R_R1, r1), (R_R2, r2),
                    (R_G1, g1), (R_G2, g2)):
        rec = jnp.where(lane == ln, val, rec)
    route_ref[...] = rec
    cnt_ref[...] = cnt_s[...]


def _moe_route(x, g, rw, tm):
    n, d = x.shape
    tri = (jnp.arange(tm)[:, None] >= jnp.arange(tm)[None, :]).astype(BF16)
    full = lambda shape: pl.BlockSpec(shape, lambda i: (0,) * len(shape))
    return pl.pallas_call(
        _moe_route_kernel,
        grid=(n // tm,),
        in_specs=[pl.BlockSpec((tm, d), lambda i: (i, 0)), full((1, d)), full((d, LANES)), full((tm, tm))],
        out_specs=[pl.BlockSpec((tm, LANES), lambda i: (i, 0)), full((1, LANES))],
        out_shape=[jax.ShapeDtypeStruct((n, LANES), F32), jax.ShapeDtypeStruct((1, LANES), F32)],
        scratch_shapes=[pltpu.VMEM((1, LANES), F32)],
        compiler_params=_cparams("arbitrary"),
        name="moe_route",
    )(x, g, rw, tri)


def _row_copies(pos_ref, tm, make_copy):
    def body(i, c):
        for s in range(2):
            make_copy(s, i, pos_ref[s, i]).start()
        return c

    lax.fori_loop(0, tm, body, 0, unroll=8)


def _moe_dispatch_kernel(pos_ref, x_ref, xs_in_ref, xs_ref, sem, *, tm):
    del xs_in_ref
    _row_copies(pos_ref, tm, lambda s, i, p: pltpu.make_async_copy(
        x_ref.at[pl.ds(i, 1)], xs_ref.at[pl.ds(p, 1)], sem))
    pltpu.make_async_copy(xs_ref.at[pl.ds(0, 2 * tm)], xs_ref.at[pl.ds(0, 2 * tm)], sem).wait()


def _moe_dispatch(x, pos, n_rows, tm):
    n, d = x.shape
    return pl.pallas_call(
        functools.partial(_moe_dispatch_kernel, tm=tm),
        grid=(n // tm,),
        in_specs=[pl.BlockSpec((None, 2, tm), lambda i: (i, 0, 0), memory_space=pltpu.SMEM),
                  pl.BlockSpec((tm, d), lambda i: (i, 0)),
                  pl.BlockSpec(memory_space=pl.ANY)],
        out_specs=pl.BlockSpec(memory_space=pl.ANY),
        out_shape=jax.ShapeDtypeStruct((n_rows, d), F32),
        input_output_aliases={2: 0},
        scratch_shapes=[pltpu.SemaphoreType.DMA(())],
        compiler_params=_cparams("arbitrary"),
        name="moe_dispatch",
    )(pos, x, jnp.zeros((n_rows, d), F32))


def _moe_expert_kernel(e_ref, rows_ref, xs_ref, g_ref, w1_ref, w3_ref, w2_ref, ys_ref):
    rows = rows_ref[pl.program_id(0)]

    @pl.when(rows > 0)
    def _():
        h = _rms(xs_ref[...], g_ref[...]).astype(BF16)
        ys_ref[...] = _swiglu(h, w1_ref, w3_ref, w2_ref)

    @pl.when(rows == 0)
    def _():
        ys_ref[...] = jnp.zeros_like(ys_ref)


def _moe_experts(xs, g, w1, w3, w2, tile_e, tile_rows, t):
    p, d = xs.shape
    wspec = lambda shape: pl.BlockSpec((None,) + shape, lambda i, e, rows: (e[i], 0, 0))
    row = pl.BlockSpec((t, d), lambda i, e, rows: (i, 0))
    return pl.pallas_call(
        _moe_expert_kernel,
        grid_spec=pltpu.PrefetchScalarGridSpec(
            num_scalar_prefetch=2,
            grid=(p // t,),
            in_specs=[row, pl.BlockSpec((1, d), lambda i, e, rows: (0, 0)),
                      wspec((d, D_FF_E)), wspec((d, D_FF_E)), wspec((D_FF_E, d))],
            out_specs=row),
        out_shape=jax.ShapeDtypeStruct((p, d), F32),
        compiler_params=_cparams("arbitrary"),
        name="moe_experts",
    )(tile_e, tile_rows, xs, g, w1, w3, w2)


def _moe_combine_kernel(pos_ref, x_ref, route_ref, fg_ref, ys_ref, o_ref, ybuf, sem, *, tm, final_norm):
    _row_copies(pos_ref, tm, lambda s, i, p: pltpu.make_async_copy(
        ys_ref.at[pl.ds(p, 1)], ybuf.at[s, pl.ds(i, 1)], sem))
    for s in range(2):
        pltpu.make_async_copy(ys_ref.at[pl.ds(0, tm)], ybuf.at[s], sem).wait()
    rec = route_ref[...]
    lane = lax.broadcasted_iota(jnp.int32, rec.shape, 1)
    g1 = jnp.sum(jnp.where(lane == R_G1, rec, 0.0), axis=-1, keepdims=True)
    g2 = jnp.sum(jnp.where(lane == R_G2, rec, 0.0), axis=-1, keepdims=True)
    y = x_ref[...] + (g1 * ybuf[0] + g2 * ybuf[1])
    o_ref[...] = _rms(y, fg_ref[...]) if final_norm else y


def _moe_combine(x, route, pos, ys, final_g, final_norm, tm):
    n, d = x.shape
    tok = pl.BlockSpec((tm, d), lambda i: (i, 0))
    return pl.pallas_call(
        functools.partial(_moe_combine_kernel, tm=tm, final_norm=final_norm),
        grid=(n // tm,),
        in_specs=[pl.BlockSpec((None, 2, tm), lambda i: (i, 0, 0), memory_space=pltpu.SMEM),
                  tok, pl.BlockSpec((tm, LANES), lambda i: (i, 0)),
                  pl.BlockSpec((1, d), lambda i: (0, 0)), pl.BlockSpec(memory_space=pl.ANY)],
        out_specs=tok,
        out_shape=jax.ShapeDtypeStruct((n, d), F32),
        scratch_shapes=[pltpu.VMEM((2, tm, d), F32), pltpu.SemaphoreType.DMA(())],
        compiler_params=_cparams("arbitrary"),
        name="moe_combine",
    )(pos, x, route, final_g, ys)


def _moe(x, g, rw, w1, w3, w2, final_g, final_norm):
    B, L, D = x.shape
    n = B * L
    tm = min(512, n)
    xf = x.reshape(n, D)
    route, counts = _moe_route(xf, g, rw, tm)

    idx = route[:, :4].astype(jnp.int32)
    cnt = counts[0, :N_EXPERTS].astype(jnp.int32)
    tiles_e = (cnt + tm - 1) // tm
    ends = jnp.cumsum(tiles_e)
    starts = ends - tiles_e
    n_used = ends[-1]
    n_tiles = 2 * n // tm + N_EXPERTS
    pos = jnp.stack([starts[idx[:, R_I1]] * tm + idx[:, R_R1],
                     starts[idx[:, R_I2]] * tm + idx[:, R_R2]])
    pos = pos.reshape(2, n // tm, tm).transpose(1, 0, 2)
    tile = jnp.arange(n_tiles, dtype=jnp.int32)
    tile_e = jnp.searchsorted(ends, jnp.minimum(tile, n_used - 1), side='right').astype(jnp.int32)
    tile_rows = jnp.where(tile < n_used, jnp.clip(cnt[tile_e] - (tile - starts[tile_e]) * tm, 0, tm), 0)

    xs = _moe_dispatch(xf, pos, n_tiles * tm, tm)
    ys = _moe_experts(xs, g, w1, w3, w2, tile_e, tile_rows.astype(jnp.int32), tm)
    out = _moe_combine(xf, route, pos, ys, final_g, final_norm, tm)
    return out.reshape(B, L, D)


def _pick(n, cands):
    for c in cands:
        if n % c == 0:
            return c
    return n


def _layer(x, l, w, cache, final_g, last, dkv, depth):
    B, L, D = x.shape
    assert B == SUBLANES, "the SSM kernel puts the streams of one time step on the 8 sublanes"
    lam_init = 0.8 - 0.6 * math.exp(-0.3 * l)
    tm = _pick(L, (512, 256, 128, 64, 32))
    n_keep = min(BAND, L)
    assert n_keep == tm, "the band cache rows must be exactly the last token tile"
    row = lambda a: a.reshape(1, -1).astype(F32)
    prompt = cache is None

    w_in = w['w_in'].astype(BF16)
    dk, dv = dkv if dkv is not None else (None, None)
    p = _norm_proj(x, row(w['norm1_g']), w_in[:, :N_QKVU], tm, prompt, l, depth, dk, dv)

    lam_vecs = jnp.stack([w['lam_q1'], w['lam_k1'], w['lam_q2'], w['lam_k2']]).astype(F32)
    subln_g = w['subln_g'].astype(F32)
    table = w['rel_bias'].astype(F32)
    if prompt:
        oa = _diff_attn_prompt(p['qa'], p['ka_b'], p['va_b'], lam_vecs, subln_g, lam_init)
        bias_t = _rel_bias(table, N_BAND_TILES, BAND_TILE, BAND_TILE, 0, BAND_TILE, True, True)
        ob = _band_attn_prompt(p['qb'], p['kb_b'], p['vb_b'], bias_t)
        h0r = jnp.zeros((B, N_STATE), F32)
        h0i = jnp.zeros((B, N_STATE), F32)
    else:
        ck_a, cv_a, ck_b, cv_b, h0_re, h0_im = cache
        P = ck_a.shape[1]
        bp = ck_b.shape[1]
        oa = _diff_attn_sample(p['qa'], ck_a.reshape(B, P, W_A).astype(BF16), cv_a.reshape(B, P, W_A).astype(BF16),
                               p['ka_b'], p['va_b'], lam_vecs, subln_g, lam_init)
        bias_c = _rel_bias(table, 1, L, bp, bp, 0, False, False)
        bias_n = _rel_bias(table, 1, L, L, 0, 0, False, False)
        ob = _band_attn_sample(p['qb'], ck_b.reshape(B, bp, W_B).astype(BF16), cv_b.reshape(B, bp, W_B).astype(BF16),
                               p['kb_b'], p['vb_b'], bias_c, bias_n)
        h0r = h0_re.reshape(B, N_STATE).astype(F32)
        h0i = h0_im.reshape(B, N_STATE).astype(F32)

    sw = _ssm_weights(w)
    oc_tm, hr, hi = _ssm(p['u_tm'].reshape(L * B, W_C), sw['wxr'], sw['wxi'], sw['ar'], sw['ai'], h0r, h0i,
                         sw['wcr'], sw['wci'], row(w['ssm_d']), w['w_glu'].astype(BF16), _pick(L, (64, 32)))

    x = _merge_out(x, oa, ob, oc_tm.reshape(L, B * W_C), row(w['norm1_g']), w_in[:, N_QKVU:],
                   w['w_br_a'].astype(BF16), w['w_br_b'].astype(BF16), w['w_br_c'].astype(BF16),
                   w['w_out'].astype(BF16), tm)

    fg = row(final_g)
    if l % 2 == 0:
        x = _ffn(x, row(w['norm2_g']), w['ffn_w1'].astype(BF16), w['ffn_w3'].astype(BF16),
                 w['ffn_w2'].astype(BF16), fg, last, tm)
    else:
        rw = jnp.pad(w['router_w'].astype(F32), ((0, 0), (0, LANES - N_EXPERTS)))
        x = _moe(x, row(w['norm2_g']), rw, w['moe_w1'].astype(BF16),
                 w['moe_w3'].astype(BF16), w['moe_w2'].astype(BF16), fg, last)

    small = (p['kb_last'].reshape(B, n_keep, HB, DH_B), p['vb_last'].reshape(B, n_keep, HB, DH_B),
             hr.reshape(B, N_GROUPS, P_STATE), hi.reshape(B, N_GROUPS, P_STATE))
    return x, (p['dk'], p['dv']), small


_PER_LAYER = ('norm1_g', 'w_in', 'lam_q1', 'lam_k1', 'lam_q2', 'lam_k2', 'subln_g', 'rel_bias',
              'ssm_a_re', 'ssm_a_im', 'ssm_log_dt', 'ssm_b_re', 'ssm_b_im', 'ssm_c_re', 'ssm_c_im',
              'ssm_d', 'w_glu', 'w_br_a', 'w_br_b', 'w_br_c', 'w_out', 'norm2_g')


def kernel(x_prompt, x_sample, cache_dk, cache_dv, cache_bk, cache_bv, state_ssm_re, state_ssm_im,
           norm1_g, w_in, lam_q1, lam_k1, lam_q2, lam_k2, subln_g, rel_bias, ssm_a_re, ssm_a_im,
           ssm_log_dt, ssm_b_re, ssm_b_im, ssm_c_re, ssm_c_im, ssm_d, w_glu, w_br_a, w_br_b, w_br_c,
           w_out, norm2_g, ffn_w1, ffn_w3, ffn_w2, router_w, moe_w1, moe_w3, moe_w2, final_g):
    stacked = dict(norm1_g=norm1_g, w_in=w_in, lam_q1=lam_q1, lam_k1=lam_k1, lam_q2=lam_q2, lam_k2=lam_k2,
                   subln_g=subln_g, rel_bias=rel_bias, ssm_a_re=ssm_a_re, ssm_a_im=ssm_a_im,
                   ssm_log_dt=ssm_log_dt, ssm_b_re=ssm_b_re, ssm_b_im=ssm_b_im, ssm_c_re=ssm_c_re,
                   ssm_c_im=ssm_c_im, ssm_d=ssm_d, w_glu=w_glu, w_br_a=w_br_a, w_br_b=w_br_b,
                   w_br_c=w_br_c, w_out=w_out, norm2_g=norm2_g)
    depth = w_in.shape[0]
    yp, ys = x_prompt, x_sample
    dkv_p, dkv_s = None, None
    small_p, small_s = [], []
    for l in range(depth):
        w = {name: stacked[name][l] for name in _PER_LAYER}
        if l % 2 == 0:
            w['ffn_w1'], w['ffn_w3'], w['ffn_w2'] = ffn_w1[l // 2], ffn_w3[l // 2], ffn_w2[l // 2]
        else:
            w['router_w'], w['moe_w1'] = router_w[l // 2], moe_w1[l // 2]
            w['moe_w3'], w['moe_w2'] = moe_w3[l // 2], moe_w2[l // 2]
        last = l == depth - 1
        yp, dkv_p, sp = _layer(yp, l, w, None, final_g, last, dkv_p, depth)
        ys, dkv_s, ss = _layer(ys, l, w, (cache_dk[l], cache_dv[l], cache_bk[l], cache_bv[l],
                                          state_ssm_re[l], state_ssm_im[l]), final_g, last, dkv_s, depth)
        small_p.append(sp)
        small_s.append(ss)
    outs = [yp, ys]
    for dkv, small in ((dkv_p, small_p), (dkv_s, small_s)):
        outs += list(dkv)
        for j in range(4):
            outs.append(jnp.stack([s[j] for s in small]))
    return tuple(outs)
```

```python
import functools
import math

import jax
import jax.numpy as jnp
from jax import lax
from jax.experimental import pallas as pl
from jax.experimental.pallas import tpu as pltpu

F32 = jnp.float32
BF16 = jnp.bfloat16

D_MODEL = 1024
CHUNK = 64
HA = 4
DH_A = 64
DV_A = 2 * DH_A
W_A = HA * DV_A
HB = 8
DH_B = 64
W_B = HB * DH_B
BAND_CHUNKS = 8
BAND = BAND_CHUNKS * CHUNK
REL_CLIP = 128
N_REL = 2 * REL_CLIP + 1
GC = 16
N_GROUPS = 32
W_C = N_GROUPS * GC
P_STATE = 64
N_STATE = N_GROUPS * P_STATE
D_FF = 11 * D_MODEL // 4
N_EXPERTS = 8
D_FF_E = D_FF // 2
EPS = 1e-6
NEG = -1e30
LOG2E = math.log2(math.e)
N_QKVU = 7 * 512

LANES = 128
SUBLANES = 8
MXU_DIM = 256
VMEM_LIMIT = 56 * 1024 * 1024


def _cparams(*sem):
    return pltpu.CompilerParams(dimension_semantics=sem, vmem_limit_bytes=VMEM_LIMIT)


def _rms(x, g):
    return x * lax.rsqrt(jnp.mean(x * x, axis=-1, keepdims=True) + EPS) * g


def _sigmoid(x):
    return 1.0 / (1.0 + jnp.exp(-x))


def _dot(a, b):
    return jnp.dot(a, b, preferred_element_type=F32)


def _dot_nt(a, b):
    return lax.dot_general(a, b, (((1,), (1,)), ((), ())), preferred_element_type=F32)


def _norm_proj_kernel(*refs, transpose_v, n_alias):
    x_ref, g_ref, w_ref = refs[:3]
    (qa_ref, dk_ref, dv_ref, kab_ref, vab_ref,
     qb_ref, kbl_ref, vbl_ref, kbb_ref, vbb_ref, u_ref) = refs[3 + n_alias:]
    h = _rms(x_ref[...], g_ref[...]).astype(BF16)

    def proj(c):
        return _dot(h, w_ref[:, c * 512:(c + 1) * 512])

    def store_heads(o_ref, y):
        for hd in range(HA):
            o_ref[:, hd, :] = y[:, hd * DV_A:(hd + 1) * DV_A]

    qa_ref[...] = (proj(0) * (DH_A ** -0.5 * LOG2E)).astype(BF16)
    ka = proj(1)
    store_heads(dk_ref, ka)
    kab_ref[...] = ka.astype(BF16)
    va = proj(2)
    store_heads(dv_ref, va)
    vab_ref[...] = (va.T if transpose_v else va).astype(BF16)
    qb_ref[...] = (proj(3) * (DH_B ** -0.5 * LOG2E)).astype(BF16)
    kb = proj(4)
    kbb_ref[...] = kb.astype(BF16)
    vb = proj(5)
    vbb_ref[...] = (vb.T if transpose_v else vb).astype(BF16)
    u_ref[...] = proj(6)

    @pl.when(pl.program_id(1) == pl.num_programs(1) - 1)
    def _():
        kbl_ref[...] = kb
        vbl_ref[...] = vb


def _norm_proj(x, g, w_qkvu, tm, transpose_v, layer, depth, dk, dv):
    B, L, D = x.shape
    nt = L // tm
    tok = jax.ShapeDtypeStruct((B, L, 512), BF16)
    tspec = pl.BlockSpec((None, tm, 512), lambda b, t: (b, t, 0))
    vshape, vspec = tok, tspec
    if transpose_v:
        vshape = jax.ShapeDtypeStruct((B, nt, 512, tm), BF16)
        vspec = pl.BlockSpec((None, None, 512, tm), lambda b, t: (b, t, 0, 0))
    cache = jax.ShapeDtypeStruct((depth, B, L, HA, DV_A), F32)
    cspec = pl.BlockSpec((None, None, tm, HA, DV_A), lambda b, t: (layer, b, t, 0, 0))
    last = jax.ShapeDtypeStruct((B, tm, 512), F32)
    lspec = pl.BlockSpec((None, tm, 512), lambda b, t: (b, 0, 0))
    out_shape = [tok, cache, cache, tok, vshape, tok, last, last, tok, vshape,
                 jax.ShapeDtypeStruct((L, B * 512), F32)]
    out_specs = [tspec, cspec, cspec, tspec, vspec, tspec, lspec, lspec, tspec, vspec,
                 pl.BlockSpec((tm, 512), lambda b, t: (t, b))]
    in_specs = [pl.BlockSpec((None, tm, D), lambda b, t: (b, t, 0)),
                pl.BlockSpec((1, D), lambda b, t: (0, 0)),
                pl.BlockSpec((D, N_QKVU), lambda b, t: (0, 0))] + [pl.BlockSpec(memory_space=pl.ANY)] * 2
    if dk is None:
        dk, dv = jnp.zeros(cache.shape, F32), jnp.zeros(cache.shape, F32)
    aliases = {3: 1, 4: 2}
    outs = pl.pallas_call(
        functools.partial(_norm_proj_kernel, transpose_v=transpose_v, n_alias=len(aliases)),
        grid=(B, nt),
        in_specs=in_specs,
        out_specs=out_specs,
        out_shape=out_shape,
        input_output_aliases=aliases,
        compiler_params=_cparams("parallel", "arbitrary"),
        name="norm_proj",
    )(x, g, w_qkvu, dk, dv)
    names = ('qa', 'dk', 'dv', 'ka_b', 'va_b', 'qb', 'kb_last', 'vb_last', 'kb_b', 'vb_b', 'u_tm')
    return dict(zip(names, outs))


def _lam_value(lam_ref, lam_init):
    lv = lam_ref[...]
    e1 = jnp.exp(jnp.sum(lv[0:1, :] * lv[1:2, :], axis=-1, keepdims=True))
    e2 = jnp.exp(jnp.sum(lv[2:3, :] * lv[3:4, :], axis=-1, keepdims=True))
    return e1 - e2 + lam_init


def _subln(o, g, lam_init):
    return _rms(o, g) * (1.0 - lam_init)


def _split_halves(q):
    lane = lax.broadcasted_iota(jnp.int32, q.shape, 1)
    zero = jnp.zeros_like(q)
    return jnp.where(lane < 64, q, zero), jnp.where(lane >= 64, q, zero)


def _diff_attn_kernel(lam_ref, q_ref, k_ref, vt_ref, g_ref, o_ref,
                      m1_s, l1_s, a1_s, m2_s, l2_s, a2_s, sa1_s, sa2_s, sb1_s, sb2_s,
                      xa1_s, xa2_s, xb1_s, xb2_s, *, t, lam_init):
    qi = pl.program_id(2)
    q1, q2 = _split_halves(q_ref[...])

    m1_s[...] = jnp.full_like(m1_s, -jnp.inf)
    m2_s[...] = jnp.full_like(m2_s, -jnp.inf)
    l1_s[...] = jnp.zeros_like(l1_s)
    l2_s[...] = jnp.zeros_like(l2_s)
    a1_s[...] = jnp.zeros_like(a1_s)
    a2_s[...] = jnp.zeros_like(a2_s)

    def update(st, mx, vt, m_s, l_s, a_s):
        m_prev = m_s[...]
        m_new = jnp.maximum(m_prev, mx)
        alpha = jnp.exp2(m_prev - m_new)
        p = jnp.exp2(st - m_new)
        l_s[...] = alpha * l_s[...] + jnp.sum(p, axis=0, keepdims=True)
        a_s[...] = alpha * a_s[...] + _dot(vt, p.astype(BF16))
        m_s[...] = m_new

    buf_a, buf_b = (sa1_s, sa2_s, xa1_s, xa2_s), (sb1_s, sb2_s, xb1_s, xb2_s)

    def produce(kj, buf):
        k = k_ref[pl.ds(pl.multiple_of(kj * t, t), t), :]
        for q, s_ref, x_ref in ((q1, buf[0], buf[2]), (q2, buf[1], buf[3])):
            st = _dot_nt(k, q)
            s_ref[...] = st
            x_ref[...] = jnp.max(st, axis=0, keepdims=True)

    def consume(kj, buf, masked):
        vt = vt_ref[kj]
        s1, s2 = buf[0][...], buf[1][...]
        if masked:
            kc = lax.broadcasted_iota(jnp.int32, (t, t), 0) // CHUNK
            qc = lax.broadcasted_iota(jnp.int32, (t, t), 1) // CHUNK
            keep = kc <= qc
            s1 = jnp.where(keep, s1, NEG)
            s2 = jnp.where(keep, s2, NEG)
            mx1 = jnp.max(s1, axis=0, keepdims=True)
            mx2 = jnp.max(s2, axis=0, keepdims=True)
        else:
            mx1, mx2 = buf[2][...], buf[3][...]
        update(s1, mx1, vt, m1_s, l1_s, a1_s)
        update(s2, mx2, vt, m2_s, l2_s, a2_s)

    produce(0, buf_a)

    def pair(jj, c):
        kj = 2 * jj
        produce(kj + 1, buf_b)
        consume(kj, buf_a, False)
        produce(kj + 2, buf_a)
        consume(kj + 1, buf_b, False)
        return c

    lax.fori_loop(0, qi // 2, pair, 0)

    @pl.when(qi % 2 == 1)
    def _():
        produce(qi, buf_b)
        consume(qi - 1, buf_a, False)
        consume(qi, buf_b, True)

    @pl.when(qi % 2 == 0)
    def _():
        consume(qi, buf_a, True)

    lam = _lam_value(lam_ref, lam_init)
    ot = a1_s[...] / l1_s[...] - lam * (a2_s[...] / l2_s[...])
    ot = ot * lax.rsqrt(jnp.mean(ot * ot, axis=0, keepdims=True) + EPS) * g_ref[...] * (1.0 - lam_init)
    o_ref[...] = ot.T.astype(BF16)


def _diff_attn_prompt(q, k, vt, lam_vecs, subln_g, lam_init):
    B, L, _ = q.shape
    nt, t = vt.shape[1], vt.shape[3]
    kern = functools.partial(_diff_attn_kernel, t=t, lam_init=lam_init)
    return pl.pallas_call(
        kern,
        grid=(B, HA, nt),
        in_specs=[pl.BlockSpec((4, DH_A), lambda b, h, i: (0, 0)),
                  pl.BlockSpec((None, t, DV_A), lambda b, h, i: (b, i, h)),
                  pl.BlockSpec((None, L, DV_A), lambda b, h, i: (b, 0, h)),
                  pl.BlockSpec((None, nt, DV_A, t), lambda b, h, i: (b, 0, h, 0)),
                  pl.BlockSpec((None, DV_A, 1), lambda b, h, i: (h, 0, 0))],
        out_specs=pl.BlockSpec((None, t, DV_A), lambda b, h, i: (b, i, h)),
        out_shape=jax.ShapeDtypeStruct((B, L, W_A), BF16),
        scratch_shapes=[pltpu.VMEM((1, t), F32), pltpu.VMEM((1, t), F32), pltpu.VMEM((DV_A, t), F32),
                        pltpu.VMEM((1, t), F32), pltpu.VMEM((1, t), F32), pltpu.VMEM((DV_A, t), F32)]
                       + [pltpu.VMEM((t, t), F32)] * 4 + [pltpu.VMEM((1, t), F32)] * 4,
        compiler_params=_cparams("parallel", "parallel", "parallel"),
        name="diff_attn",
    )(lam_vecs, q, k, vt, subln_g.reshape(HA, DV_A, 1))


def _diff_attn_sample_kernel(lam_ref, q_ref, ck_ref, cv_ref, k_ref, v_ref, g_ref, o_ref, *, lam_init):
    lam = _lam_value(lam_ref, lam_init)
    for h in range(HA):
        cols = slice(h * DV_A, (h + 1) * DV_A)
        q1, q2 = _split_halves(q_ref[:, cols])
        ck, cv = ck_ref[:, cols], cv_ref[:, cols]
        k, v = k_ref[:, cols], v_ref[:, cols]

        def one_map(qm):
            sc = _dot_nt(qm, ck)
            sn = _dot_nt(qm, k)
            m = jnp.maximum(jnp.max(sc, axis=-1, keepdims=True), jnp.max(sn, axis=-1, keepdims=True))
            pc = jnp.exp2(sc - m)
            pn = jnp.exp2(sn - m)
            l = jnp.sum(pc, axis=-1, keepdims=True) + jnp.sum(pn, axis=-1, keepdims=True)
            return (_dot(pc.astype(BF16), cv) + _dot(pn.astype(BF16), v)) / l

        o = one_map(q1) - lam * one_map(q2)
        o_ref[:, cols] = _subln(o, g_ref[h], lam_init).astype(BF16)


def _diff_attn_sample(q, ck, cv, k, v, lam_vecs, subln_g, lam_init):
    B, T, _ = q.shape
    P = ck.shape[1]
    kern = functools.partial(_diff_attn_sample_kernel, lam_init=lam_init)
    new = pl.BlockSpec((None, T, W_A), lambda b: (b, 0, 0))
    old = pl.BlockSpec((None, P, W_A), lambda b: (b, 0, 0))
    return pl.pallas_call(
        kern,
        grid=(B,),
        in_specs=[pl.BlockSpec((4, DH_A), lambda b: (0, 0)), new, old, old, new, new,
                  pl.BlockSpec((HA, 1, DV_A), lambda b: (0, 0, 0))],
        out_specs=new,
        out_shape=jax.ShapeDtypeStruct((B, T, W_A), BF16),
        compiler_params=_cparams("parallel"),
        name="diff_attn_sample",
    )(lam_vecs, q, ck, cv, k, v, subln_g.reshape(HA, 1, DV_A))


def _rel_bias_kernel(tab_ref, o_ref, *, rows, cols, off0, off_step, masked, transposed):
    h = pl.program_id(0)
    d = pl.program_id(1)
    off = off0 + d * off_step
    strip = min(rows, CHUNK)
    for r0 in range(0, rows, strip):
        r = r0 + lax.broadcasted_iota(jnp.int32, (strip, cols), 0)
        c = lax.broadcasted_iota(jnp.int32, (strip, cols), 1)
        qi, ki = (c, r) if transposed else (r, c)
        q_lo, q_hi = (0, cols - 1) if transposed else (r0, r0 + strip - 1)
        k_lo, k_hi = (r0, r0 + strip - 1) if transposed else (0, cols - 1)
        idx = jnp.clip(off + qi - ki, -REL_CLIP, REL_CLIP) + REL_CLIP
        lo = jnp.clip(off + q_lo - k_hi, -REL_CLIP, REL_CLIP) + REL_CLIP
        hi = jnp.clip(off + q_hi - k_lo, -REL_CLIP, REL_CLIP) + REL_CLIP

        def body(j, acc, idx=idx):
            return jnp.where(idx == j, tab_ref[j * HB + h], acc)

        bias = lax.fori_loop(lo, hi + 1, body, jnp.zeros((strip, cols), F32)) * LOG2E
        if masked:
            dc = d * (off_step // CHUNK) + qi // CHUNK - ki // CHUNK
            bias = jnp.where((dc >= 0) & (dc <= BAND_CHUNKS), bias, NEG)
        o_ref[r0:r0 + strip, :] = bias


def _rel_bias(table, n_off, rows, cols, off0, off_step, masked, transposed):
    kern = functools.partial(_rel_bias_kernel, rows=rows, cols=cols, off0=off0,
                             off_step=off_step, masked=masked, transposed=transposed)
    return pl.pallas_call(
        kern,
        grid=(HB, n_off),
        in_specs=[pl.BlockSpec(memory_space=pltpu.SMEM)],
        out_specs=pl.BlockSpec((None, None, rows, cols), lambda h, d: (h, d, 0, 0)),
        out_shape=jax.ShapeDtypeStruct((HB, n_off, rows, cols), F32),
        compiler_params=_cparams("parallel", "parallel"),
        name="rel_bias",
    )(table.reshape(N_REL * HB))


def _pair_softmax_out(q, ks, vs, bias_fn):
    lane = lax.broadcasted_iota(jnp.int32, (q.shape[0], LANES), 1)
    outs = []
    for hh, qm in enumerate(_split_halves(q)):
        ss = [_dot_nt(qm, k) + bias_fn(hh, j) for j, k in enumerate(ks)]
        m = functools.reduce(jnp.maximum, [jnp.max(s, axis=-1, keepdims=True) for s in ss])
        ps = [jnp.exp2(s - m) for s in ss]
        l = functools.reduce(jnp.add, [jnp.sum(p, axis=-1, keepdims=True) for p in ps])
        o = functools.reduce(jnp.add, [_dot(p.astype(BF16), v) for p, v in zip(ps, vs)])
        outs.append(o / l)
    return jnp.where(lane < 64, outs[0], outs[1])


BAND_TILE = 4 * CHUNK
N_BAND_TILES = BAND // BAND_TILE + 1


def _band_attn_kernel(q_ref, k0_ref, k1_ref, k2_ref, v0_ref, v1_ref, v2_ref, b_ref, o_ref, sa_s, sb_s):
    i = pl.program_id(1)
    k_refs = (k0_ref, k1_ref, k2_ref)
    vt_refs = (v0_ref, v1_ref, v2_ref)
    row = lax.broadcasted_iota(jnp.int32, (LANES, BAND_TILE), 0)

    def produce(h, buf):
        cols = slice((h // 2) * LANES, (h // 2 + 1) * LANES)
        qm = _split_halves(q_ref[:, cols])[h % 2]
        for d, k_ref in enumerate(k_refs):
            buf[d] = jnp.where(i >= d, _dot_nt(k_ref[:, cols], qm) + b_ref[h, d], NEG)

    def consume(h, buf):
        cols = slice((h // 2) * LANES, (h // 2 + 1) * LANES)
        ss = [buf[d] for d in range(N_BAND_TILES)]
        m = functools.reduce(jnp.maximum, [jnp.max(s, axis=0, keepdims=True) for s in ss])
        ps = [jnp.exp2(s - m) for s in ss]
        l = functools.reduce(jnp.add, [jnp.sum(p, axis=0, keepdims=True) for p in ps])
        ot = functools.reduce(jnp.add, [_dot(r[cols, :], p.astype(BF16)) for r, p in zip(vt_refs, ps)])
        return ot / l

    bufs = (sa_s, sb_s)
    produce(0, bufs[0])
    prev = None
    for h in range(HB):
        if h + 1 < HB:
            produce(h + 1, bufs[(h + 1) % 2])
        ot = consume(h, bufs[h % 2])
        if h % 2 == 1:
            cols = slice((h // 2) * LANES, (h // 2 + 1) * LANES)
            o_ref[:, cols] = jnp.where(row < DH_B, prev, ot).T.astype(BF16)
        prev = ot


def _band_attn_prompt(q, k, vt, bias_t):
    B, L, _ = q.shape
    t = BAND_TILE
    per = vt.shape[3] // t
    qspec = pl.BlockSpec((None, t, W_B), lambda b, i: (b, i, 0))
    kspec = lambda d: pl.BlockSpec((None, t, W_B), lambda b, i: (b, jnp.maximum(i - d, 0), 0))
    vspec = lambda d: pl.BlockSpec(
        (None, None, W_B, t),
        lambda b, i: (b, jnp.maximum(i - d, 0) // per, 0, jnp.maximum(i - d, 0) % per))
    return pl.pallas_call(
        _band_attn_kernel,
        grid=(B, L // t),
        in_specs=[qspec, kspec(0), kspec(1), kspec(2), vspec(0), vspec(1), vspec(2),
                  pl.BlockSpec((HB, N_BAND_TILES, t, t), lambda b, i: (0, 0, 0, 0),
                               pipeline_mode=pl.Buffered(1))],
        out_specs=qspec,
        out_shape=jax.ShapeDtypeStruct((B, L, W_B), BF16),
        scratch_shapes=[pltpu.VMEM((N_BAND_TILES, t, t), F32)] * 2,
        compiler_params=_cparams("parallel", "parallel"),
        name="band_attn",
    )(q, k, k, k, vt, vt, vt, bias_t)


def _band_attn_sample_kernel(q_ref, ck_ref, cv_ref, k_ref, v_ref, bc_ref, bn_ref, o_ref):
    for hp in range(HB // 2):
        cols = slice(hp * LANES, (hp + 1) * LANES)
        ks = [ck_ref[:, cols], k_ref[:, cols]]
        vs = [cv_ref[:, cols], v_ref[:, cols]]

        def bias_fn(hh, j, hp=hp):
            return (bc_ref if j == 0 else bn_ref)[2 * hp + hh, 0]

        o_ref[:, cols] = _pair_softmax_out(q_ref[:, cols], ks, vs, bias_fn).astype(BF16)


def _band_attn_sample(q, ck, cv, k, v, bias_c, bias_n):
    B, T, _ = q.shape
    P = ck.shape[1]
    new = pl.BlockSpec((None, T, W_B), lambda b: (b, 0, 0))
    old = pl.BlockSpec((None, P, W_B), lambda b: (b, 0, 0))
    return pl.pallas_call(
        _band_attn_sample_kernel,
        grid=(B,),
        in_specs=[new, old, old, new, new,
                  pl.BlockSpec((HB, 1, T, P), lambda b: (0, 0, 0, 0)),
                  pl.BlockSpec((HB, 1, T, T), lambda b: (0, 0, 0, 0))],
        out_specs=new,
        out_shape=jax.ShapeDtypeStruct((B, T, W_B), BF16),
        compiler_params=_cparams("parallel"),
        name="band_attn_sample",
    )(q, ck, cv, k, v, bias_c, bias_n)


N_SLAB = N_STATE // MXU_DIM


def _ssm_kernel(u_ref, wxr_ref, wxi_ref, ar_ref, ai_ref, h0r_ref, h0i_ref, wcr_ref, wci_ref,
                d_ref, wglu_ref, oc_ref, hro_ref, hio_ref, xr_s, xi_s, hr_s, hi_s, io_s, *, tt, nb, half):
    i = pl.program_id(0)

    @pl.when(i == 0)
    def _():
        hr_s[...] = h0r_ref[...]
        hi_s[...] = h0i_ref[...]

    n_ch = W_C // LANES
    for b in range(nb):
        for s in range(n_ch):
            c0 = b * W_C + s * LANES
            io_s[s, pl.ds(b, tt, stride=nb), :] = u_ref[:, c0:c0 + LANES]
    u = jnp.concatenate([io_s[s] for s in range(n_ch)], axis=-1)
    ub = u.astype(BF16)
    for j in range(N_SLAB):
        us = ub[:, LANES * (j // 2):LANES * (j // 2 + 1)]
        xr_s[:, MXU_DIM * j:MXU_DIM * (j + 1)] = _dot(us, wxr_ref[j])
        xi_s[:, MXU_DIM * j:MXU_DIM * (j + 1)] = _dot(us, wxi_ref[j])

    for c in range(N_STATE // half):
        cols = slice(c * half, (c + 1) * half)
        ar = jnp.broadcast_to(ar_ref[:, cols], (nb, half))
        ai = jnp.broadcast_to(ai_ref[:, cols], (nb, half))

        def step(t, carry, cols=cols, ar=ar, ai=ai):
            hr, hi = carry
            r0 = pl.multiple_of(t * nb, nb)
            nhr = ar * hr - ai * hi + xr_s[pl.ds(r0, nb), cols]
            nhi = ar * hi + ai * hr + xi_s[pl.ds(r0, nb), cols]
            xr_s[pl.ds(r0, nb), cols] = nhr
            xi_s[pl.ds(r0, nb), cols] = nhi
            return nhr, nhi

        hr, hi = lax.fori_loop(0, tt, step, (hr_s[:, cols], hi_s[:, cols]))
        hr_s[:, cols] = hr
        hi_s[:, cols] = hi

    hro_ref[...] = hr_s[...]
    hio_ref[...] = hi_s[...]

    ys = []
    for s in range(W_C // LANES):
        acc = None
        for j in (2 * s, 2 * s + 1):
            hrb = xr_s[:, MXU_DIM * j:MXU_DIM * (j + 1)].astype(BF16)
            hib = xi_s[:, MXU_DIM * j:MXU_DIM * (j + 1)].astype(BF16)
            part = _dot(hrb, wcr_ref[j]) + _dot(hib, wci_ref[j])
            acc = part if acc is None else acc + part
        ys.append(acc)
    y = jnp.concatenate(ys, axis=-1) + d_ref[...] * u
    ge = 0.5 * y * (1.0 + jnp.tanh(math.sqrt(2.0 / math.pi) * (y + 0.044715 * (y * y * y))))
    gl = _dot(ge.astype(BF16), wglu_ref[...])
    oc = gl[:, :W_C] * _sigmoid(gl[:, W_C:])
    for s in range(n_ch):
        io_s[s] = oc[:, s * LANES:(s + 1) * LANES]
    for b in range(nb):
        for s in range(n_ch):
            c0 = b * W_C + s * LANES
            oc_ref[:, c0:c0 + LANES] = io_s[s, pl.ds(b, tt, stride=nb), :].astype(BF16)


def _ssm(u_tm, wxr, wxi, ar, ai, h0r, h0i, wcr, wci, d, wglu, tt):
    L = u_tm.shape[0]
    nb = h0r.shape[0]
    rows = tt * nb
    half = N_STATE // 2
    kern = functools.partial(_ssm_kernel, tt=tt, nb=nb, half=half)
    full = lambda shape: pl.BlockSpec(shape, lambda i: (0,) * len(shape))
    return pl.pallas_call(
        kern,
        grid=(L // tt,),
        in_specs=[pl.BlockSpec((tt, nb * W_C), lambda i: (i, 0)),
                  full((N_SLAB, LANES, MXU_DIM)), full((N_SLAB, LANES, MXU_DIM)),
                  full((1, N_STATE)), full((1, N_STATE)),
                  full((nb, N_STATE)), full((nb, N_STATE)),
                  full((N_SLAB, MXU_DIM, LANES)), full((N_SLAB, MXU_DIM, LANES)),
                  full((1, W_C)), full((W_C, 2 * W_C))],
        out_specs=[pl.BlockSpec((tt, nb * W_C), lambda i: (i, 0)),
                   full((nb, N_STATE)), full((nb, N_STATE))],
        out_shape=[jax.ShapeDtypeStruct((L, nb * W_C), BF16),
                   jax.ShapeDtypeStruct((nb, N_STATE), F32),
                   jax.ShapeDtypeStruct((nb, N_STATE), F32)],
        scratch_shapes=[pltpu.VMEM((rows, N_STATE), F32), pltpu.VMEM((rows, N_STATE), F32),
                        pltpu.VMEM((nb, N_STATE), F32), pltpu.VMEM((nb, N_STATE), F32),
                        pltpu.VMEM((W_C // LANES, rows, LANES), F32)],
        compiler_params=_cparams("arbitrary"),
        name="ssm",
    )(u_tm, wxr, wxi, ar, ai, h0r, h0i, wcr, wci, d, wglu)


def _ssm_weights(w):
    ar, ai = w['ssm_a_re'].astype(F32), w['ssm_a_im'].astype(F32)
    dt = jnp.exp(w['ssm_log_dt'].astype(F32))[:, None]
    mag = jnp.exp(ar * dt)
    abar_r, abar_i = mag * jnp.cos(ai * dt), mag * jnp.sin(ai * dt)
    den = ar * ar + ai * ai
    nr, ni = abar_r - 1.0, abar_i
    z_r = (nr * ar + ni * ai) / den
    z_i = (ni * ar - nr * ai) / den
    br, bi = w['ssm_b_re'].astype(F32), w['ssm_b_im'].astype(F32)
    bb_r = z_r[..., None] * br - z_i[..., None] * bi
    bb_i = z_r[..., None] * bi + z_i[..., None] * br

    gps = MXU_DIM // P_STATE
    eye = jnp.eye(gps, dtype=F32)

    def x_slabs(bb):
        b4 = bb.reshape(N_SLAB, gps, P_STATE, GC)
        blk = jnp.einsum('jgpc,gh->jhcgp', b4, eye).reshape(N_SLAB, gps * GC, MXU_DIM)
        zero = jnp.zeros_like(blk)
        even = jnp.concatenate([blk, zero], axis=1)
        odd = jnp.concatenate([zero, blk], axis=1)
        sel = (jnp.arange(N_SLAB) % 2 == 0)[:, None, None]
        return jnp.where(sel, even, odd).astype(BF16)

    def y_slabs(cc):
        c4 = cc.reshape(N_SLAB, gps, GC, P_STATE)
        blk = jnp.einsum('jgcp,gh->jgphc', c4, eye).reshape(N_SLAB, MXU_DIM, gps * GC)
        zero = jnp.zeros_like(blk)
        even = jnp.concatenate([blk, zero], axis=2)
        odd = jnp.concatenate([zero, blk], axis=2)
        sel = (jnp.arange(N_SLAB) % 2 == 0)[:, None, None]
        return jnp.where(sel, even, odd).astype(BF16)

    return dict(wxr=x_slabs(bb_r), wxi=x_slabs(bb_i),
                ar=abar_r.reshape(1, N_STATE), ai=abar_i.reshape(1, N_STATE),
                wcr=y_slabs(w['ssm_c_re'].astype(F32)), wci=y_slabs(-w['ssm_c_im'].astype(F32)))


def _merge_out_kernel(x_ref, oa_ref, ob_ref, oc_ref, g_ref, wg_ref, wa_ref, wb_ref, wc_ref, wo_ref, o_ref):
    x = x_ref[...]
    h = _rms(x, g_ref[...]).astype(BF16)
    merged = None
    for j, (br_ref, w_ref) in enumerate(((oa_ref, wa_ref), (ob_ref, wb_ref), (oc_ref, wc_ref))):
        gate = _sigmoid(_dot(h, wg_ref[:, j * D_MODEL:(j + 1) * D_MODEL]))
        term = gate * _dot(br_ref[...], w_ref[...])
        merged = term if merged is None else merged + term
    o_ref[...] = x + _dot(merged.astype(BF16), wo_ref[...])


def _merge_out(x, oa, ob, oc_tm, g, wg, wa, wb, wc, wo, tm):
    B, L, D = x.shape
    full = lambda shape: pl.BlockSpec(shape, lambda b, t: (0,) * len(shape))
    br = pl.BlockSpec((None, tm, 512), lambda b, t: (b, t, 0))
    return pl.pallas_call(
        _merge_out_kernel,
        grid=(B, L // tm),
        in_specs=[pl.BlockSpec((None, tm, D), lambda b, t: (b, t, 0)), br, br,
                  pl.BlockSpec((tm, 512), lambda b, t: (t, b)),
                  full((1, D)), full((D, 3 * D)), full((W_A, D)), full((W_B, D)), full((W_C, D)), full((D, D))],
        out_specs=pl.BlockSpec((None, tm, D), lambda b, t: (b, t, 0)),
        out_shape=jax.ShapeDtypeStruct((B, L, D), F32),
        compiler_params=_cparams("parallel", "parallel"),
        name="merge_out",
    )(x, oa, ob, oc_tm, g, wg, wa, wb, wc, wo)


FF_CHUNK = 2 * MXU_DIM


def _swiglu(h, w1_ref, w3_ref, w2_ref):
    n_ff = w1_ref.shape[1]
    acc = None
    for c0 in range(0, n_ff, FF_CHUNK):
        c1 = min(c0 + FF_CHUNK, n_ff)
        a = _dot(h, w1_ref[:, c0:c1])
        b = _dot(h, w3_ref[:, c0:c1])
        part = _dot((a * _sigmoid(a) * b).astype(BF16), w2_ref[c0:c1, :])
        acc = part if acc is None else acc + part
    return acc


def _ffn_kernel(x_ref, g_ref, w1_ref, w3_ref, w2_ref, fg_ref, o_ref, *, final_norm):
    x = x_ref[...]
    h = _rms(x, g_ref[...]).astype(BF16)
    y = x + _swiglu(h, w1_ref, w3_ref, w2_ref)
    o_ref[...] = _rms(y, fg_ref[...]) if final_norm else y


def _ffn(x, g, w1, w3, w2, final_g, final_norm, tm):
    B, L, D = x.shape
    full = lambda shape: pl.BlockSpec(shape, lambda b, t: (0,) * len(shape), pipeline_mode=pl.Buffered(1))
    tok = pl.BlockSpec((None, tm, D), lambda b, t: (b, t, 0))
    return pl.pallas_call(
        functools.partial(_ffn_kernel, final_norm=final_norm),
        grid=(B, L // tm),
        in_specs=[tok, full((1, D)), full((D, D_FF)), full((D, D_FF)), full((D_FF, D)), full((1, D))],
        out_specs=tok,
        out_shape=jax.ShapeDtypeStruct((B, L, D), F32),
        compiler_params=_cparams("parallel", "parallel"),
        name="ffn",
    )(x, g, w1, w3, w2, final_g)


R_I1, R_I2, R_R1, R_R2, R_G1, R_G2 = range(6)


def _moe_route_kernel(x_ref, g_ref, rwh_ref, rwl_ref, tri_ref, route_ref, cnt_ref, cnt_s):
    @pl.when(pl.program_id(0) == 0)
    def _():
        cnt_s[...] = jnp.zeros_like(cnt_s)

    lane = lax.broadcasted_iota(jnp.int32, route_ref.shape, 1)
    hf = _rms(x_ref[...], g_ref[...])
    h_hi = hf.astype(BF16)
    h_lo = (hf - h_hi.astype(F32)).astype(BF16)
    logits = _dot(h_hi, rwh_ref[...]) + (_dot(h_hi, rwl_ref[...]) + _dot(h_lo, rwh_ref[...]))
    logits = jnp.where(lane < N_EXPERTS, logits, -jnp.inf)
    v1 = jnp.max(logits, axis=-1, keepdims=True)
    i1 = jnp.min(jnp.where(logits == v1, lane, LANES), axis=-1, keepdims=True)
    rest = jnp.where(lane == i1, -jnp.inf, logits)
    v2 = jnp.max(rest, axis=-1, keepdims=True)
    i2 = jnp.min(jnp.where(rest == v2, lane, LANES), axis=-1, keepdims=True)
    ex = jnp.exp(v2 - v1)
    g1 = 1.0 / (1.0 + ex)
    g2 = ex / (1.0 + ex)

    oh1 = (lane == i1).astype(F32)
    oh2 = (lane == i2).astype(F32)
    oh = oh1 + oh2
    incl = _dot(tri_ref[...], oh.astype(BF16))
    rank = cnt_s[...] + incl - oh
    r1 = jnp.sum(oh1 * rank, axis=-1, keepdims=True)
    r2 = jnp.sum(oh2 * rank, axis=-1, keepdims=True)
    cnt_s[...] += jnp.sum(oh, axis=0, keepdims=True)

    rec = jnp.zeros(route_ref.shape, F32)
    for ln, val in ((R_I1, i1.astype(F32)), (R_I2, i2.astype(F32)), (R_R1, r1), (R_R2, r2),
                    (R_G1, g1), (R_G2, g2)):
        rec = jnp.where(lane == ln, val, rec)
    route_ref[...] = rec
    cnt_ref[...] = cnt_s[...]


def _moe_route(x, g, rw, tm):
    n, d = x.shape
    tri = (jnp.arange(tm)[:, None] >= jnp.arange(tm)[None, :]).astype(BF16)
    rw_hi = rw.astype(BF16)
    rw_lo = (rw - rw_hi.astype(F32)).astype(BF16)
    full = lambda shape: pl.BlockSpec(shape, lambda i: (0,) * len(shape))
    return pl.pallas_call(
        _moe_route_kernel,
        grid=(n // tm,),
        in_specs=[pl.BlockSpec((tm, d), lambda i: (i, 0)), full((1, d)), full((d, LANES)), full((d, LANES)),
                  full((tm, tm))],
        out_specs=[pl.BlockSpec((tm, LANES), lambda i: (i, 0)), full((1, LANES))],
        out_shape=[jax.ShapeDtypeStruct((n, LANES), F32), jax.ShapeDtypeStruct((1, LANES), F32)],
        scratch_shapes=[pltpu.VMEM((1, LANES), F32)],
        compiler_params=_cparams("arbitrary"),
        name="moe_route",
    )(x, g, rw_hi, rw_lo, tri)


def _row_copies(pos_ref, tm, make_copy):
    def body(k, c):
        base = pl.multiple_of(k * SUBLANES, SUBLANES)
        for j in range(SUBLANES):
            for s in range(2):
                make_copy(s, base + j, pos_ref[s, base + j]).start()
        return c

    lax.fori_loop(0, tm // SUBLANES, body, 0)


def _moe_dispatch_kernel(pos_ref, x_ref, xs_in_ref, xs_ref, sem, *, tm):
    del xs_in_ref
    _row_copies(pos_ref, tm, lambda s, i, p: pltpu.make_async_copy(
        x_ref.at[pl.ds(i, 1)], xs_ref.at[pl.ds(p, 1)], sem))
    pltpu.make_async_copy(xs_ref.at[pl.ds(0, 2 * tm)], xs_ref.at[pl.ds(0, 2 * tm)], sem).wait()


def _moe_dispatch(x, pos, n_rows, tm):
    n, d = x.shape
    return pl.pallas_call(
        functools.partial(_moe_dispatch_kernel, tm=tm),
        grid=(n // tm,),
        in_specs=[pl.BlockSpec((None, 2, tm), lambda i: (i, 0, 0), memory_space=pltpu.SMEM),
                  pl.BlockSpec((tm, d), lambda i: (i, 0)),
                  pl.BlockSpec(memory_space=pl.ANY)],
        out_specs=pl.BlockSpec(memory_space=pl.ANY),
        out_shape=jax.ShapeDtypeStruct((n_rows, d), F32),
        input_output_aliases={2: 0},
        scratch_shapes=[pltpu.SemaphoreType.DMA(())],
        compiler_params=_cparams("arbitrary"),
        name="moe_dispatch",
    )(pos, x, jnp.zeros((n_rows, d), F32))


def _moe_expert_kernel(e_ref, rows_ref, xs_ref, g_ref, w1_ref, w3_ref, w2_ref, ys_ref):
    rows = rows_ref[pl.program_id(0)]

    @pl.when(rows > 0)
    def _():
        h = _rms(xs_ref[...], g_ref[...]).astype(BF16)
        ys_ref[...] = _swiglu(h, w1_ref, w3_ref, w2_ref)

    @pl.when(rows == 0)
    def _():
        ys_ref[...] = jnp.zeros_like(ys_ref)


def _moe_experts(xs, g, w1, w3, w2, tile_e, tile_rows, t):
    p, d = xs.shape
    wspec = lambda shape: pl.BlockSpec((None,) + shape, lambda i, e, rows: (e[i], 0, 0))
    row = pl.BlockSpec((t, d), lambda i, e, rows: (i, 0))
    return pl.pallas_call(
        _moe_expert_kernel,
        grid_spec=pltpu.PrefetchScalarGridSpec(
            num_scalar_prefetch=2,
            grid=(p // t,),
            in_specs=[row, pl.BlockSpec((1, d), lambda i, e, rows: (0, 0)),
                      wspec((d, D_FF_E)), wspec((d, D_FF_E)), wspec((D_FF_E, d))],
            out_specs=row),
        out_shape=jax.ShapeDtypeStruct((p, d), F32),
        compiler_params=_cparams("arbitrary"),
        name="moe_experts",
    )(tile_e, tile_rows, xs, g, w1, w3, w2)


def _moe_combine_kernel(pos_ref, x_ref, route_ref, fg_ref, ys_ref, o_ref, ybuf, sem, *, tm, final_norm):
    _row_copies(pos_ref, tm, lambda s, i, p: pltpu.make_async_copy(
        ys_ref.at[pl.ds(p, 1)], ybuf.at[s, pl.ds(i, 1)], sem))
    for s in range(2):
        pltpu.make_async_copy(ys_ref.at[pl.ds(0, tm)], ybuf.at[s], sem).wait()
    rec = route_ref[...]
    lane = lax.broadcasted_iota(jnp.int32, rec.shape, 1)
    g1 = jnp.sum(jnp.where(lane == R_G1, rec, 0.0), axis=-1, keepdims=True)
    g2 = jnp.sum(jnp.where(lane == R_G2, rec, 0.0), axis=-1, keepdims=True)
    y = x_ref[...] + (g1 * ybuf[0] + g2 * ybuf[1])
    o_ref[...] = _rms(y, fg_ref[...]) if final_norm else y


def _moe_combine(x, route, pos, ys, final_g, final_norm, tm):
    n, d = x.shape
    tok = pl.BlockSpec((tm, d), lambda i: (i, 0))
    return pl.pallas_call(
        functools.partial(_moe_combine_kernel, tm=tm, final_norm=final_norm),
        grid=(n // tm,),
        in_specs=[pl.BlockSpec((None, 2, tm), lambda i: (i, 0, 0), memory_space=pltpu.SMEM),
                  tok, pl.BlockSpec((tm, LANES), lambda i: (i, 0)),
                  pl.BlockSpec((1, d), lambda i: (0, 0)), pl.BlockSpec(memory_space=pl.ANY)],
        out_specs=tok,
        out_shape=jax.ShapeDtypeStruct((n, d), F32),
        scratch_shapes=[pltpu.VMEM((2, tm, d), F32), pltpu.SemaphoreType.DMA(())],
        compiler_params=_cparams("arbitrary"),
        name="moe_combine",
    )(pos, x, route, final_g, ys)


def _moe(x, g, rw, w1, w3, w2, final_g, final_norm):
    B, L, D = x.shape
    n = B * L
    tm = min(512, n)
    xf = x.reshape(n, D)
    route, counts = _moe_route(xf, g, rw, tm)

    idx = route[:, :4].astype(jnp.int32)
    cnt = counts[0, :N_EXPERTS].astype(jnp.int32)
    tiles_e = (cnt + tm - 1) // tm
    ends = jnp.cumsum(tiles_e)
    starts = ends - tiles_e
    n_used = ends[-1]
    n_tiles = 2 * n // tm + N_EXPERTS
    pos = jnp.stack([starts[idx[:, R_I1]] * tm + idx[:, R_R1],
                     starts[idx[:, R_I2]] * tm + idx[:, R_R2]])
    pos = pos.reshape(2, n // tm, tm).transpose(1, 0, 2)
    tile = jnp.arange(n_tiles, dtype=jnp.int32)
    tile_e = jnp.searchsorted(ends, jnp.minimum(tile, n_used - 1), side='right').astype(jnp.int32)
    tile_rows = jnp.where(tile < n_used, jnp.clip(cnt[tile_e] - (tile - starts[tile_e]) * tm, 0, tm), 0)

    xs = _moe_dispatch(xf, pos, n_tiles * tm, tm)
    ys = _moe_experts(xs, g, w1, w3, w2, tile_e, tile_rows.astype(jnp.int32), tm)
    out = _moe_combine(xf, route, pos, ys, final_g, final_norm, tm)
    return out.reshape(B, L, D)


def _pick(n, cands):
    for c in cands:
        if n % c == 0:
            return c
    return n


def _layer(x, l, w, cache, final_g, last, dkv, depth):
    B, L, D = x.shape
    assert B == SUBLANES, "the SSM kernel puts the streams of one time step on the 8 sublanes"
    lam_init = 0.8 - 0.6 * math.exp(-0.3 * l)
    tm = _pick(L, (512, 256, 128, 64, 32))
    n_keep = min(BAND, L)
    assert n_keep == tm, "the band cache rows must be exactly the last token tile"
    row = lambda a: a.reshape(1, -1).astype(F32)
    prompt = cache is None

    w_in = w['w_in'].astype(BF16)
    dk, dv = dkv if dkv is not None else (None, None)
    p = _norm_proj(x, row(w['norm1_g']), w_in[:, :N_QKVU], tm, prompt, l, depth, dk, dv)

    lam_vecs = jnp.stack([w['lam_q1'], w['lam_k1'], w['lam_q2'], w['lam_k2']]).astype(F32)
    subln_g = w['subln_g'].astype(F32)
    table = w['rel_bias'].astype(F32)
    if prompt:
        oa = _diff_attn_prompt(p['qa'], p['ka_b'], p['va_b'], lam_vecs, subln_g, lam_init)
        bias_t = _rel_bias(table, N_BAND_TILES, BAND_TILE, BAND_TILE, 0, BAND_TILE, True, True)
        ob = _band_attn_prompt(p['qb'], p['kb_b'], p['vb_b'], bias_t)
        h0r = jnp.zeros((B, N_STATE), F32)
        h0i = jnp.zeros((B, N_STATE), F32)
    else:
        ck_a, cv_a, ck_b, cv_b, h0_re, h0_im = cache
        P = ck_a.shape[1]
        bp = ck_b.shape[1]
        oa = _diff_attn_sample(p['qa'], ck_a.reshape(B, P, W_A).astype(BF16), cv_a.reshape(B, P, W_A).astype(BF16),
                               p['ka_b'], p['va_b'], lam_vecs, subln_g, lam_init)
        bias_c = _rel_bias(table, 1, L, bp, bp, 0, False, False)
        bias_n = _rel_bias(table, 1, L, L, 0, 0, False, False)
        ob = _band_attn_sample(p['qb'], ck_b.reshape(B, bp, W_B).astype(BF16), cv_b.reshape(B, bp, W_B).astype(BF16),
                               p['kb_b'], p['vb_b'], bias_c, bias_n)
        h0r = h0_re.reshape(B, N_STATE).astype(F32)
        h0i = h0_im.reshape(B, N_STATE).astype(F32)

    sw = _ssm_weights(w)
    oc_tm, hr, hi = _ssm(p['u_tm'], sw['wxr'], sw['wxi'], sw['ar'], sw['ai'], h0r, h0i,
                         sw['wcr'], sw['wci'], row(w['ssm_d']), w['w_glu'].astype(BF16), _pick(L, (64, 32)))

    x = _merge_out(x, oa, ob, oc_tm, row(w['norm1_g']), w_in[:, N_QKVU:],
                   w['w_br_a'].astype(BF16), w['w_br_b'].astype(BF16), w['w_br_c'].astype(BF16),
                   w['w_out'].astype(BF16), tm)

    fg = row(final_g)
    if l % 2 == 0:
        x = _ffn(x, row(w['norm2_g']), w['ffn_w1'].astype(BF16), w['ffn_w3'].astype(BF16),
                 w['ffn_w2'].astype(BF16), fg, last, tm)
    else:
        rw = jnp.pad(w['router_w'].astype(F32), ((0, 0), (0, LANES - N_EXPERTS)))
        x = _moe(x, row(w['norm2_g']), rw, w['moe_w1'].astype(BF16),
                 w['moe_w3'].astype(BF16), w['moe_w2'].astype(BF16), fg, last)

    small = (p['kb_last'].reshape(B, n_keep, HB, DH_B), p['vb_last'].reshape(B, n_keep, HB, DH_B),
             hr.reshape(B, N_GROUPS, P_STATE), hi.reshape(B, N_GROUPS, P_STATE))
    return x, (p['dk'], p['dv']), small


_PER_LAYER = ('norm1_g', 'w_in', 'lam_q1', 'lam_k1', 'lam_q2', 'lam_k2', 'subln_g', 'rel_bias',
              'ssm_a_re', 'ssm_a_im', 'ssm_log_dt', 'ssm_b_re', 'ssm_b_im', 'ssm_c_re', 'ssm_c_im',
              'ssm_d', 'w_glu', 'w_br_a', 'w_br_b', 'w_br_c', 'w_out', 'norm2_g')


def kernel(x_prompt, x_sample, cache_dk, cache_dv, cache_bk, cache_bv, state_ssm_re, state_ssm_im,
           norm1_g, w_in, lam_q1, lam_k1, lam_q2, lam_k2, subln_g, rel_bias, ssm_a_re, ssm_a_im,
           ssm_log_dt, ssm_b_re, ssm_b_im, ssm_c_re, ssm_c_im, ssm_d, w_glu, w_br_a, w_br_b, w_br_c,
           w_out, norm2_g, ffn_w1, ffn_w3, ffn_w2, router_w, moe_w1, moe_w3, moe_w2, final_g):
    stacked = dict(norm1_g=norm1_g, w_in=w_in, lam_q1=lam_q1, lam_k1=lam_k1, lam_q2=lam_q2, lam_k2=lam_k2,
                   subln_g=subln_g, rel_bias=rel_bias, ssm_a_re=ssm_a_re, ssm_a_im=ssm_a_im,
                   ssm_log_dt=ssm_log_dt, ssm_b_re=ssm_b_re, ssm_b_im=ssm_b_im, ssm_c_re=ssm_c_re,
                   ssm_c_im=ssm_c_im, ssm_d=ssm_d, w_glu=w_glu, w_br_a=w_br_a, w_br_b=w_br_b,
                   w_br_c=w_br_c, w_out=w_out, norm2_g=norm2_g)
    depth = w_in.shape[0]
    yp, ys = x_prompt, x_sample
    dkv_p, dkv_s = None, None
    small_p, small_s = [], []
    for l in range(depth):
        w = {name: stacked[name][l] for name in _PER_LAYER}
        if l % 2 == 0:
            w['ffn_w1'], w['ffn_w3'], w['ffn_w2'] = ffn_w1[l // 2], ffn_w3[l // 2], ffn_w2[l // 2]
        else:
            w['router_w'], w['moe_w1'] = router_w[l // 2], moe_w1[l // 2]
            w['moe_w3'], w['moe_w2'] = moe_w3[l // 2], moe_w2[l // 2]
        last = l == depth - 1
        yp, dkv_p, sp = _layer(yp, l, w, None, final_g, last, dkv_p, depth)
        ys, dkv_s, ss = _layer(ys, l, w, (cache_dk[l], cache_dv[l], cache_bk[l], cache_bv[l],
                                          state_ssm_re[l], state_ssm_im[l]), final_g, last, dkv_s, depth)
        small_p.append(sp)
        small_s.append(ss)
    outs = [yp, ys]
    for dkv, small in ((dkv_p, small_p), (dkv_s, small_s)):
        outs += list(dkv)
        for j in range(4):
            outs.append(jnp.stack([s[j] for s in small]))
    return tuple(outs)
```

```python
import functools
import math

import jax
import jax.numpy as jnp
from jax import lax
from jax.experimental import pallas as pl
from jax.experimental.pallas import tpu as pltpu

F32 = jnp.float32
BF16 = jnp.bfloat16

D_MODEL = 1024
CHUNK = 64
HA = 4
DH_A = 64
DV_A = 2 * DH_A
W_A = HA * DV_A
HB = 8
DH_B = 64
W_B = HB * DH_B
BAND_CHUNKS = 8
BAND = BAND_CHUNKS * CHUNK
REL_CLIP = 128
N_REL = 2 * REL_CLIP + 1
GC = 16
N_GROUPS = 32
W_C = N_GROUPS * GC
P_STATE = 64
N_STATE = N_GROUPS * P_STATE
D_FF = 11 * D_MODEL // 4
N_EXPERTS = 8
D_FF_E = D_FF // 2
EPS = 1e-6
NEG = -1e30
LOG2E = math.log2(math.e)
N_QKVU = 7 * 512

LANES = 128
SUBLANES = 8
MXU_DIM = 256
VMEM_LIMIT = 56 * 1024 * 1024


def _cparams(*sem):
    return pltpu.CompilerParams(dimension_semantics=sem, vmem_limit_bytes=VMEM_LIMIT)


def _rms(x, g):
    return x * lax.rsqrt(jnp.mean(x * x, axis=-1, keepdims=True) + EPS) * g


def _sigmoid(x):
    return 1.0 / (1.0 + jnp.exp(-x))


def _dot(a, b):
    return jnp.dot(a, b, preferred_element_type=F32)


def _dot_nt(a, b):
    return lax.dot_general(a, b, (((1,), (1,)), ((), ())), preferred_element_type=F32)


def _norm_proj_kernel(*refs, transpose_v, n_alias):
    x_ref, g_ref, w_ref = refs[:3]
    (qa_ref, dk_ref, dv_ref, kab_ref, vab_ref,
     qb_ref, kbl_ref, vbl_ref, kbb_ref, vbb_ref, u_ref) = refs[3 + n_alias:]
    h = _rms(x_ref[...], g_ref[...]).astype(BF16)

    def proj(c):
        return _dot(h, w_ref[:, c * 512:(c + 1) * 512])

    def store_heads(o_ref, y):
        for hd in range(HA):
            o_ref[:, hd, :] = y[:, hd * DV_A:(hd + 1) * DV_A]

    qa_ref[...] = (proj(0) * (DH_A ** -0.5 * LOG2E)).astype(BF16)
    ka = proj(1)
    store_heads(dk_ref, ka)
    kab_ref[...] = ka.astype(BF16)
    va = proj(2)
    store_heads(dv_ref, va)
    vab_ref[...] = (va.T if transpose_v else va).astype(BF16)
    qb_ref[...] = (proj(3) * (DH_B ** -0.5 * LOG2E)).astype(BF16)
    kb = proj(4)
    kbl_ref[...] = kb
    kbb_ref[...] = kb.astype(BF16)
    vb = proj(5)
    vbl_ref[...] = vb
    vbb_ref[...] = (vb.T if transpose_v else vb).astype(BF16)
    u_ref[...] = proj(6)


def _norm_proj(x, g, w_qkvu, tm, transpose_v, layer, depth, dk, dv):
    B, L, D = x.shape
    nt = L // tm
    tok = jax.ShapeDtypeStruct((B, L, 512), BF16)
    tspec = pl.BlockSpec((None, tm, 512), lambda b, t: (b, t, 0))
    vshape, vspec = tok, tspec
    if transpose_v:
        vshape = jax.ShapeDtypeStruct((B, nt, 512, tm), BF16)
        vspec = pl.BlockSpec((None, None, 512, tm), lambda b, t: (b, t, 0, 0))
    cache = jax.ShapeDtypeStruct((depth, B, L, HA, DV_A), F32)
    cspec = pl.BlockSpec((None, None, tm, HA, DV_A), lambda b, t: (layer, b, t, 0, 0))
    last = jax.ShapeDtypeStruct((B, tm, 512), F32)
    lspec = pl.BlockSpec((None, tm, 512), lambda b, t: (b, 0, 0))
    out_shape = [tok, cache, cache, tok, vshape, tok, last, last, tok, vshape,
                 jax.ShapeDtypeStruct((L, B * 512), F32)]
    out_specs = [tspec, cspec, cspec, tspec, vspec, tspec, lspec, lspec, tspec, vspec,
                 pl.BlockSpec((tm, 512), lambda b, t: (t, b))]
    in_specs = [pl.BlockSpec((None, tm, D), lambda b, t: (b, t, 0)),
                pl.BlockSpec((1, D), lambda b, t: (0, 0)),
                pl.BlockSpec((D, N_QKVU), lambda b, t: (0, 0))] + [pl.BlockSpec(memory_space=pl.ANY)] * 2
    if dk is None:
        dk, dv = jnp.zeros(cache.shape, F32), jnp.zeros(cache.shape, F32)
    aliases = {3: 1, 4: 2}
    outs = pl.pallas_call(
        functools.partial(_norm_proj_kernel, transpose_v=transpose_v, n_alias=len(aliases)),
        grid=(B, nt),
        in_specs=in_specs,
        out_specs=out_specs,
        out_shape=out_shape,
        input_output_aliases=aliases,
        compiler_params=_cparams("parallel", "arbitrary"),
        name="norm_proj",
    )(x, g, w_qkvu, dk, dv)
    names = ('qa', 'dk', 'dv', 'ka_b', 'va_b', 'qb', 'kb_last', 'vb_last', 'kb_b', 'vb_b', 'u_tm')
    return dict(zip(names, outs))


def _lam_value(lam_ref, lam_init):
    lv = lam_ref[...]
    e1 = jnp.exp(jnp.sum(lv[0:1, :] * lv[1:2, :], axis=-1, keepdims=True))
    e2 = jnp.exp(jnp.sum(lv[2:3, :] * lv[3:4, :], axis=-1, keepdims=True))
    return e1 - e2 + lam_init


def _subln(o, g, lam_init):
    return _rms(o, g) * (1.0 - lam_init)


def _split_halves(q):
    lane = lax.broadcasted_iota(jnp.int32, q.shape, 1)
    zero = jnp.zeros_like(q)
    return jnp.where(lane < 64, q, zero), jnp.where(lane >= 64, q, zero)


def _diff_attn_kernel(lam_ref, q_ref, k_ref, vt_ref, g_ref, o_ref,
                      m1_s, l1_s, a1_s, m2_s, l2_s, a2_s, sa1_s, sa2_s, sb1_s, sb2_s,
                      xa1_s, xa2_s, xb1_s, xb2_s, *, t, lam_init):
    qi = pl.program_id(2)
    nt = pl.num_programs(2)

    def q_halves(i):
        return _split_halves(q_ref[pl.ds(pl.multiple_of(i * t, t), t), :])

    q_cur = q_halves(qi)
    q_nxt = q_halves(jnp.minimum(qi + 1, nt - 1))

    m1_s[...] = jnp.full_like(m1_s, -jnp.inf)
    m2_s[...] = jnp.full_like(m2_s, -jnp.inf)
    l1_s[...] = jnp.zeros_like(l1_s)
    l2_s[...] = jnp.zeros_like(l2_s)
    a1_s[...] = jnp.zeros_like(a1_s)
    a2_s[...] = jnp.zeros_like(a2_s)

    def update(st, mx, vt, m_s, l_s, a_s):
        m_prev = m_s[...]
        m_new = jnp.maximum(m_prev, mx)
        alpha = jnp.exp2(m_prev - m_new)
        p = jnp.exp2(st - m_new)
        l_s[...] = alpha * l_s[...] + jnp.sum(p, axis=0, keepdims=True)
        a_s[...] = alpha * a_s[...] + _dot(vt, p.astype(BF16))
        m_s[...] = m_new

    buf_a, buf_b = (sa1_s, sa2_s, xa1_s, xa2_s), (sb1_s, sb2_s, xb1_s, xb2_s)

    def produce(kj, buf, qq):
        k = k_ref[pl.ds(pl.multiple_of(kj * t, t), t), :]
        for q, s_ref, x_ref in ((qq[0], buf[0], buf[2]), (qq[1], buf[1], buf[3])):
            st = _dot_nt(k, q)
            s_ref[...] = st
            x_ref[...] = jnp.max(st, axis=0, keepdims=True)

    def consume(kj, buf, masked):
        vt = vt_ref[kj]
        s1, s2 = buf[0][...], buf[1][...]
        if masked:
            kc = lax.broadcasted_iota(jnp.int32, (t, t), 0) // CHUNK
            qc = lax.broadcasted_iota(jnp.int32, (t, t), 1) // CHUNK
            keep = kc <= qc
            s1 = jnp.where(keep, s1, NEG)
            s2 = jnp.where(keep, s2, NEG)
            mx1 = jnp.max(s1, axis=0, keepdims=True)
            mx2 = jnp.max(s2, axis=0, keepdims=True)
        else:
            mx1, mx2 = buf[2][...], buf[3][...]
        update(s1, mx1, vt, m1_s, l1_s, a1_s)
        update(s2, mx2, vt, m2_s, l2_s, a2_s)

    @pl.when(qi == 0)
    def _():
        produce(0, buf_a, q_cur)

    def run(first, second, odd):
        def pair(jj, c):
            kj = 2 * jj
            produce(kj + 1, second, q_cur)
            consume(kj, first, False)
            produce(kj + 2, first, q_cur)
            consume(kj + 1, second, False)
            return c

        lax.fori_loop(0, qi // 2, pair, 0)
        if odd:
            produce(qi, second, q_cur)
            consume(qi - 1, first, False)
            produce(0, first, q_nxt)
            consume(qi, second, True)
        else:
            produce(0, second, q_nxt)
            consume(qi, first, True)

    for r, (first, second) in enumerate(((buf_a, buf_b), (buf_b, buf_a), (buf_b, buf_a), (buf_a, buf_b))):
        pl.when(qi % 4 == r)(functools.partial(run, first, second, r % 2 == 1))

    lam = _lam_value(lam_ref, lam_init)
    ot = a1_s[...] / l1_s[...] - lam * (a2_s[...] / l2_s[...])
    ot = ot * lax.rsqrt(jnp.mean(ot * ot, axis=0, keepdims=True) + EPS) * g_ref[...] * (1.0 - lam_init)
    o_ref[...] = ot.T.astype(BF16)


def _diff_attn_prompt(q, k, vt, lam_vecs, subln_g, lam_init):
    B, L, _ = q.shape
    nt, t = vt.shape[1], vt.shape[3]
    kern = functools.partial(_diff_attn_kernel, t=t, lam_init=lam_init)
    return pl.pallas_call(
        kern,
        grid=(B, HA, nt),
        in_specs=[pl.BlockSpec((4, DH_A), lambda b, h, i: (0, 0)),
                  pl.BlockSpec((None, L, DV_A), lambda b, h, i: (b, 0, h)),
                  pl.BlockSpec((None, L, DV_A), lambda b, h, i: (b, 0, h)),
                  pl.BlockSpec((None, nt, DV_A, t), lambda b, h, i: (b, 0, h, 0)),
                  pl.BlockSpec((None, DV_A, 1), lambda b, h, i: (h, 0, 0))],
        out_specs=pl.BlockSpec((None, t, DV_A), lambda b, h, i: (b, i, h)),
        out_shape=jax.ShapeDtypeStruct((B, L, W_A), BF16),
        scratch_shapes=[pltpu.VMEM((1, t), F32), pltpu.VMEM((1, t), F32), pltpu.VMEM((DV_A, t), F32),
                        pltpu.VMEM((1, t), F32), pltpu.VMEM((1, t), F32), pltpu.VMEM((DV_A, t), F32)]
                       + [pltpu.VMEM((t, t), F32)] * 4 + [pltpu.VMEM((1, t), F32)] * 4,
        compiler_params=_cparams("parallel", "parallel", "arbitrary"),
        name="diff_attn",
    )(lam_vecs, q, k, vt, subln_g.reshape(HA, DV_A, 1))


def _diff_attn_sample_kernel(lam_ref, q_ref, ck_ref, cv_ref, k_ref, v_ref, g_ref, o_ref, *, lam_init):
    lam = _lam_value(lam_ref, lam_init)
    for h in range(HA):
        cols = slice(h * DV_A, (h + 1) * DV_A)
        q1, q2 = _split_halves(q_ref[:, cols])
        ck, cv = ck_ref[:, cols], cv_ref[:, cols]
        k, v = k_ref[:, cols], v_ref[:, cols]

        def one_map(qm):
            sc = _dot_nt(qm, ck)
            sn = _dot_nt(qm, k)
            m = jnp.maximum(jnp.max(sc, axis=-1, keepdims=True), jnp.max(sn, axis=-1, keepdims=True))
            pc = jnp.exp2(sc - m)
            pn = jnp.exp2(sn - m)
            l = jnp.sum(pc, axis=-1, keepdims=True) + jnp.sum(pn, axis=-1, keepdims=True)
            return (_dot(pc.astype(BF16), cv) + _dot(pn.astype(BF16), v)) / l

        o = one_map(q1) - lam * one_map(q2)
        o_ref[:, cols] = _subln(o, g_ref[h], lam_init).astype(BF16)


def _diff_attn_sample(q, ck, cv, k, v, lam_vecs, subln_g, lam_init):
    B, T, _ = q.shape
    P = ck.shape[1]
    kern = functools.partial(_diff_attn_sample_kernel, lam_init=lam_init)
    new = pl.BlockSpec((None, T, W_A), lambda b: (b, 0, 0))
    old = pl.BlockSpec((None, P, W_A), lambda b: (b, 0, 0))
    return pl.pallas_call(
        kern,
        grid=(B,),
        in_specs=[pl.BlockSpec((4, DH_A), lambda b: (0, 0)), new, old, old, new, new,
                  pl.BlockSpec((HA, 1, DV_A), lambda b: (0, 0, 0))],
        out_specs=new,
        out_shape=jax.ShapeDtypeStruct((B, T, W_A), BF16),
        compiler_params=_cparams("parallel"),
        name="diff_attn_sample",
    )(lam_vecs, q, ck, cv, k, v, subln_g.reshape(HA, 1, DV_A))


def _rel_bias_kernel(tab_ref, o_ref, *, rows, cols, off0, off_step, masked, transposed):
    h = pl.program_id(0)
    d = pl.program_id(1)
    off = off0 + d * off_step
    strip = min(rows, CHUNK)
    for r0 in range(0, rows, strip):
        r = r0 + lax.broadcasted_iota(jnp.int32, (strip, cols), 0)
        c = lax.broadcasted_iota(jnp.int32, (strip, cols), 1)
        qi, ki = (c, r) if transposed else (r, c)
        q_lo, q_hi = (0, cols - 1) if transposed else (r0, r0 + strip - 1)
        k_lo, k_hi = (r0, r0 + strip - 1) if transposed else (0, cols - 1)
        idx = jnp.clip(off + qi - ki, -REL_CLIP, REL_CLIP) + REL_CLIP
        lo = jnp.clip(off + q_lo - k_hi, -REL_CLIP, REL_CLIP) + REL_CLIP
        hi = jnp.clip(off + q_hi - k_lo, -REL_CLIP, REL_CLIP) + REL_CLIP

        def body(j, acc, idx=idx):
            return jnp.where(idx == j, tab_ref[j * HB + h], acc)

        bias = lax.fori_loop(lo, hi + 1, body, jnp.zeros((strip, cols), F32)) * LOG2E
        if masked:
            dc = d * (off_step // CHUNK) + qi // CHUNK - ki // CHUNK
            bias = jnp.where((dc >= 0) & (dc <= BAND_CHUNKS), bias, NEG)
        o_ref[r0:r0 + strip, :] = bias


def _rel_bias(table, n_off, rows, cols, off0, off_step, masked, transposed):
    kern = functools.partial(_rel_bias_kernel, rows=rows, cols=cols, off0=off0,
                             off_step=off_step, masked=masked, transposed=transposed)
    return pl.pallas_call(
        kern,
        grid=(HB, n_off),
        in_specs=[pl.BlockSpec(memory_space=pltpu.SMEM)],
        out_specs=pl.BlockSpec((None, None, rows, cols), lambda h, d: (h, d, 0, 0)),
        out_shape=jax.ShapeDtypeStruct((HB, n_off, rows, cols), F32),
        compiler_params=_cparams("parallel", "parallel"),
        name="rel_bias",
    )(table.reshape(N_REL * HB))


def _pair_softmax_out(q, ks, vs, bias_fn):
    lane = lax.broadcasted_iota(jnp.int32, (q.shape[0], LANES), 1)
    outs = []
    for hh, qm in enumerate(_split_halves(q)):
        ss = [_dot_nt(qm, k) + bias_fn(hh, j) for j, k in enumerate(ks)]
        m = functools.reduce(jnp.maximum, [jnp.max(s, axis=-1, keepdims=True) for s in ss])
        ps = [jnp.exp2(s - m) for s in ss]
        l = functools.reduce(jnp.add, [jnp.sum(p, axis=-1, keepdims=True) for p in ps])
        o = functools.reduce(jnp.add, [_dot(p.astype(BF16), v) for p, v in zip(ps, vs)])
        outs.append(o / l)
    return jnp.where(lane < 64, outs[0], outs[1])


BAND_TILE = 4 * CHUNK
N_BAND_TILES = BAND // BAND_TILE + 1


def _band_attn_kernel(q_ref, k0_ref, k1_ref, k2_ref, v0_ref, v1_ref, v2_ref, b_ref, o_ref, sa_s, sb_s):
    i = pl.program_id(1)
    k_refs = (k0_ref, k1_ref, k2_ref)
    vt_refs = (v0_ref, v1_ref, v2_ref)
    row = lax.broadcasted_iota(jnp.int32, (LANES, BAND_TILE), 0)

    def produce(h, buf, guarded):
        cols = slice((h // 2) * LANES, (h // 2 + 1) * LANES)
        qm = _split_halves(q_ref[:, cols])[h % 2]
        for d, k_ref in enumerate(k_refs):
            s = _dot_nt(k_ref[:, cols], qm) + b_ref[h, d]
            if guarded and d > 0:
                s = jnp.where(i >= d, s, NEG)
            buf[d] = s

    def consume(h, buf):
        cols = slice((h // 2) * LANES, (h // 2 + 1) * LANES)
        ss = [buf[d] for d in range(N_BAND_TILES)]
        m = functools.reduce(jnp.maximum, [jnp.max(s, axis=0, keepdims=True) for s in ss])
        ps = [jnp.exp2(s - m) for s in ss]
        l = functools.reduce(jnp.add, [jnp.sum(p, axis=0, keepdims=True) for p in ps])
        ot = functools.reduce(jnp.add, [_dot(r[cols, :], p.astype(BF16)) for r, p in zip(vt_refs, ps)])
        return ot / l

    def all_heads(guarded):
        bufs = (sa_s, sb_s)
        produce(0, bufs[0], guarded)
        prev = None
        for h in range(HB):
            if h + 1 < HB:
                produce(h + 1, bufs[(h + 1) % 2], guarded)
            ot = consume(h, bufs[h % 2])
            if h % 2 == 1:
                cols = slice((h // 2) * LANES, (h // 2 + 1) * LANES)
                o_ref[:, cols] = jnp.where(row < DH_B, prev, ot).T.astype(BF16)
            prev = ot

    @pl.when(i >= N_BAND_TILES - 1)
    def _():
        all_heads(False)

    @pl.when(i < N_BAND_TILES - 1)
    def _():
        all_heads(True)


def _band_attn_prompt(q, k, vt, bias_t):
    B, L, _ = q.shape
    t = BAND_TILE
    per = vt.shape[3] // t
    qspec = pl.BlockSpec((None, t, W_B), lambda b, i: (b, i, 0))
    kspec = lambda d: pl.BlockSpec((None, t, W_B), lambda b, i: (b, jnp.maximum(i - d, 0), 0))
    vspec = lambda d: pl.BlockSpec(
        (None, None, W_B, t),
        lambda b, i: (b, jnp.maximum(i - d, 0) // per, 0, jnp.maximum(i - d, 0) % per))
    return pl.pallas_call(
        _band_attn_kernel,
        grid=(B, L // t),
        in_specs=[qspec, kspec(0), kspec(1), kspec(2), vspec(0), vspec(1), vspec(2),
                  pl.BlockSpec((HB, N_BAND_TILES, t, t), lambda b, i: (0, 0, 0, 0),
                               pipeline_mode=pl.Buffered(1))],
        out_specs=qspec,
        out_shape=jax.ShapeDtypeStruct((B, L, W_B), BF16),
        scratch_shapes=[pltpu.VMEM((N_BAND_TILES, t, t), F32)] * 2,
        compiler_params=_cparams("parallel", "parallel"),
        name="band_attn",
    )(q, k, k, k, vt, vt, vt, bias_t)


def _band_attn_sample_kernel(q_ref, ck_ref, cv_ref, k_ref, v_ref, bc_ref, bn_ref, o_ref):
    for hp in range(HB // 2):
        cols = slice(hp * LANES, (hp + 1) * LANES)
        ks = [ck_ref[:, cols], k_ref[:, cols]]
        vs = [cv_ref[:, cols], v_ref[:, cols]]

        def bias_fn(hh, j, hp=hp):
            return (bc_ref if j == 0 else bn_ref)[2 * hp + hh, 0]

        o_ref[:, cols] = _pair_softmax_out(q_ref[:, cols], ks, vs, bias_fn).astype(BF16)


def _band_attn_sample(q, ck, cv, k, v, bias_c, bias_n):
    B, T, _ = q.shape
    P = ck.shape[1]
    new = pl.BlockSpec((None, T, W_B), lambda b: (b, 0, 0))
    old = pl.BlockSpec((None, P, W_B), lambda b: (b, 0, 0))
    return pl.pallas_call(
        _band_attn_sample_kernel,
        grid=(B,),
        in_specs=[new, old, old, new, new,
                  pl.BlockSpec((HB, 1, T, P), lambda b: (0, 0, 0, 0)),
                  pl.BlockSpec((HB, 1, T, T), lambda b: (0, 0, 0, 0))],
        out_specs=new,
        out_shape=jax.ShapeDtypeStruct((B, T, W_B), BF16),
        compiler_params=_cparams("parallel"),
        name="band_attn_sample",
    )(q, ck, cv, k, v, bias_c, bias_n)


N_SLAB = N_STATE // MXU_DIM


def _ssm_kernel(u_ref, wxr_ref, wxi_ref, ar_ref, ai_ref, h0r_ref, h0i_ref, wcr_ref, wci_ref,
                d_ref, wglu_ref, oc_ref, hro_ref, hio_ref, xr_s, xi_s, hr_s, hi_s, io_s, *, tt, nb, half):
    i = pl.program_id(0)

    @pl.when(i == 0)
    def _():
        hr_s[...] = h0r_ref[...]
        hi_s[...] = h0i_ref[...]

    n_ch = W_C // LANES
    for b in range(nb):
        for s in range(n_ch):
            c0 = b * W_C + s * LANES
            io_s[s, pl.ds(b, tt, stride=nb), :] = u_ref[:, c0:c0 + LANES]
    u = jnp.concatenate([io_s[s] for s in range(n_ch)], axis=-1)
    ub = u.astype(BF16)
    for j in range(N_SLAB):
        us = ub[:, LANES * (j // 2):LANES * (j // 2 + 1)]
        xr_s[:, MXU_DIM * j:MXU_DIM * (j + 1)] = _dot(us, wxr_ref[j])
        xi_s[:, MXU_DIM * j:MXU_DIM * (j + 1)] = _dot(us, wxi_ref[j])

    for c in range(N_STATE // half):
        cols = slice(c * half, (c + 1) * half)
        ar = jnp.broadcast_to(ar_ref[:, cols], (nb, half))
        ai = jnp.broadcast_to(ai_ref[:, cols], (nb, half))

        def step(t, carry, cols=cols, ar=ar, ai=ai):
            hr, hi = carry
            r0 = pl.multiple_of(t * nb, nb)
            nhr = ar * hr - ai * hi + xr_s[pl.ds(r0, nb), cols]
            nhi = ar * hi + ai * hr + xi_s[pl.ds(r0, nb), cols]
            xr_s[pl.ds(r0, nb), cols] = nhr
            xi_s[pl.ds(r0, nb), cols] = nhi
            return nhr, nhi

        hr, hi = lax.fori_loop(0, tt, step, (hr_s[:, cols], hi_s[:, cols]))
        hr_s[:, cols] = hr
        hi_s[:, cols] = hi

    hro_ref[...] = hr_s[...]
    hio_ref[...] = hi_s[...]

    ys = []
    for s in range(W_C // LANES):
        acc = None
        for j in (2 * s, 2 * s + 1):
            hrb = xr_s[:, MXU_DIM * j:MXU_DIM * (j + 1)].astype(BF16)
            hib = xi_s[:, MXU_DIM * j:MXU_DIM * (j + 1)].astype(BF16)
            part = _dot(hrb, wcr_ref[j]) + _dot(hib, wci_ref[j])
            acc = part if acc is None else acc + part
        ys.append(acc)
    y = jnp.concatenate(ys, axis=-1) + d_ref[...] * u
    ge = 0.5 * y * (1.0 + jnp.tanh(math.sqrt(2.0 / math.pi) * (y + 0.044715 * (y * y * y))))
    gl = _dot(ge.astype(BF16), wglu_ref[...])
    oc = gl[:, :W_C] * _sigmoid(gl[:, W_C:])
    for s in range(n_ch):
        io_s[s] = oc[:, s * LANES:(s + 1) * LANES]
    for b in range(nb):
        for s in range(n_ch):
            c0 = b * W_C + s * LANES
            oc_ref[:, c0:c0 + LANES] = io_s[s, pl.ds(b, tt, stride=nb), :].astype(BF16)


def _ssm(u_tm, wxr, wxi, ar, ai, h0r, h0i, wcr, wci, d, wglu, tt):
    L = u_tm.shape[0]
    nb = h0r.shape[0]
    rows = tt * nb
    half = N_STATE // 2
    kern = functools.partial(_ssm_kernel, tt=tt, nb=nb, half=half)
    full = lambda shape: pl.BlockSpec(shape, lambda i: (0,) * len(shape))
    return pl.pallas_call(
        kern,
        grid=(L // tt,),
        in_specs=[pl.BlockSpec((tt, nb * W_C), lambda i: (i, 0)),
                  full((N_SLAB, LANES, MXU_DIM)), full((N_SLAB, LANES, MXU_DIM)),
                  full((1, N_STATE)), full((1, N_STATE)),
                  full((nb, N_STATE)), full((nb, N_STATE)),
                  full((N_SLAB, MXU_DIM, LANES)), full((N_SLAB, MXU_DIM, LANES)),
                  full((1, W_C)), full((W_C, 2 * W_C))],
        out_specs=[pl.BlockSpec((tt, nb * W_C), lambda i: (i, 0)),
                   full((nb, N_STATE)), full((nb, N_STATE))],
        out_shape=[jax.ShapeDtypeStruct((L, nb * W_C), BF16),
                   jax.ShapeDtypeStruct((nb, N_STATE), F32),
                   jax.ShapeDtypeStruct((nb, N_STATE), F32)],
        scratch_shapes=[pltpu.VMEM((rows, N_STATE), F32), pltpu.VMEM((rows, N_STATE), F32),
                        pltpu.VMEM((nb, N_STATE), F32), pltpu.VMEM((nb, N_STATE), F32),
                        pltpu.VMEM((W_C // LANES, rows, LANES), F32)],
        compiler_params=_cparams("arbitrary"),
        name="ssm",
    )(u_tm, wxr, wxi, ar, ai, h0r, h0i, wcr, wci, d, wglu)


def _ssm_weights(w):
    ar, ai = w['ssm_a_re'].astype(F32), w['ssm_a_im'].astype(F32)
    dt = jnp.exp(w['ssm_log_dt'].astype(F32))[:, None]
    mag = jnp.exp(ar * dt)
    abar_r, abar_i = mag * jnp.cos(ai * dt), mag * jnp.sin(ai * dt)
    den = ar * ar + ai * ai
    nr, ni = abar_r - 1.0, abar_i
    z_r = (nr * ar + ni * ai) / den
    z_i = (ni * ar - nr * ai) / den
    br, bi = w['ssm_b_re'].astype(F32), w['ssm_b_im'].astype(F32)
    bb_r = z_r[..., None] * br - z_i[..., None] * bi
    bb_i = z_r[..., None] * bi + z_i[..., None] * br

    gps = MXU_DIM // P_STATE
    eye = jnp.eye(gps, dtype=F32)

    def x_slabs(bb):
        b4 = bb.reshape(N_SLAB, gps, P_STATE, GC)
        blk = jnp.einsum('jgpc,gh->jhcgp', b4, eye).reshape(N_SLAB, gps * GC, MXU_DIM)
        zero = jnp.zeros_like(blk)
        even = jnp.concatenate([blk, zero], axis=1)
        odd = jnp.concatenate([zero, blk], axis=1)
        sel = (jnp.arange(N_SLAB) % 2 == 0)[:, None, None]
        return jnp.where(sel, even, odd).astype(BF16)

    def y_slabs(cc):
        c4 = cc.reshape(N_SLAB, gps, GC, P_STATE)
        blk = jnp.einsum('jgcp,gh->jgphc', c4, eye).reshape(N_SLAB, MXU_DIM, gps * GC)
        zero = jnp.zeros_like(blk)
        even = jnp.concatenate([blk, zero], axis=2)
        odd = jnp.concatenate([zero, blk], axis=2)
        sel = (jnp.arange(N_SLAB) % 2 == 0)[:, None, None]
        return jnp.where(sel, even, odd).astype(BF16)

    return dict(wxr=x_slabs(bb_r), wxi=x_slabs(bb_i),
                ar=abar_r.reshape(1, N_STATE), ai=abar_i.reshape(1, N_STATE),
                wcr=y_slabs(w['ssm_c_re'].astype(F32)), wci=y_slabs(-w['ssm_c_im'].astype(F32)))


def _merge_out_kernel(x_ref, oa_ref, ob_ref, oc_ref, g_ref, wg_ref, wa_ref, wb_ref, wc_ref, wo_ref, o_ref):
    x = x_ref[...]
    h = _rms(x, g_ref[...]).astype(BF16)
    merged = None
    for j, (br_ref, w_ref) in enumerate(((oa_ref, wa_ref), (ob_ref, wb_ref), (oc_ref, wc_ref))):
        gate = _sigmoid(_dot(h, wg_ref[:, j * D_MODEL:(j + 1) * D_MODEL]))
        term = gate * _dot(br_ref[...], w_ref[...])
        merged = term if merged is None else merged + term
    o_ref[...] = x + _dot(merged.astype(BF16), wo_ref[...])


def _merge_out(x, oa, ob, oc_tm, g, wg, wa, wb, wc, wo, tm):
    B, L, D = x.shape
    full = lambda shape: pl.BlockSpec(shape, lambda b, t: (0,) * len(shape))
    br = pl.BlockSpec((None, tm, 512), lambda b, t: (b, t, 0))
    return pl.pallas_call(
        _merge_out_kernel,
        grid=(B, L // tm),
        in_specs=[pl.BlockSpec((None, tm, D), lambda b, t: (b, t, 0)), br, br,
                  pl.BlockSpec((tm, 512), lambda b, t: (t, b)),
                  full((1, D)), full((D, 3 * D)), full((W_A, D)), full((W_B, D)), full((W_C, D)), full((D, D))],
        out_specs=pl.BlockSpec((None, tm, D), lambda b, t: (b, t, 0)),
        out_shape=jax.ShapeDtypeStruct((B, L, D), F32),
        compiler_params=_cparams("parallel", "parallel"),
        name="merge_out",
    )(x, oa, ob, oc_tm, g, wg, wa, wb, wc, wo)


FF_CHUNK = 2 * MXU_DIM


def _swiglu(h, w1_ref, w3_ref, w2_ref):
    n_ff = w1_ref.shape[1]
    acc = None
    for c0 in range(0, n_ff, FF_CHUNK):
        c1 = min(c0 + FF_CHUNK, n_ff)
        a = _dot(h, w1_ref[:, c0:c1])
        b = _dot(h, w3_ref[:, c0:c1])
        part = _dot((a * _sigmoid(a) * b).astype(BF16), w2_ref[c0:c1, :])
        acc = part if acc is None else acc + part
    return acc


def _ffn_kernel(x_ref, g_ref, w1_ref, w3_ref, w2_ref, fg_ref, o_ref, *, final_norm):
    x = x_ref[...]
    h = _rms(x, g_ref[...]).astype(BF16)
    y = x + _swiglu(h, w1_ref, w3_ref, w2_ref)
    o_ref[...] = _rms(y, fg_ref[...]) if final_norm else y


def _ffn(x, g, w1, w3, w2, final_g, final_norm, tm):
    B, L, D = x.shape
    full = lambda shape: pl.BlockSpec(shape, lambda b, t: (0,) * len(shape), pipeline_mode=pl.Buffered(1))
    tok = pl.BlockSpec((None, tm, D), lambda b, t: (b, t, 0))
    return pl.pallas_call(
        functools.partial(_ffn_kernel, final_norm=final_norm),
        grid=(B, L // tm),
        in_specs=[tok, full((1, D)), full((D, D_FF)), full((D, D_FF)), full((D_FF, D)), full((1, D))],
        out_specs=tok,
        out_shape=jax.ShapeDtypeStruct((B, L, D), F32),
        compiler_params=_cparams("parallel", "parallel"),
        name="ffn",
    )(x, g, w1, w3, w2, final_g)


R_I1, R_I2, R_R1, R_R2, R_G1, R_G2 = range(6)


def _moe_route_kernel(x_ref, g_ref, rwh_ref, rwl_ref, tri_ref, route_ref, cnt_ref, cnt_s):
    @pl.when(pl.program_id(0) == 0)
    def _():
        cnt_s[...] = jnp.zeros_like(cnt_s)

    lane = lax.broadcasted_iota(jnp.int32, route_ref.shape, 1)
    hf = _rms(x_ref[...], g_ref[...])
    h_hi = hf.astype(BF16)
    h_lo = (hf - h_hi.astype(F32)).astype(BF16)
    logits = _dot(h_hi, rwh_ref[...]) + (_dot(h_hi, rwl_ref[...]) + _dot(h_lo, rwh_ref[...]))
    logits = jnp.where(lane < N_EXPERTS, logits, -jnp.inf)
    v1 = jnp.max(logits, axis=-1, keepdims=True)
    i1 = jnp.min(jnp.where(logits == v1, lane, LANES), axis=-1, keepdims=True)
    rest = jnp.where(lane == i1, -jnp.inf, logits)
    v2 = jnp.max(rest, axis=-1, keepdims=True)
    i2 = jnp.min(jnp.where(rest == v2, lane, LANES), axis=-1, keepdims=True)
    ex = jnp.exp(v2 - v1)
    g1 = 1.0 / (1.0 + ex)
    g2 = ex / (1.0 + ex)

    oh1 = (lane == i1).astype(F32)
    oh2 = (lane == i2).astype(F32)
    oh = oh1 + oh2
    incl = _dot(tri_ref[...], oh.astype(BF16))
    rank = cnt_s[...] + incl - oh
    r1 = jnp.sum(oh1 * rank, axis=-1, keepdims=True)
    r2 = jnp.sum(oh2 * rank, axis=-1, keepdims=True)
    cnt_s[...] += jnp.sum(oh, axis=0, keepdims=True)

    rec = jnp.zeros(route_ref.shape, F32)
    for ln, val in ((R_I1, i1.astype(F32)), (R_I2, i2.astype(F32)), (R_R1, r1), (R_R2, r2),
                    (R_G1, g1), (R_G2, g2)):
        rec = jnp.where(lane == ln, val, rec)
    route_ref[...] = rec
    cnt_ref[...] = cnt_s[...]


def _moe_route(x, g, rw, tm):
    n, d = x.shape
    tri = (jnp.arange(tm)[:, None] >= jnp.arange(tm)[None, :]).astype(BF16)
    rw_hi = rw.astype(BF16)
    rw_lo = (rw - rw_hi.astype(F32)).astype(BF16)
    full = lambda shape: pl.BlockSpec(shape, lambda i: (0,) * len(shape))
    return pl.pallas_call(
        _moe_route_kernel,
        grid=(n // tm,),
        in_specs=[pl.BlockSpec((tm, d), lambda i: (i, 0)), full((1, d)), full((d, LANES)), full((d, LANES)),
                  full((tm, tm))],
        out_specs=[pl.BlockSpec((tm, LANES), lambda i: (i, 0)), full((1, LANES))],
        out_shape=[jax.ShapeDtypeStruct((n, LANES), F32), jax.ShapeDtypeStruct((1, LANES), F32)],
        scratch_shapes=[pltpu.VMEM((1, LANES), F32)],
        compiler_params=_cparams("arbitrary"),
        name="moe_route",
    )(x, g, rw_hi, rw_lo, tri)


def _row_copies(pos_ref, tm, make_copy):
    def body(k, c):
        base = pl.multiple_of(k * SUBLANES, SUBLANES)
        for j in range(SUBLANES):
            for s in range(2):
                make_copy(s, base + j, pos_ref[s, base + j]).start(priority=s)
        return c

    lax.fori_loop(0, tm // SUBLANES, body, 0)


def _moe_dispatch_kernel(pos_ref, x_ref, xs_in_ref, xs_ref, sem, *, tm):
    del xs_in_ref
    _row_copies(pos_ref, tm, lambda s, i, p: pltpu.make_async_copy(
        x_ref.at[pl.ds(i, 1)], xs_ref.at[pl.ds(p, 1)], sem))
    pltpu.make_async_copy(xs_ref.at[pl.ds(0, 2 * tm)], xs_ref.at[pl.ds(0, 2 * tm)], sem).wait()


def _moe_dispatch(x, pos, n_rows, tm):
    n, d = x.shape
    return pl.pallas_call(
        functools.partial(_moe_dispatch_kernel, tm=tm),
        grid=(n // tm,),
        in_specs=[pl.BlockSpec((None, 2, tm), lambda i: (i, 0, 0), memory_space=pltpu.SMEM),
                  pl.BlockSpec((tm, d), lambda i: (i, 0)),
                  pl.BlockSpec(memory_space=pl.ANY)],
        out_specs=pl.BlockSpec(memory_space=pl.ANY),
        out_shape=jax.ShapeDtypeStruct((n_rows, d), F32),
        input_output_aliases={2: 0},
        scratch_shapes=[pltpu.SemaphoreType.DMA(())],
        compiler_params=_cparams("arbitrary"),
        name="moe_dispatch",
    )(pos, x, jnp.zeros((n_rows, d), F32))


def _moe_expert_kernel(e_ref, rows_ref, xs_ref, g_ref, w1_ref, w3_ref, w2_ref, ys_ref):
    rows = rows_ref[pl.program_id(0)]

    @pl.when(rows > 0)
    def _():
        h = _rms(xs_ref[...], g_ref[...]).astype(BF16)
        ys_ref[...] = _swiglu(h, w1_ref, w3_ref, w2_ref)

    @pl.when(rows == 0)
    def _():
        ys_ref[...] = jnp.zeros_like(ys_ref)


def _moe_experts(xs, g, w1, w3, w2, tile_e, tile_rows, t):
    p, d = xs.shape
    wspec = lambda shape: pl.BlockSpec((None,) + shape, lambda i, e, rows: (e[i], 0, 0))
    row = pl.BlockSpec((t, d), lambda i, e, rows: (i, 0))
    return pl.pallas_call(
        _moe_expert_kernel,
        grid_spec=pltpu.PrefetchScalarGridSpec(
            num_scalar_prefetch=2,
            grid=(p // t,),
            in_specs=[row, pl.BlockSpec((1, d), lambda i, e, rows: (0, 0)),
                      wspec((d, D_FF_E)), wspec((d, D_FF_E)), wspec((D_FF_E, d))],
            out_specs=row),
        out_shape=jax.ShapeDtypeStruct((p, d), F32),
        compiler_params=_cparams("arbitrary"),
        name="moe_experts",
    )(tile_e, tile_rows, xs, g, w1, w3, w2)


def _moe_combine_kernel(pos_ref, x_ref, route_ref, fg_ref, ys_ref, o_ref, ybuf, sem, *, tm, final_norm):
    _row_copies(pos_ref, tm, lambda s, i, p: pltpu.make_async_copy(
        ys_ref.at[pl.ds(p, 1)], ybuf.at[s, pl.ds(i, 1)], sem))
    for s in range(2):
        pltpu.make_async_copy(ys_ref.at[pl.ds(0, tm)], ybuf.at[s], sem).wait()
    rec = route_ref[...]
    lane = lax.broadcasted_iota(jnp.int32, rec.shape, 1)
    g1 = jnp.sum(jnp.where(lane == R_G1, rec, 0.0), axis=-1, keepdims=True)
    g2 = jnp.sum(jnp.where(lane == R_G2, rec, 0.0), axis=-1, keepdims=True)
    y = x_ref[...] + (g1 * ybuf[0] + g2 * ybuf[1])
    o_ref[...] = _rms(y, fg_ref[...]) if final_norm else y


def _moe_combine(x, route, pos, ys, final_g, final_norm, tm):
    n, d = x.shape
    tok = pl.BlockSpec((tm, d), lambda i: (i, 0))
    return pl.pallas_call(
        functools.partial(_moe_combine_kernel, tm=tm, final_norm=final_norm),
        grid=(n // tm,),
        in_specs=[pl.BlockSpec((None, 2, tm), lambda i: (i, 0, 0), memory_space=pltpu.SMEM),
                  tok, pl.BlockSpec((tm, LANES), lambda i: (i, 0)),
                  pl.BlockSpec((1, d), lambda i: (0, 0)), pl.BlockSpec(memory_space=pl.ANY)],
        out_specs=tok,
        out_shape=jax.ShapeDtypeStruct((n, d), F32),
        scratch_shapes=[pltpu.VMEM((2, tm, d), F32), pltpu.SemaphoreType.DMA(())],
        compiler_params=_cparams("arbitrary"),
        name="moe_combine",
    )(pos, x, route, final_g, ys)


def _moe(x, g, rw, w1, w3, w2, final_g, final_norm):
    B, L, D = x.shape
    n = B * L
    tm = min(512, n)
    xf = x.reshape(n, D)
    route, counts = _moe_route(xf, g, rw, tm)

    idx = route[:, :4].astype(jnp.int32)
    cnt = counts[0, :N_EXPERTS].astype(jnp.int32)
    tiles_e = (cnt + tm - 1) // tm
    ends = jnp.cumsum(tiles_e)
    starts = ends - tiles_e
    n_used = ends[-1]
    n_tiles = 2 * n // tm + N_EXPERTS
    pos = jnp.stack([starts[idx[:, R_I1]] * tm + idx[:, R_R1],
                     starts[idx[:, R_I2]] * tm + idx[:, R_R2]])
    pos = pos.reshape(2, n // tm, tm).transpose(1, 0, 2)
    tile = jnp.arange(n_tiles, dtype=jnp.int32)
    tile_e = jnp.searchsorted(ends, jnp.minimum(tile, n_used - 1), side='right').astype(jnp.int32)
    tile_rows = jnp.where(tile < n_used, jnp.clip(cnt[tile_e] - (tile - starts[tile_e]) * tm, 0, tm), 0)

    xs = _moe_dispatch(xf, pos, n_tiles * tm, tm)
    ys = _moe_experts(xs, g, w1, w3, w2, tile_e, tile_rows.astype(jnp.int32), tm)
    out = _moe_combine(xf, route, pos, ys, final_g, final_norm, tm)
    return out.reshape(B, L, D)


def _pick(n, cands):
    for c in cands:
        if n % c == 0:
            return c
    return n


def _layer(x, l, w, cache, final_g, last, dkv, depth):
    B, L, D = x.shape
    assert B == SUBLANES, "the SSM kernel puts the streams of one time step on the 8 sublanes"
    lam_init = 0.8 - 0.6 * math.exp(-0.3 * l)
    tm = _pick(L, (512, 256, 128, 64, 32))
    n_keep = min(BAND, L)
    assert n_keep == tm, "the band cache rows must be exactly the last token tile"
    row = lambda a: a.reshape(1, -1).astype(F32)
    prompt = cache is None

    w_in = w['w_in'].astype(BF16)
    dk, dv = dkv if dkv is not None else (None, None)
    p = _norm_proj(x, row(w['norm1_g']), w_in[:, :N_QKVU], tm, prompt, l, depth, dk, dv)

    lam_vecs = jnp.stack([w['lam_q1'], w['lam_k1'], w['lam_q2'], w['lam_k2']]).astype(F32)
    subln_g = w['subln_g'].astype(F32)
    table = w['rel_bias'].astype(F32)
    if prompt:
        oa = _diff_attn_prompt(p['qa'], p['ka_b'], p['va_b'], lam_vecs, subln_g, lam_init)
        bias_t = _rel_bias(table, N_BAND_TILES, BAND_TILE, BAND_TILE, 0, BAND_TILE, True, True)
        ob = _band_attn_prompt(p['qb'], p['kb_b'], p['vb_b'], bias_t)
        h0r = jnp.zeros((B, N_STATE), F32)
        h0i = jnp.zeros((B, N_STATE), F32)
    else:
        ck_a, cv_a, ck_b, cv_b, h0_re, h0_im = cache
        P = ck_a.shape[1]
        bp = ck_b.shape[1]
        oa = _diff_attn_sample(p['qa'], ck_a.reshape(B, P, W_A).astype(BF16), cv_a.reshape(B, P, W_A).astype(BF16),
                               p['ka_b'], p['va_b'], lam_vecs, subln_g, lam_init)
        bias_c = _rel_bias(table, 1, L, bp, bp, 0, False, False)
        bias_n = _rel_bias(table, 1, L, L, 0, 0, False, False)
        ob = _band_attn_sample(p['qb'], ck_b.reshape(B, bp, W_B).astype(BF16), cv_b.reshape(B, bp, W_B).astype(BF16),
                               p['kb_b'], p['vb_b'], bias_c, bias_n)
        h0r = h0_re.reshape(B, N_STATE).astype(F32)
        h0i = h0_im.reshape(B, N_STATE).astype(F32)

    sw = _ssm_weights(w)
    oc_tm, hr, hi = _ssm(p['u_tm'], sw['wxr'], sw['wxi'], sw['ar'], sw['ai'], h0r, h0i,
                         sw['wcr'], sw['wci'], row(w['ssm_d']), w['w_glu'].astype(BF16), _pick(L, (64, 32)))

    x = _merge_out(x, oa, ob, oc_tm, row(w['norm1_g']), w_in[:, N_QKVU:],
                   w['w_br_a'].astype(BF16), w['w_br_b'].astype(BF16), w['w_br_c'].astype(BF16),
                   w['w_out'].astype(BF16), tm)

    fg = row(final_g)
    if l % 2 == 0:
        x = _ffn(x, row(w['norm2_g']), w['ffn_w1'].astype(BF16), w['ffn_w3'].astype(BF16),
                 w['ffn_w2'].astype(BF16), fg, last, tm)
    else:
        rw = jnp.pad(w['router_w'].astype(F32), ((0, 0), (0, LANES - N_EXPERTS)))
        x = _moe(x, row(w['norm2_g']), rw, w['moe_w1'].astype(BF16),
                 w['moe_w3'].astype(BF16), w['moe_w2'].astype(BF16), fg, last)

    small = (p['kb_last'].reshape(B, n_keep, HB, DH_B), p['vb_last'].reshape(B, n_keep, HB, DH_B),
             hr.reshape(B, N_GROUPS, P_STATE), hi.reshape(B, N_GROUPS, P_STATE))
    return x, (p['dk'], p['dv']), small


_PER_LAYER = ('norm1_g', 'w_in', 'lam_q1', 'lam_k1', 'lam_q2', 'lam_k2', 'subln_g', 'rel_bias',
              'ssm_a_re', 'ssm_a_im', 'ssm_log_dt', 'ssm_b_re', 'ssm_b_im', 'ssm_c_re', 'ssm_c_im',
              'ssm_d', 'w_glu', 'w_br_a', 'w_br_b', 'w_br_c', 'w_out', 'norm2_g')


def kernel(x_prompt, x_sample, cache_dk, cache_dv, cache_bk, cache_bv, state_ssm_re, state_ssm_im,
           norm1_g, w_in, lam_q1, lam_k1, lam_q2, lam_k2, subln_g, rel_bias, ssm_a_re, ssm_a_im,
           ssm_log_dt, ssm_b_re, ssm_b_im, ssm_c_re, ssm_c_im, ssm_d, w_glu, w_br_a, w_br_b, w_br_c,
           w_out, norm2_g, ffn_w1, ffn_w3, ffn_w2, router_w, moe_w1, moe_w3, moe_w2, final_g):
    stacked = dict(norm1_g=norm1_g, w_in=w_in, lam_q1=lam_q1, lam_k1=lam_k1, lam_q2=lam_q2, lam_k2=lam_k2,
                   subln_g=subln_g, rel_bias=rel_bias, ssm_a_re=ssm_a_re, ssm_a_im=ssm_a_im,
                   ssm_log_dt=ssm_log_dt, ssm_b_re=ssm_b_re, ssm_b_im=ssm_b_im, ssm_c_re=ssm_c_re,
                   ssm_c_im=ssm_c_im, ssm_d=ssm_d, w_glu=w_glu, w_br_a=w_br_a, w_br_b=w_br_b,
                   w_br_c=w_br_c, w_out=w_out, norm2_g=norm2_g)
    depth = w_in.shape[0]
    yp, ys = x_prompt, x_sample
    dkv_p, dkv_s = None, None
    small_p, small_s = [], []
    for l in range(depth):
        w = {name: stacked[name][l] for name in _PER_LAYER}
        if l % 2 == 0:
            w['ffn_w1'], w['ffn_w3'], w['ffn_w2'] = ffn_w1[l // 2], ffn_w3[l // 2], ffn_w2[l // 2]
        else:
            w['router_w'], w['moe_w1'] = router_w[l // 2], moe_w1[l // 2]
            w['moe_w3'], w['moe_w2'] = moe_w3[l // 2], moe_w2[l // 2]
        last = l == depth - 1
        yp, dkv_p, sp = _layer(yp, l, w, None, final_g, last, dkv_p, depth)
        ys, dkv_s, ss = _layer(ys, l, w, (cache_dk[l], cache_dv[l], cache_bk[l], cache_bv[l],
                                          state_ssm_re[l], state_ssm_im[l]), final_g, last, dkv_s, depth)
        small_p.append(sp)
        small_s.append(ss)
    outs = [yp, ys]
    for dkv, small in ((dkv_p, small_p), (dkv_s, small_s)):
        outs += list(dkv)
        for j in range(4):
            outs.append(jnp.stack([s[j] for s in small]))
    return tuple(outs)
```

```python
import functools
import math

import jax
import jax.numpy as jnp
from jax import lax
from jax.experimental import pallas as pl
from jax.experimental.pallas import tpu as pltpu

F32 = jnp.float32
BF16 = jnp.bfloat16

D_MODEL = 1024
CHUNK = 64
HA = 4
DH_A = 64
DV_A = 2 * DH_A
W_A = HA * DV_A
HB = 8
DH_B = 64
W_B = HB * DH_B
BAND_CHUNKS = 8
BAND = BAND_CHUNKS * CHUNK
REL_CLIP = 128
N_REL = 2 * REL_CLIP + 1
GC = 16
N_GROUPS = 32
W_C = N_GROUPS * GC
P_STATE = 64
N_STATE = N_GROUPS * P_STATE
D_FF = 11 * D_MODEL // 4
N_EXPERTS = 8
D_FF_E = D_FF // 2
EPS = 1e-6
NEG = -1e30
LOG2E = math.log2(math.e)
N_QKVU = 7 * 512

LANES = 128
SUBLANES = 8
MXU_DIM = 256
VMEM_LIMIT = 56 * 1024 * 1024


def _cparams(*sem):
    return pltpu.CompilerParams(dimension_semantics=sem, vmem_limit_bytes=VMEM_LIMIT)


def _rms(x, g):
    return x * lax.rsqrt(jnp.mean(x * x, axis=-1, keepdims=True) + EPS) * g


def _sigmoid(x):
    return 1.0 / (1.0 + jnp.exp(-x))


def _dot(a, b):
    return jnp.dot(a, b, preferred_element_type=F32)


def _dot_nt(a, b):
    return lax.dot_general(a, b, (((1,), (1,)), ((), ())), preferred_element_type=F32)


def _norm_proj_kernel(*refs, transpose_v, n_alias):
    x_ref, g_ref, w_ref = refs[:3]
    (qa_ref, dk_ref, dv_ref, kab_ref, vab_ref,
     qb_ref, kbl_ref, vbl_ref, kbb_ref, vbb_ref, u_ref) = refs[3 + n_alias:]
    h = _rms(x_ref[...], g_ref[...]).astype(BF16)

    def proj(c):
        return _dot(h, w_ref[:, c * 512:(c + 1) * 512])

    def store_heads(o_ref, y):
        for hd in range(HA):
            o_ref[:, hd, :] = y[:, hd * DV_A:(hd + 1) * DV_A]

    qa_ref[...] = (proj(0) * (DH_A ** -0.5 * LOG2E)).astype(BF16)
    ka = proj(1)
    store_heads(dk_ref, ka)
    kab_ref[...] = ka.astype(BF16)
    va = proj(2)
    store_heads(dv_ref, va)
    vab_ref[...] = (va.T if transpose_v else va).astype(BF16)
    qb_ref[...] = (proj(3) * (DH_B ** -0.5 * LOG2E)).astype(BF16)
    kb = proj(4)
    kbl_ref[...] = kb
    kbb_ref[...] = kb.astype(BF16)
    vb = proj(5)
    vbl_ref[...] = vb
    vbb_ref[...] = (vb.T if transpose_v else vb).astype(BF16)
    u_ref[...] = proj(6)


def _norm_proj(x, g, w_qkvu, tm, transpose_v, layer, depth, dk, dv):
    B, L, D = x.shape
    nt = L // tm
    tok = jax.ShapeDtypeStruct((B, L, 512), BF16)
    tspec = pl.BlockSpec((None, tm, 512), lambda b, t: (b, t, 0))
    vshape, vspec = tok, tspec
    if transpose_v:
        vshape = jax.ShapeDtypeStruct((B, nt, 512, tm), BF16)
        vspec = pl.BlockSpec((None, None, 512, tm), lambda b, t: (b, t, 0, 0))
    cache = jax.ShapeDtypeStruct((depth, B, L, HA, DV_A), F32)
    cspec = pl.BlockSpec((None, None, tm, HA, DV_A), lambda b, t: (layer, b, t, 0, 0))
    last = jax.ShapeDtypeStruct((B, tm, 512), F32)
    lspec = pl.BlockSpec((None, tm, 512), lambda b, t: (b, 0, 0))
    out_shape = [tok, cache, cache, tok, vshape, tok, last, last, tok, vshape,
                 jax.ShapeDtypeStruct((L, B * 512), F32)]
    out_specs = [tspec, cspec, cspec, tspec, vspec, tspec, lspec, lspec, tspec, vspec,
                 pl.BlockSpec((tm, 512), lambda b, t: (t, b))]
    in_specs = [pl.BlockSpec((None, tm, D), lambda b, t: (b, t, 0)),
                pl.BlockSpec((1, D), lambda b, t: (0, 0)),
                pl.BlockSpec((D, N_QKVU), lambda b, t: (0, 0))] + [pl.BlockSpec(memory_space=pl.ANY)] * 2
    if dk is None:
        dk, dv = jnp.zeros(cache.shape, F32), jnp.zeros(cache.shape, F32)
    aliases = {3: 1, 4: 2}
    outs = pl.pallas_call(
        functools.partial(_norm_proj_kernel, transpose_v=transpose_v, n_alias=len(aliases)),
        grid=(B, nt),
        in_specs=in_specs,
        out_specs=out_specs,
        out_shape=out_shape,
        input_output_aliases=aliases,
        compiler_params=_cparams("parallel", "arbitrary"),
        name="norm_proj",
    )(x, g, w_qkvu, dk, dv)
    names = ('qa', 'dk', 'dv', 'ka_b', 'va_b', 'qb', 'kb_last', 'vb_last', 'kb_b', 'vb_b', 'u_tm')
    return dict(zip(names, outs))


def _lam_value(lam_ref, lam_init):
    lv = lam_ref[...]
    e1 = jnp.exp(jnp.sum(lv[0:1, :] * lv[1:2, :], axis=-1, keepdims=True))
    e2 = jnp.exp(jnp.sum(lv[2:3, :] * lv[3:4, :], axis=-1, keepdims=True))
    return e1 - e2 + lam_init


def _subln(o, g, lam_init):
    return _rms(o, g) * (1.0 - lam_init)


def _split_halves(q):
    lane = lax.broadcasted_iota(jnp.int32, q.shape, 1)
    zero = jnp.zeros_like(q)
    return jnp.where(lane < 64, q, zero), jnp.where(lane >= 64, q, zero)


def _diff_attn_kernel(lam_ref, q_ref, k_ref, vt_ref, g_ref, o_ref,
                      m1_s, l1_s, a1_s, m2_s, l2_s, a2_s, sa1_s, sa2_s, sb1_s, sb2_s,
                      xa1_s, xa2_s, xb1_s, xb2_s, *, t, lam_init):
    qi = pl.program_id(2)
    nt = pl.num_programs(2)

    def q_halves(i):
        return _split_halves(q_ref[pl.ds(pl.multiple_of(i * t, t), t), :])

    q_cur = q_halves(qi)
    q_nxt = q_halves(jnp.minimum(qi + 1, nt - 1))

    m1_s[...] = jnp.full_like(m1_s, -jnp.inf)
    m2_s[...] = jnp.full_like(m2_s, -jnp.inf)
    l1_s[...] = jnp.zeros_like(l1_s)
    l2_s[...] = jnp.zeros_like(l2_s)
    a1_s[...] = jnp.zeros_like(a1_s)
    a2_s[...] = jnp.zeros_like(a2_s)

    def update(st, mx, vt, m_s, l_s, a_s):
        m_prev = m_s[...]
        m_new = jnp.maximum(m_prev, mx)
        alpha = jnp.exp2(m_prev - m_new)
        p = jnp.exp2(st - m_new)
        l_s[...] = alpha * l_s[...] + jnp.sum(p, axis=0, keepdims=True)
        a_s[...] = alpha * a_s[...] + _dot(vt, p.astype(BF16))
        m_s[...] = m_new

    buf_a, buf_b = (sa1_s, sa2_s, xa1_s, xa2_s), (sb1_s, sb2_s, xb1_s, xb2_s)

    def produce(kj, buf, qq):
        k = k_ref[pl.ds(pl.multiple_of(kj * t, t), t), :]
        for q, s_ref, x_ref in ((qq[0], buf[0], buf[2]), (qq[1], buf[1], buf[3])):
            st = _dot_nt(k, q)
            s_ref[...] = st
            x_ref[...] = jnp.max(st, axis=0, keepdims=True)

    def consume(kj, buf, masked):
        vt = vt_ref[kj]
        s1, s2 = buf[0][...], buf[1][...]
        if masked:
            kc = lax.broadcasted_iota(jnp.int32, (t, t), 0) // CHUNK
            qc = lax.broadcasted_iota(jnp.int32, (t, t), 1) // CHUNK
            keep = kc <= qc
            s1 = jnp.where(keep, s1, NEG)
            s2 = jnp.where(keep, s2, NEG)
            mx1 = jnp.max(s1, axis=0, keepdims=True)
            mx2 = jnp.max(s2, axis=0, keepdims=True)
        else:
            mx1, mx2 = buf[2][...], buf[3][...]
        update(s1, mx1, vt, m1_s, l1_s, a1_s)
        update(s2, mx2, vt, m2_s, l2_s, a2_s)

    @pl.when(qi == 0)
    def _():
        produce(0, buf_a, q_cur)

    def run(first, second, odd):
        def pair(jj, c):
            kj = 2 * jj
            produce(kj + 1, second, q_cur)
            consume(kj, first, False)
            produce(kj + 2, first, q_cur)
            consume(kj + 1, second, False)
            return c

        lax.fori_loop(0, qi // 2, pair, 0)
        if odd:
            produce(qi, second, q_cur)
            consume(qi - 1, first, False)
            produce(0, first, q_nxt)
            consume(qi, second, True)
        else:
            produce(0, second, q_nxt)
            consume(qi, first, True)

    for r, (first, second) in enumerate(((buf_a, buf_b), (buf_b, buf_a), (buf_b, buf_a), (buf_a, buf_b))):
        pl.when(qi % 4 == r)(functools.partial(run, first, second, r % 2 == 1))

    lam = _lam_value(lam_ref, lam_init)
    ot = a1_s[...] / l1_s[...] - lam * (a2_s[...] / l2_s[...])
    ot = ot * lax.rsqrt(jnp.mean(ot * ot, axis=0, keepdims=True) + EPS) * g_ref[...] * (1.0 - lam_init)
    o_ref[...] = ot.T.astype(BF16)


def _diff_attn_prompt(q, k, vt, lam_vecs, subln_g, lam_init):
    B, L, _ = q.shape
    nt, t = vt.shape[1], vt.shape[3]
    kern = functools.partial(_diff_attn_kernel, t=t, lam_init=lam_init)
    return pl.pallas_call(
        kern,
        grid=(B, HA, nt),
        in_specs=[pl.BlockSpec((4, DH_A), lambda b, h, i: (0, 0)),
                  pl.BlockSpec((None, L, DV_A), lambda b, h, i: (b, 0, h)),
                  pl.BlockSpec((None, L, DV_A), lambda b, h, i: (b, 0, h)),
                  pl.BlockSpec((None, nt, DV_A, t), lambda b, h, i: (b, 0, h, 0)),
                  pl.BlockSpec((None, DV_A, 1), lambda b, h, i: (h, 0, 0))],
        out_specs=pl.BlockSpec((None, t, DV_A), lambda b, h, i: (b, i, h)),
        out_shape=jax.ShapeDtypeStruct((B, L, W_A), BF16),
        scratch_shapes=[pltpu.VMEM((1, t), F32), pltpu.VMEM((1, t), F32), pltpu.VMEM((DV_A, t), F32),
                        pltpu.VMEM((1, t), F32), pltpu.VMEM((1, t), F32), pltpu.VMEM((DV_A, t), F32)]
                       + [pltpu.VMEM((t, t), F32)] * 4 + [pltpu.VMEM((1, t), F32)] * 4,
        compiler_params=_cparams("parallel", "parallel", "arbitrary"),
        name="diff_attn",
    )(lam_vecs, q, k, vt, subln_g.reshape(HA, DV_A, 1))


def _diff_attn_sample_kernel(lam_ref, q_ref, ck_ref, cv_ref, k_ref, v_ref, g_ref, o_ref, *, lam_init):
    lam = _lam_value(lam_ref, lam_init)
    for h in range(HA):
        cols = slice(h * DV_A, (h + 1) * DV_A)
        q1, q2 = _split_halves(q_ref[:, cols])
        ck, cv = ck_ref[:, cols], cv_ref[:, cols]
        k, v = k_ref[:, cols], v_ref[:, cols]

        def one_map(qm):
            sc = _dot_nt(qm, ck)
            sn = _dot_nt(qm, k)
            m = jnp.maximum(jnp.max(sc, axis=-1, keepdims=True), jnp.max(sn, axis=-1, keepdims=True))
            pc = jnp.exp2(sc - m)
            pn = jnp.exp2(sn - m)
            l = jnp.sum(pc, axis=-1, keepdims=True) + jnp.sum(pn, axis=-1, keepdims=True)
            return (_dot(pc.astype(BF16), cv) + _dot(pn.astype(BF16), v)) / l

        o = one_map(q1) - lam * one_map(q2)
        o_ref[:, cols] = _subln(o, g_ref[h], lam_init).astype(BF16)


def _diff_attn_sample(q, ck, cv, k, v, lam_vecs, subln_g, lam_init):
    B, T, _ = q.shape
    P = ck.shape[1]
    kern = functools.partial(_diff_attn_sample_kernel, lam_init=lam_init)
    new = pl.BlockSpec((None, T, W_A), lambda b: (b, 0, 0))
    old = pl.BlockSpec((None, P, W_A), lambda b: (b, 0, 0))
    return pl.pallas_call(
        kern,
        grid=(B,),
        in_specs=[pl.BlockSpec((4, DH_A), lambda b: (0, 0)), new, old, old, new, new,
                  pl.BlockSpec((HA, 1, DV_A), lambda b: (0, 0, 0))],
        out_specs=new,
        out_shape=jax.ShapeDtypeStruct((B, T, W_A), BF16),
        compiler_params=_cparams("parallel"),
        name="diff_attn_sample",
    )(lam_vecs, q, ck, cv, k, v, subln_g.reshape(HA, 1, DV_A))


def _rel_bias_kernel(tab_ref, o_ref, *, rows, cols, off0, off_step, masked, transposed):
    h = pl.program_id(0)
    d = pl.program_id(1)
    off = off0 + d * off_step
    strip = min(rows, CHUNK)
    for r0 in range(0, rows, strip):
        r = r0 + lax.broadcasted_iota(jnp.int32, (strip, cols), 0)
        c = lax.broadcasted_iota(jnp.int32, (strip, cols), 1)
        qi, ki = (c, r) if transposed else (r, c)
        q_lo, q_hi = (0, cols - 1) if transposed else (r0, r0 + strip - 1)
        k_lo, k_hi = (r0, r0 + strip - 1) if transposed else (0, cols - 1)
        idx = jnp.clip(off + qi - ki, -REL_CLIP, REL_CLIP) + REL_CLIP
        lo = jnp.clip(off + q_lo - k_hi, -REL_CLIP, REL_CLIP) + REL_CLIP
        hi = jnp.clip(off + q_hi - k_lo, -REL_CLIP, REL_CLIP) + REL_CLIP

        def body(j, acc, idx=idx):
            return jnp.where(idx == j, tab_ref[j * HB + h], acc)

        bias = lax.fori_loop(lo, hi + 1, body, jnp.zeros((strip, cols), F32)) * LOG2E
        if masked:
            dc = d * (off_step // CHUNK) + qi // CHUNK - ki // CHUNK
            bias = jnp.where((dc >= 0) & (dc <= BAND_CHUNKS), bias, NEG)
        o_ref[r0:r0 + strip, :] = bias


def _rel_bias(table, n_off, rows, cols, off0, off_step, masked, transposed):
    kern = functools.partial(_rel_bias_kernel, rows=rows, cols=cols, off0=off0,
                             off_step=off_step, masked=masked, transposed=transposed)
    return pl.pallas_call(
        kern,
        grid=(HB, n_off),
        in_specs=[pl.BlockSpec(memory_space=pltpu.SMEM)],
        out_specs=pl.BlockSpec((None, None, rows, cols), lambda h, d: (h, d, 0, 0)),
        out_shape=jax.ShapeDtypeStruct((HB, n_off, rows, cols), F32),
        compiler_params=_cparams("parallel", "parallel"),
        name="rel_bias",
    )(table.reshape(N_REL * HB))


def _pair_softmax_out(q, ks, vs, bias_fn):
    lane = lax.broadcasted_iota(jnp.int32, (q.shape[0], LANES), 1)
    outs = []
    for hh, qm in enumerate(_split_halves(q)):
        ss = [_dot_nt(qm, k) + bias_fn(hh, j) for j, k in enumerate(ks)]
        m = functools.reduce(jnp.maximum, [jnp.max(s, axis=-1, keepdims=True) for s in ss])
        ps = [jnp.exp2(s - m) for s in ss]
        l = functools.reduce(jnp.add, [jnp.sum(p, axis=-1, keepdims=True) for p in ps])
        o = functools.reduce(jnp.add, [_dot(p.astype(BF16), v) for p, v in zip(ps, vs)])
        outs.append(o / l)
    return jnp.where(lane < 64, outs[0], outs[1])


BAND_TILE = 4 * CHUNK
N_BAND_TILES = BAND // BAND_TILE + 1


def _band_attn_kernel(q_ref, k0_ref, k1_ref, k2_ref, v0_ref, v1_ref, v2_ref, b_ref, o_ref, sa_s, sb_s):
    i = pl.program_id(1)
    k_refs = (k0_ref, k1_ref, k2_ref)
    vt_refs = (v0_ref, v1_ref, v2_ref)
    row = lax.broadcasted_iota(jnp.int32, (LANES, BAND_TILE), 0)

    def produce(h, buf, guarded):
        cols = slice((h // 2) * LANES, (h // 2 + 1) * LANES)
        qm = _split_halves(q_ref[:, cols])[h % 2]
        for d, k_ref in enumerate(k_refs):
            s = _dot_nt(k_ref[:, cols], qm) + b_ref[h, d]
            if guarded and d > 0:
                s = jnp.where(i >= d, s, NEG)
            buf[d] = s

    def consume(h, buf):
        cols = slice((h // 2) * LANES, (h // 2 + 1) * LANES)
        ss = [buf[d] for d in range(N_BAND_TILES)]
        m = functools.reduce(jnp.maximum, [jnp.max(s, axis=0, keepdims=True) for s in ss])
        ps = [jnp.exp2(s - m) for s in ss]
        l = functools.reduce(jnp.add, [jnp.sum(p, axis=0, keepdims=True) for p in ps])
        ot = functools.reduce(jnp.add, [_dot(r[cols, :], p.astype(BF16)) for r, p in zip(vt_refs, ps)])
        return ot / l

    def all_heads(guarded):
        bufs = (sa_s, sb_s)
        produce(0, bufs[0], guarded)
        prev = None
        for h in range(HB):
            if h + 1 < HB:
                produce(h + 1, bufs[(h + 1) % 2], guarded)
            ot = consume(h, bufs[h % 2])
            if h % 2 == 1:
                cols = slice((h // 2) * LANES, (h // 2 + 1) * LANES)
                o_ref[:, cols] = jnp.where(row < DH_B, prev, ot).T.astype(BF16)
            prev = ot

    @pl.when(i >= N_BAND_TILES - 1)
    def _():
        all_heads(False)

    @pl.when(i < N_BAND_TILES - 1)
    def _():
        all_heads(True)


def _band_attn_prompt(q, k, vt, bias_t):
    B, L, _ = q.shape
    t = BAND_TILE
    per = vt.shape[3] // t
    qspec = pl.BlockSpec((None, t, W_B), lambda b, i: (b, i, 0))
    kspec = lambda d: pl.BlockSpec((None, t, W_B), lambda b, i: (b, jnp.maximum(i - d, 0), 0))
    vspec = lambda d: pl.BlockSpec(
        (None, None, W_B, t),
        lambda b, i: (b, jnp.maximum(i - d, 0) // per, 0, jnp.maximum(i - d, 0) % per))
    return pl.pallas_call(
        _band_attn_kernel,
        grid=(B, L // t),
        in_specs=[qspec, kspec(0), kspec(1), kspec(2), vspec(0), vspec(1), vspec(2),
                  pl.BlockSpec((HB, N_BAND_TILES, t, t), lambda b, i: (0, 0, 0, 0),
                               pipeline_mode=pl.Buffered(1))],
        out_specs=qspec,
        out_shape=jax.ShapeDtypeStruct((B, L, W_B), BF16),
        scratch_shapes=[pltpu.VMEM((N_BAND_TILES, t, t), F32)] * 2,
        compiler_params=_cparams("parallel", "parallel"),
        name="band_attn",
    )(q, k, k, k, vt, vt, vt, bias_t)


def _band_attn_sample_kernel(q_ref, ck_ref, cv_ref, k_ref, v_ref, bc_ref, bn_ref, o_ref):
    for hp in range(HB // 2):
        cols = slice(hp * LANES, (hp + 1) * LANES)
        ks = [ck_ref[:, cols], k_ref[:, cols]]
        vs = [cv_ref[:, cols], v_ref[:, cols]]

        def bias_fn(hh, j, hp=hp):
            return (bc_ref if j == 0 else bn_ref)[2 * hp + hh, 0]

        o_ref[:, cols] = _pair_softmax_out(q_ref[:, cols], ks, vs, bias_fn).astype(BF16)


def _band_attn_sample(q, ck, cv, k, v, bias_c, bias_n):
    B, T, _ = q.shape
    P = ck.shape[1]
    new = pl.BlockSpec((None, T, W_B), lambda b: (b, 0, 0))
    old = pl.BlockSpec((None, P, W_B), lambda b: (b, 0, 0))
    return pl.pallas_call(
        _band_attn_sample_kernel,
        grid=(B,),
        in_specs=[new, old, old, new, new,
                  pl.BlockSpec((HB, 1, T, P), lambda b: (0, 0, 0, 0)),
                  pl.BlockSpec((HB, 1, T, T), lambda b: (0, 0, 0, 0))],
        out_specs=new,
        out_shape=jax.ShapeDtypeStruct((B, T, W_B), BF16),
        compiler_params=_cparams("parallel"),
        name="band_attn_sample",
    )(q, ck, cv, k, v, bias_c, bias_n)


N_SLAB = N_STATE // MXU_DIM


def _ssm_kernel(u_ref, wxr_ref, wxi_ref, ar_ref, ai_ref, h0r_ref, h0i_ref, wcr_ref, wci_ref,
                d_ref, wglu_ref, oc_ref, hro_ref, hio_ref, xr_s, xi_s, hr_s, hi_s, io_s, *, tt, nb, half):
    i = pl.program_id(0)

    @pl.when(i == 0)
    def _():
        hr_s[...] = h0r_ref[...]
        hi_s[...] = h0i_ref[...]

    n_ch = W_C // LANES
    for b in range(nb):
        for s in range(n_ch):
            c0 = b * W_C + s * LANES
            io_s[s, pl.ds(b, tt, stride=nb), :] = u_ref[:, c0:c0 + LANES]
    u = jnp.concatenate([io_s[s] for s in range(n_ch)], axis=-1)
    ub = u.astype(BF16)
    for j in range(N_SLAB):
        us = ub[:, LANES * (j // 2):LANES * (j // 2 + 1)]
        xr_s[:, MXU_DIM * j:MXU_DIM * (j + 1)] = _dot(us, wxr_ref[j])
        xi_s[:, MXU_DIM * j:MXU_DIM * (j + 1)] = _dot(us, wxi_ref[j])

    for c in range(N_STATE // half):
        cols = slice(c * half, (c + 1) * half)
        ar = jnp.broadcast_to(ar_ref[:, cols], (nb, half))
        ai = jnp.broadcast_to(ai_ref[:, cols], (nb, half))

        def step(t, carry, cols=cols, ar=ar, ai=ai):
            hr, hi = carry
            r0 = pl.multiple_of(t * nb, nb)
            nhr = ar * hr - ai * hi + xr_s[pl.ds(r0, nb), cols]
            nhi = ar * hi + ai * hr + xi_s[pl.ds(r0, nb), cols]
            xr_s[pl.ds(r0, nb), cols] = nhr
            xi_s[pl.ds(r0, nb), cols] = nhi
            return nhr, nhi

        hr, hi = lax.fori_loop(0, tt, step, (hr_s[:, cols], hi_s[:, cols]))
        hr_s[:, cols] = hr
        hi_s[:, cols] = hi

    hro_ref[...] = hr_s[...]
    hio_ref[...] = hi_s[...]

    ys = []
    for s in range(W_C // LANES):
        acc = None
        for j in (2 * s, 2 * s + 1):
            hrb = xr_s[:, MXU_DIM * j:MXU_DIM * (j + 1)].astype(BF16)
            hib = xi_s[:, MXU_DIM * j:MXU_DIM * (j + 1)].astype(BF16)
            part = _dot(hrb, wcr_ref[j]) + _dot(hib, wci_ref[j])
            acc = part if acc is None else acc + part
        ys.append(acc)
    y = jnp.concatenate(ys, axis=-1) + d_ref[...] * u
    ge = 0.5 * y * (1.0 + jnp.tanh(math.sqrt(2.0 / math.pi) * (y + 0.044715 * (y * y * y))))
    gl = _dot(ge.astype(BF16), wglu_ref[...])
    oc = gl[:, :W_C] * _sigmoid(gl[:, W_C:])
    for s in range(n_ch):
        io_s[s] = oc[:, s * LANES:(s + 1) * LANES]
    for b in range(nb):
        for s in range(n_ch):
            c0 = b * W_C + s * LANES
            oc_ref[:, c0:c0 + LANES] = io_s[s, pl.ds(b, tt, stride=nb), :].astype(BF16)


def _ssm(u_tm, wxr, wxi, ar, ai, h0r, h0i, wcr, wci, d, wglu, tt):
    L = u_tm.shape[0]
    nb = h0r.shape[0]
    rows = tt * nb
    half = N_STATE // 2
    kern = functools.partial(_ssm_kernel, tt=tt, nb=nb, half=half)
    full = lambda shape: pl.BlockSpec(shape, lambda i: (0,) * len(shape))
    return pl.pallas_call(
        kern,
        grid=(L // tt,),
        in_specs=[pl.BlockSpec((tt, nb * W_C), lambda i: (i, 0)),
                  full((N_SLAB, LANES, MXU_DIM)), full((N_SLAB, LANES, MXU_DIM)),
                  full((1, N_STATE)), full((1, N_STATE)),
                  full((nb, N_STATE)), full((nb, N_STATE)),
                  full((N_SLAB, MXU_DIM, LANES)), full((N_SLAB, MXU_DIM, LANES)),
                  full((1, W_C)), full((W_C, 2 * W_C))],
        out_specs=[pl.BlockSpec((tt, nb * W_C), lambda i: (i, 0)),
                   full((nb, N_STATE)), full((nb, N_STATE))],
        out_shape=[jax.ShapeDtypeStruct((L, nb * W_C), BF16),
                   jax.ShapeDtypeStruct((nb, N_STATE), F32),
                   jax.ShapeDtypeStruct((nb, N_STATE), F32)],
        scratch_shapes=[pltpu.VMEM((rows, N_STATE), F32), pltpu.VMEM((rows, N_STATE), F32),
                        pltpu.VMEM((nb, N_STATE), F32), pltpu.VMEM((nb, N_STATE), F32),
                        pltpu.VMEM((W_C // LANES, rows, LANES), F32)],
        compiler_params=_cparams("arbitrary"),
        name="ssm",
    )(u_tm, wxr, wxi, ar, ai, h0r, h0i, wcr, wci, d, wglu)


def _ssm_weights(w):
    ar, ai = w['ssm_a_re'].astype(F32), w['ssm_a_im'].astype(F32)
    dt = jnp.exp(w['ssm_log_dt'].astype(F32))[:, None]
    mag = jnp.exp(ar * dt)
    abar_r, abar_i = mag * jnp.cos(ai * dt), mag * jnp.sin(ai * dt)
    den = ar * ar + ai * ai
    nr, ni = abar_r - 1.0, abar_i
    z_r = (nr * ar + ni * ai) / den
    z_i = (ni * ar - nr * ai) / den
    br, bi = w['ssm_b_re'].astype(F32), w['ssm_b_im'].astype(F32)
    bb_r = z_r[..., None] * br - z_i[..., None] * bi
    bb_i = z_r[..., None] * bi + z_i[..., None] * br

    gps = MXU_DIM // P_STATE
    eye = jnp.eye(gps, dtype=F32)

    def x_slabs(bb):
        b4 = bb.reshape(N_SLAB, gps, P_STATE, GC)
        blk = jnp.einsum('jgpc,gh->jhcgp', b4, eye).reshape(N_SLAB, gps * GC, MXU_DIM)
        zero = jnp.zeros_like(blk)
        even = jnp.concatenate([blk, zero], axis=1)
        odd = jnp.concatenate([zero, blk], axis=1)
        sel = (jnp.arange(N_SLAB) % 2 == 0)[:, None, None]
        return jnp.where(sel, even, odd).astype(BF16)

    def y_slabs(cc):
        c4 = cc.reshape(N_SLAB, gps, GC, P_STATE)
        blk = jnp.einsum('jgcp,gh->jgphc', c4, eye).reshape(N_SLAB, MXU_DIM, gps * GC)
        zero = jnp.zeros_like(blk)
        even = jnp.concatenate([blk, zero], axis=2)
        odd = jnp.concatenate([zero, blk], axis=2)
        sel = (jnp.arange(N_SLAB) % 2 == 0)[:, None, None]
        return jnp.where(sel, even, odd).astype(BF16)

    return dict(wxr=x_slabs(bb_r), wxi=x_slabs(bb_i),
                ar=abar_r.reshape(1, N_STATE), ai=abar_i.reshape(1, N_STATE),
                wcr=y_slabs(w['ssm_c_re'].astype(F32)), wci=y_slabs(-w['ssm_c_im'].astype(F32)))


def _merge_out_kernel(x_ref, oa_ref, ob_ref, oc_ref, g_ref, wg_ref, wa_ref, wb_ref, wc_ref, wo_ref, o_ref):
    x = x_ref[...]
    h = _rms(x, g_ref[...]).astype(BF16)
    merged = None
    for j, (br_ref, w_ref) in enumerate(((oa_ref, wa_ref), (ob_ref, wb_ref), (oc_ref, wc_ref))):
        gate = _sigmoid(_dot(h, wg_ref[:, j * D_MODEL:(j + 1) * D_MODEL]))
        term = gate * _dot(br_ref[...], w_ref[...])
        merged = term if merged is None else merged + term
    o_ref[...] = x + _dot(merged.astype(BF16), wo_ref[...])


def _merge_out(x, oa, ob, oc_tm, g, wg, wa, wb, wc, wo, tm):
    B, L, D = x.shape
    full = lambda shape: pl.BlockSpec(shape, lambda b, t: (0,) * len(shape))
    br = pl.BlockSpec((None, tm, 512), lambda b, t: (b, t, 0))
    return pl.pallas_call(
        _merge_out_kernel,
        grid=(B, L // tm),
        in_specs=[pl.BlockSpec((None, tm, D), lambda b, t: (b, t, 0)), br, br,
                  pl.BlockSpec((tm, 512), lambda b, t: (t, b)),
                  full((1, D)), full((D, 3 * D)), full((W_A, D)), full((W_B, D)), full((W_C, D)), full((D, D))],
        out_specs=pl.BlockSpec((None, tm, D), lambda b, t: (b, t, 0)),
        out_shape=jax.ShapeDtypeStruct((B, L, D), F32),
        compiler_params=_cparams("parallel", "parallel"),
        name="merge_out",
    )(x, oa, ob, oc_tm, g, wg, wa, wb, wc, wo)


FF_CHUNK = 2 * MXU_DIM


def _swiglu(h, w1_ref, w3_ref, w2_ref):
    n_ff = w1_ref.shape[1]
    acc = None
    for c0 in range(0, n_ff, FF_CHUNK):
        c1 = min(c0 + FF_CHUNK, n_ff)
        a = _dot(h, w1_ref[:, c0:c1])
        b = _dot(h, w3_ref[:, c0:c1])
        part = _dot((a * _sigmoid(a) * b).astype(BF16), w2_ref[c0:c1, :])
        acc = part if acc is None else acc + part
    return acc


def _ffn_kernel(x_ref, g_ref, w1_ref, w3_ref, w2_ref, fg_ref, o_ref, *, final_norm):
    x = x_ref[...]
    h = _rms(x, g_ref[...]).astype(BF16)
    y = x + _swiglu(h, w1_ref, w3_ref, w2_ref)
    o_ref[...] = _rms(y, fg_ref[...]) if final_norm else y


def _ffn(x, g, w1, w3, w2, final_g, final_norm, tm):
    B, L, D = x.shape
    full = lambda shape: pl.BlockSpec(shape, lambda b, t: (0,) * len(shape), pipeline_mode=pl.Buffered(1))
    tok = pl.BlockSpec((None, tm, D), lambda b, t: (b, t, 0))
    return pl.pallas_call(
        functools.partial(_ffn_kernel, final_norm=final_norm),
        grid=(B, L // tm),
        in_specs=[tok, full((1, D)), full((D, D_FF)), full((D, D_FF)), full((D_FF, D)), full((1, D))],
        out_specs=tok,
        out_shape=jax.ShapeDtypeStruct((B, L, D), F32),
        compiler_params=_cparams("parallel", "parallel"),
        name="ffn",
    )(x, g, w1, w3, w2, final_g)


R_I1, R_I2, R_R1, R_R2, R_G1, R_G2 = range(6)


def _moe_route_kernel(x_ref, g_ref, rwh_ref, rwl_ref, tri_ref, route_ref, cnt_ref, cnt_s):
    @pl.when(pl.program_id(0) == 0)
    def _():
        cnt_s[...] = jnp.zeros_like(cnt_s)

    lane = lax.broadcasted_iota(jnp.int32, route_ref.shape, 1)
    hf = _rms(x_ref[...], g_ref[...])
    h_hi = hf.astype(BF16)
    h_lo = (hf - h_hi.astype(F32)).astype(BF16)
    logits = _dot(h_hi, rwh_ref[...]) + (_dot(h_hi, rwl_ref[...]) + _dot(h_lo, rwh_ref[...]))
    logits = jnp.where(lane < N_EXPERTS, logits, -jnp.inf)
    v1 = jnp.max(logits, axis=-1, keepdims=True)
    i1 = jnp.min(jnp.where(logits == v1, lane, LANES), axis=-1, keepdims=True)
    rest = jnp.where(lane == i1, -jnp.inf, logits)
    v2 = jnp.max(rest, axis=-1, keepdims=True)
    i2 = jnp.min(jnp.where(rest == v2, lane, LANES), axis=-1, keepdims=True)
    ex = jnp.exp(v2 - v1)
    g1 = 1.0 / (1.0 + ex)
    g2 = ex / (1.0 + ex)

    oh1 = (lane == i1).astype(F32)
    oh2 = (lane == i2).astype(F32)
    oh = oh1 + oh2
    incl = _dot(tri_ref[...], oh.astype(BF16))
    rank = cnt_s[...] + incl - oh
    r1 = jnp.sum(oh1 * rank, axis=-1, keepdims=True)
    r2 = jnp.sum(oh2 * rank, axis=-1, keepdims=True)
    cnt_s[...] += jnp.sum(oh, axis=0, keepdims=True)

    rec = jnp.zeros(route_ref.shape, F32)
    for ln, val in ((R_I1, i1.astype(F32)), (R_I2, i2.astype(F32)), (R_R1, r1), (R_R2, r2),
                    (R_G1, g1), (R_G2, g2)):
        rec = jnp.where(lane == ln, val, rec)
    route_ref[...] = rec
    cnt_ref[...] = cnt_s[...]


def _moe_route(x, g, rw, tm):
    n, d = x.shape
    tri = (jnp.arange(tm)[:, None] >= jnp.arange(tm)[None, :]).astype(BF16)
    rw_hi = rw.astype(BF16)
    rw_lo = (rw - rw_hi.astype(F32)).astype(BF16)
    full = lambda shape: pl.BlockSpec(shape, lambda i: (0,) * len(shape))
    return pl.pallas_call(
        _moe_route_kernel,
        grid=(n // tm,),
        in_specs=[pl.BlockSpec((tm, d), lambda i: (i, 0)), full((1, d)), full((d, LANES)), full((d, LANES)),
                  full((tm, tm))],
        out_specs=[pl.BlockSpec((tm, LANES), lambda i: (i, 0)), full((1, LANES))],
        out_shape=[jax.ShapeDtypeStruct((n, LANES), F32), jax.ShapeDtypeStruct((1, LANES), F32)],
        scratch_shapes=[pltpu.VMEM((1, LANES), F32)],
        compiler_params=_cparams("arbitrary"),
        name="moe_route",
    )(x, g, rw_hi, rw_lo, tri)


def _row_copies(pos_ref, tm, make_copy):
    def body(k, c):
        for j in range(SUBLANES):
            for s in range(2):
                make_copy(s, k, j, pos_ref[0, s * tm + k * SUBLANES + j]).start(priority=s)
        return c

    lax.fori_loop(0, tm // SUBLANES, body, 0)


N_PAD_SEGS = N_EXPERTS + 1


def _moe_dispatch_kernel(pad_ref, pos_ref, x_ref, xs_ref, zrow_s, sem, zsem, *, tm):
    @pl.when(pl.program_id(0) == 0)
    def _():
        zrow_s[...] = jnp.zeros_like(zrow_s)
        zero_row = zrow_s.at[pl.ds(0, 1)]
        total = 0
        for seg in range(N_PAD_SEGS):
            start, count = pad_ref[0, seg], pad_ref[1, seg]

            def start_one(r, c, start=start):
                pltpu.make_async_copy(zero_row, xs_ref.at[pl.ds(start + r, 1)], zsem).start()
                return c

            lax.fori_loop(0, count, start_one, 0)
            total = total + count

        def wait_one(r, c):
            pltpu.make_async_copy(zero_row, xs_ref.at[pl.ds(0, 1)], zsem).wait()
            return c

        lax.fori_loop(0, total, wait_one, 0)

    _row_copies(pos_ref, tm, lambda s, k, j, p: pltpu.make_async_copy(
        x_ref.at[k, pl.ds(j, 1)], xs_ref.at[pl.ds(p, 1)], sem))
    pltpu.make_async_copy(xs_ref.at[pl.ds(0, 2 * tm)], xs_ref.at[pl.ds(0, 2 * tm)], sem).wait()


def _moe_dispatch(x, pos, pad, n_rows, tm):
    n, d = x.shape
    return pl.pallas_call(
        functools.partial(_moe_dispatch_kernel, tm=tm),
        grid=(n // tm,),
        in_specs=[pl.BlockSpec(memory_space=pltpu.SMEM),
                  pl.BlockSpec((None, 1, 2 * tm), lambda i: (i, 0, 0), memory_space=pltpu.SMEM),
                  pl.BlockSpec((tm // SUBLANES, SUBLANES, d), lambda i: (i, 0, 0))],
        out_specs=pl.BlockSpec(memory_space=pl.ANY),
        out_shape=jax.ShapeDtypeStruct((n_rows, d), F32),
        scratch_shapes=[pltpu.VMEM((SUBLANES, d), F32), pltpu.SemaphoreType.DMA(()),
                        pltpu.SemaphoreType.DMA(())],
        compiler_params=_cparams("arbitrary"),
        name="moe_dispatch",
    )(pad, pos, x.reshape(n // SUBLANES, SUBLANES, d))


def _moe_expert_kernel(e_ref, rows_ref, xs_ref, g_ref, w1_ref, w3_ref, w2_ref, ys_ref):
    rows = rows_ref[pl.program_id(0)]

    @pl.when(rows > 0)
    def _():
        h = _rms(xs_ref[...], g_ref[...]).astype(BF16)
        ys_ref[...] = _swiglu(h, w1_ref, w3_ref, w2_ref)

    @pl.when(rows == 0)
    def _():
        ys_ref[...] = jnp.zeros_like(ys_ref)


def _moe_experts(xs, g, w1, w3, w2, tile_e, tile_rows, t):
    p, d = xs.shape
    wspec = lambda shape: pl.BlockSpec((None,) + shape, lambda i, e, rows: (e[i], 0, 0))
    row = pl.BlockSpec((t, d), lambda i, e, rows: (i, 0))
    return pl.pallas_call(
        _moe_expert_kernel,
        grid_spec=pltpu.PrefetchScalarGridSpec(
            num_scalar_prefetch=2,
            grid=(p // t,),
            in_specs=[row, pl.BlockSpec((1, d), lambda i, e, rows: (0, 0)),
                      wspec((d, D_FF_E)), wspec((d, D_FF_E)), wspec((D_FF_E, d))],
            out_specs=row),
        out_shape=jax.ShapeDtypeStruct((p, d), F32),
        compiler_params=_cparams("arbitrary"),
        name="moe_experts",
    )(tile_e, tile_rows, xs, g, w1, w3, w2)


def _moe_combine_kernel(pos_ref, x_ref, route_ref, fg_ref, ys_ref, o_ref, ybuf, sem, *, tm, final_norm):
    _row_copies(pos_ref, tm, lambda s, k, j, p: pltpu.make_async_copy(
        ys_ref.at[pl.ds(p, 1)], ybuf.at[s, k, pl.ds(j, 1)], sem))
    for s in range(2):
        pltpu.make_async_copy(ybuf.at[s], ybuf.at[s], sem).wait()
    rec = route_ref[...]
    lane = lax.broadcasted_iota(jnp.int32, rec.shape, 1)
    g1 = jnp.sum(jnp.where(lane == R_G1, rec, 0.0), axis=-1, keepdims=True)
    g2 = jnp.sum(jnp.where(lane == R_G2, rec, 0.0), axis=-1, keepdims=True)
    y1 = ybuf[0].reshape(x_ref.shape)
    y2 = ybuf[1].reshape(x_ref.shape)
    y = x_ref[...] + (g1 * y1 + g2 * y2)
    o_ref[...] = _rms(y, fg_ref[...]) if final_norm else y


def _moe_combine(x, route, pos, ys, final_g, final_norm, tm):
    n, d = x.shape
    tok = pl.BlockSpec((tm, d), lambda i: (i, 0))
    return pl.pallas_call(
        functools.partial(_moe_combine_kernel, tm=tm, final_norm=final_norm),
        grid=(n // tm,),
        in_specs=[pl.BlockSpec((None, 1, 2 * tm), lambda i: (i, 0, 0), memory_space=pltpu.SMEM),
                  tok, pl.BlockSpec((tm, LANES), lambda i: (i, 0)),
                  pl.BlockSpec((1, d), lambda i: (0, 0)), pl.BlockSpec(memory_space=pl.ANY)],
        out_specs=tok,
        out_shape=jax.ShapeDtypeStruct((n, d), F32),
        scratch_shapes=[pltpu.VMEM((2, tm // SUBLANES, SUBLANES, d), F32), pltpu.SemaphoreType.DMA(())],
        compiler_params=_cparams("arbitrary"),
        name="moe_combine",
    )(pos, x, route, final_g, ys)


def _moe(x, g, rw, w1, w3, w2, final_g, final_norm):
    B, L, D = x.shape
    n = B * L
    tm = min(512, n)
    xf = x.reshape(n, D)
    route, counts = _moe_route(xf, g, rw, tm)

    idx = route[:, :4].astype(jnp.int32)
    cnt = counts[0, :N_EXPERTS].astype(jnp.int32)
    tiles_e = (cnt + tm - 1) // tm
    ends = jnp.cumsum(tiles_e)
    starts = ends - tiles_e
    n_used = ends[-1]
    n_tiles = 2 * n // tm + N_EXPERTS
    pos = jnp.stack([starts[idx[:, R_I1]] * tm + idx[:, R_R1],
                     starts[idx[:, R_I2]] * tm + idx[:, R_R2]])
    pos = pos.reshape(2, n // tm, tm).transpose(1, 0, 2).reshape(n // tm, 1, 2 * tm)
    tile = jnp.arange(n_tiles, dtype=jnp.int32)
    tile_e = jnp.searchsorted(ends, jnp.minimum(tile, n_used - 1), side='right').astype(jnp.int32)
    tile_rows = jnp.where(tile < n_used, jnp.clip(cnt[tile_e] - (tile - starts[tile_e]) * tm, 0, tm), 0)

    pad_start = jnp.concatenate([starts * tm + cnt, (n_used * tm)[None]])
    pad_count = jnp.concatenate([tiles_e * tm - cnt, ((n_tiles - n_used) * tm)[None]])
    pad = jnp.stack([pad_start, pad_count]).astype(jnp.int32)

    xs = _moe_dispatch(xf, pos, pad, n_tiles * tm, tm)
    ys = _moe_experts(xs, g, w1, w3, w2, tile_e, tile_rows.astype(jnp.int32), tm)
    out = _moe_combine(xf, route, pos, ys, final_g, final_norm, tm)
    return out.reshape(B, L, D)


def _pick(n, cands):
    for c in cands:
        if n % c == 0:
            return c
    return n


def _layer(x, l, w, cache, final_g, last, dkv, depth):
    B, L, D = x.shape
    assert B == SUBLANES, "the SSM kernel puts the streams of one time step on the 8 sublanes"
    lam_init = 0.8 - 0.6 * math.exp(-0.3 * l)
    tm = _pick(L, (512, 256, 128, 64, 32))
    n_keep = min(BAND, L)
    assert n_keep == tm, "the band cache rows must be exactly the last token tile"
    row = lambda a: a.reshape(1, -1).astype(F32)
    prompt = cache is None

    w_in = w['w_in'].astype(BF16)
    dk, dv = dkv if dkv is not None else (None, None)
    p = _norm_proj(x, row(w['norm1_g']), w_in[:, :N_QKVU], tm, prompt, l, depth, dk, dv)

    lam_vecs = jnp.stack([w['lam_q1'], w['lam_k1'], w['lam_q2'], w['lam_k2']]).astype(F32)
    subln_g = w['subln_g'].astype(F32)
    table = w['rel_bias'].astype(F32)
    if prompt:
        oa = _diff_attn_prompt(p['qa'], p['ka_b'], p['va_b'], lam_vecs, subln_g, lam_init)
        bias_t = _rel_bias(table, N_BAND_TILES, BAND_TILE, BAND_TILE, 0, BAND_TILE, True, True)
        ob = _band_attn_prompt(p['qb'], p['kb_b'], p['vb_b'], bias_t)
        h0r = jnp.zeros((B, N_STATE), F32)
        h0i = jnp.zeros((B, N_STATE), F32)
    else:
        ck_a, cv_a, ck_b, cv_b, h0_re, h0_im = cache
        P = ck_a.shape[1]
        bp = ck_b.shape[1]
        oa = _diff_attn_sample(p['qa'], ck_a.reshape(B, P, W_A).astype(BF16), cv_a.reshape(B, P, W_A).astype(BF16),
                               p['ka_b'], p['va_b'], lam_vecs, subln_g, lam_init)
        bias_c = _rel_bias(table, 1, L, bp, bp, 0, False, False)
        bias_n = _rel_bias(table, 1, L, L, 0, 0, False, False)
        ob = _band_attn_sample(p['qb'], ck_b.reshape(B, bp, W_B).astype(BF16), cv_b.reshape(B, bp, W_B).astype(BF16),
                               p['kb_b'], p['vb_b'], bias_c, bias_n)
        h0r = h0_re.reshape(B, N_STATE).astype(F32)
        h0i = h0_im.reshape(B, N_STATE).astype(F32)

    sw = _ssm_weights(w)
    oc_tm, hr, hi = _ssm(p['u_tm'], sw['wxr'], sw['wxi'], sw['ar'], sw['ai'], h0r, h0i,
                         sw['wcr'], sw['wci'], row(w['ssm_d']), w['w_glu'].astype(BF16), _pick(L, (64, 32)))

    x = _merge_out(x, oa, ob, oc_tm, row(w['norm1_g']), w_in[:, N_QKVU:],
                   w['w_br_a'].astype(BF16), w['w_br_b'].astype(BF16), w['w_br_c'].astype(BF16),
                   w['w_out'].astype(BF16), tm)

    fg = row(final_g)
    if l % 2 == 0:
        x = _ffn(x, row(w['norm2_g']), w['ffn_w1'].astype(BF16), w['ffn_w3'].astype(BF16),
                 w['ffn_w2'].astype(BF16), fg, last, tm)
    else:
        rw = jnp.pad(w['router_w'].astype(F32), ((0, 0), (0, LANES - N_EXPERTS)))
        x = _moe(x, row(w['norm2_g']), rw, w['moe_w1'].astype(BF16),
                 w['moe_w3'].astype(BF16), w['moe_w2'].astype(BF16), fg, last)

    small = (p['kb_last'].reshape(B, n_keep, HB, DH_B), p['vb_last'].reshape(B, n_keep, HB, DH_B),
             hr.reshape(B, N_GROUPS, P_STATE), hi.reshape(B, N_GROUPS, P_STATE))
    return x, (p['dk'], p['dv']), small


_PER_LAYER = ('norm1_g', 'w_in', 'lam_q1', 'lam_k1', 'lam_q2', 'lam_k2', 'subln_g', 'rel_bias',
              'ssm_a_re', 'ssm_a_im', 'ssm_log_dt', 'ssm_b_re', 'ssm_b_im', 'ssm_c_re', 'ssm_c_im',
              'ssm_d', 'w_glu', 'w_br_a', 'w_br_b', 'w_br_c', 'w_out', 'norm2_g')


def kernel(x_prompt, x_sample, cache_dk, cache_dv, cache_bk, cache_bv, state_ssm_re, state_ssm_im,
           norm1_g, w_in, lam_q1, lam_k1, lam_q2, lam_k2, subln_g, rel_bias, ssm_a_re, ssm_a_im,
           ssm_log_dt, ssm_b_re, ssm_b_im, ssm_c_re, ssm_c_im, ssm_d, w_glu, w_br_a, w_br_b, w_br_c,
           w_out, norm2_g, ffn_w1, ffn_w3, ffn_w2, router_w, moe_w1, moe_w3, moe_w2, final_g):
    stacked = dict(norm1_g=norm1_g, w_in=w_in, lam_q1=lam_q1, lam_k1=lam_k1, lam_q2=lam_q2, lam_k2=lam_k2,
                   subln_g=subln_g, rel_bias=rel_bias, ssm_a_re=ssm_a_re, ssm_a_im=ssm_a_im,
                   ssm_log_dt=ssm_log_dt, ssm_b_re=ssm_b_re, ssm_b_im=ssm_b_im, ssm_c_re=ssm_c_re,
                   ssm_c_im=ssm_c_im, ssm_d=ssm_d, w_glu=w_glu, w_br_a=w_br_a, w_br_b=w_br_b,
                   w_br_c=w_br_c, w_out=w_out, norm2_g=norm2_g)
    depth = w_in.shape[0]
    yp, ys = x_prompt, x_sample
    dkv_p, dkv_s = None, None
    small_p, small_s = [], []
    for l in range(depth):
        w = {name: stacked[name][l] for name in _PER_LAYER}
        if l % 2 == 0:
            w['ffn_w1'], w['ffn_w3'], w['ffn_w2'] = ffn_w1[l // 2], ffn_w3[l // 2], ffn_w2[l // 2]
        else:
            w['router_w'], w['moe_w1'] = router_w[l // 2], moe_w1[l // 2]
            w['moe_w3'], w['moe_w2'] = moe_w3[l // 2], moe_w2[l // 2]
        last = l == depth - 1
        yp, dkv_p, sp = _layer(yp, l, w, None, final_g, last, dkv_p, depth)
        ys, dkv_s, ss = _layer(ys, l, w, (cache_dk[l], cache_dv[l], cache_bk[l], cache_bv[l],
                                          state_ssm_re[l], state_ssm_im[l]), final_g, last, dkv_s, depth)
        small_p.append(sp)
        small_s.append(ss)
    outs = [yp, ys]
    for dkv, small in ((dkv_p, small_p), (dkv_s, small_s)):
        outs += list(dkv)
        for j in range(4):
            outs.append(jnp.stack([s[j] for s in small]))
    return tuple(outs)
```

```python
import functools
import math

import jax
import jax.numpy as jnp
from jax import lax
from jax.experimental import pallas as pl
from jax.experimental.pallas import tpu as pltpu

F32 = jnp.float32
BF16 = jnp.bfloat16

D_MODEL = 1024
CHUNK = 64
HA = 4
DH_A = 64
DV_A = 2 * DH_A
W_A = HA * DV_A
HB = 8
DH_B = 64
W_B = HB * DH_B
BAND_CHUNKS = 8
BAND = BAND_CHUNKS * CHUNK
REL_CLIP = 128
N_REL = 2 * REL_CLIP + 1
GC = 16
N_GROUPS = 32
W_C = N_GROUPS * GC
P_STATE = 64
N_STATE = N_GROUPS * P_STATE
D_FF = 11 * D_MODEL // 4
N_EXPERTS = 8
D_FF_E = D_FF // 2
EPS = 1e-6
NEG = -1e30
LOG2E = math.log2(math.e)
N_QKVU = 7 * 512

LANES = 128
SUBLANES = 8
MXU_DIM = 256
VMEM_LIMIT = 56 * 1024 * 1024


def _cparams(*sem):
    return pltpu.CompilerParams(dimension_semantics=sem, vmem_limit_bytes=VMEM_LIMIT)


def _rms(x, g):
    return x * lax.rsqrt(jnp.mean(x * x, axis=-1, keepdims=True) + EPS) * g


def _sigmoid(x):
    return 1.0 / (1.0 + jnp.exp(-x))


def _dot(a, b):
    return jnp.dot(a, b, preferred_element_type=F32)


def _dot_nt(a, b):
    return lax.dot_general(a, b, (((1,), (1,)), ((), ())), preferred_element_type=F32)


def _norm_proj_kernel(*refs, transpose_v, n_alias, layer):
    x_ref, g_ref, w_ref = refs[:3]
    (qa_ref, dk_ref, dv_ref, kab_ref, vab_ref,
     qb_ref, kbl_ref, vbl_ref, kbb_ref, vbb_ref, u_ref) = refs[3 + n_alias:]
    if n_alias == 0:
        for slot in range(dk_ref.shape[0]):
            if slot != layer:
                dk_ref[slot] = jnp.zeros(dk_ref.shape[1:], F32)
                dv_ref[slot] = jnp.zeros(dv_ref.shape[1:], F32)
        dk_ref, dv_ref = dk_ref.at[layer], dv_ref.at[layer]
    h = _rms(x_ref[...], g_ref[...]).astype(BF16)

    def proj(c):
        return _dot(h, w_ref[:, c * 512:(c + 1) * 512])

    def store_heads(o_ref, y):
        for hd in range(HA):
            o_ref[:, hd, :] = y[:, hd * DV_A:(hd + 1) * DV_A]

    qa_ref[...] = (proj(0) * (DH_A ** -0.5 * LOG2E)).astype(BF16)
    ka = proj(1)
    store_heads(dk_ref, ka)
    kab_ref[...] = ka.astype(BF16)
    va = proj(2)
    store_heads(dv_ref, va)
    vab_ref[...] = (va.T if transpose_v else va).astype(BF16)
    qb_ref[...] = (proj(3) * (DH_B ** -0.5 * LOG2E)).astype(BF16)
    kb = proj(4)
    kbl_ref[...] = kb
    kbb_ref[...] = kb.astype(BF16)
    vb = proj(5)
    vbl_ref[...] = vb
    vbb_ref[...] = (vb.T if transpose_v else vb).astype(BF16)
    u_ref[...] = proj(6)


def _norm_proj(x, g, w_qkvu, tm, transpose_v, layer, depth, dk, dv):
    B, L, D = x.shape
    nt = L // tm
    tok = jax.ShapeDtypeStruct((B, L, 512), BF16)
    tspec = pl.BlockSpec((None, tm, 512), lambda b, t: (b, t, 0))
    vshape, vspec = tok, tspec
    if transpose_v:
        vshape = jax.ShapeDtypeStruct((B, nt, 512, tm), BF16)
        vspec = pl.BlockSpec((None, None, 512, tm), lambda b, t: (b, t, 0, 0))
    cache = jax.ShapeDtypeStruct((depth, B, L, HA, DV_A), F32)
    cspec = pl.BlockSpec((None, None, tm, HA, DV_A), lambda b, t: (layer, b, t, 0, 0))
    last = jax.ShapeDtypeStruct((B, tm, 512), F32)
    lspec = pl.BlockSpec((None, tm, 512), lambda b, t: (b, 0, 0))
    out_shape = [tok, cache, cache, tok, vshape, tok, last, last, tok, vshape,
                 jax.ShapeDtypeStruct((L, B * 512), F32)]
    out_specs = [tspec, cspec, cspec, tspec, vspec, tspec, lspec, lspec, tspec, vspec,
                 pl.BlockSpec((tm, 512), lambda b, t: (t, b))]
    in_specs = [pl.BlockSpec((None, tm, D), lambda b, t: (b, t, 0)),
                pl.BlockSpec((1, D), lambda b, t: (0, 0)),
                pl.BlockSpec((D, N_QKVU), lambda b, t: (0, 0))]
    args = [x, g, w_qkvu]
    aliases = {}
    if dk is None:
        whole = pl.BlockSpec((depth, None, tm, HA, DV_A), lambda b, t: (0, b, t, 0, 0))
        out_specs[1] = out_specs[2] = whole
    else:
        in_specs += [pl.BlockSpec(memory_space=pl.ANY)] * 2
        args += [dk, dv]
        aliases = {3: 1, 4: 2}
    outs = pl.pallas_call(
        functools.partial(_norm_proj_kernel, transpose_v=transpose_v, n_alias=len(aliases), layer=layer),
        grid=(B, nt),
        in_specs=in_specs,
        out_specs=out_specs,
        out_shape=out_shape,
        input_output_aliases=aliases,
        compiler_params=_cparams("parallel", "arbitrary"),
        name="norm_proj",
    )(*args)
    names = ('qa', 'dk', 'dv', 'ka_b', 'va_b', 'qb', 'kb_last', 'vb_last', 'kb_b', 'vb_b', 'u_tm')
    return dict(zip(names, outs))


def _lam_value(lam_ref, lam_init):
    lv = lam_ref[...]
    e1 = jnp.exp(jnp.sum(lv[0:1, :] * lv[1:2, :], axis=-1, keepdims=True))
    e2 = jnp.exp(jnp.sum(lv[2:3, :] * lv[3:4, :], axis=-1, keepdims=True))
    return e1 - e2 + lam_init


def _subln(o, g, lam_init):
    return _rms(o, g) * (1.0 - lam_init)


def _split_halves(q):
    lane = lax.broadcasted_iota(jnp.int32, q.shape, 1)
    zero = jnp.zeros_like(q)
    return jnp.where(lane < 64, q, zero), jnp.where(lane >= 64, q, zero)


def _diff_attn_kernel(lam_ref, q_ref, k_ref, vt_ref, g_ref, o_ref,
                      m1_s, l1_s, a1_s, m2_s, l2_s, a2_s, sa1_s, sa2_s, sb1_s, sb2_s,
                      xa1_s, xa2_s, xb1_s, xb2_s, *, t, lam_init):
    qi = pl.program_id(2)
    nt = pl.num_programs(2)

    def q_halves(i):
        return _split_halves(q_ref[pl.ds(pl.multiple_of(i * t, t), t), :])

    q_cur = q_halves(qi)
    q_nxt = q_halves(jnp.minimum(qi + 1, nt - 1))

    m1_s[...] = jnp.full_like(m1_s, -jnp.inf)
    m2_s[...] = jnp.full_like(m2_s, -jnp.inf)
    l1_s[...] = jnp.zeros_like(l1_s)
    l2_s[...] = jnp.zeros_like(l2_s)
    a1_s[...] = jnp.zeros_like(a1_s)
    a2_s[...] = jnp.zeros_like(a2_s)

    def update(st, mx, vt, m_s, l_s, a_s):
        m_prev = m_s[...]
        m_new = jnp.maximum(m_prev, mx)
        alpha = jnp.exp2(m_prev - m_new)
        p = jnp.exp2(st - m_new)
        l_s[...] = alpha * l_s[...] + jnp.sum(p, axis=0, keepdims=True)
        a_s[...] = alpha * a_s[...] + _dot(vt, p.astype(BF16))
        m_s[...] = m_new

    buf_a, buf_b = (sa1_s, sa2_s, xa1_s, xa2_s), (sb1_s, sb2_s, xb1_s, xb2_s)

    def produce(kj, buf, qq):
        k = k_ref[pl.ds(pl.multiple_of(kj * t, t), t), :]
        for q, s_ref, x_ref in ((qq[0], buf[0], buf[2]), (qq[1], buf[1], buf[3])):
            st = _dot_nt(k, q)
            s_ref[...] = st
            x_ref[...] = jnp.max(st, axis=0, keepdims=True)

    def consume(kj, buf, masked):
        vt = vt_ref[kj]
        s1, s2 = buf[0][...], buf[1][...]
        if masked:
            kc = lax.broadcasted_iota(jnp.int32, (t, t), 0) // CHUNK
            qc = lax.broadcasted_iota(jnp.int32, (t, t), 1) // CHUNK
            keep = kc <= qc
            s1 = jnp.where(keep, s1, NEG)
            s2 = jnp.where(keep, s2, NEG)
            mx1 = jnp.max(s1, axis=0, keepdims=True)
            mx2 = jnp.max(s2, axis=0, keepdims=True)
        else:
            mx1, mx2 = buf[2][...], buf[3][...]
        update(s1, mx1, vt, m1_s, l1_s, a1_s)
        update(s2, mx2, vt, m2_s, l2_s, a2_s)

    @pl.when(qi == 0)
    def _():
        produce(0, buf_a, q_cur)

    def run(first, second, odd):
        def pair(jj, c):
            kj = 2 * jj
            produce(kj + 1, second, q_cur)
            consume(kj, first, False)
            produce(kj + 2, first, q_cur)
            consume(kj + 1, second, False)
            return c

        lax.fori_loop(0, qi // 2, pair, 0)
        if odd:
            produce(qi, second, q_cur)
            consume(qi - 1, first, False)
            produce(0, first, q_nxt)
            consume(qi, second, True)
        else:
            produce(0, second, q_nxt)
            consume(qi, first, True)

    for r, (first, second) in enumerate(((buf_a, buf_b), (buf_b, buf_a), (buf_b, buf_a), (buf_a, buf_b))):
        pl.when(qi % 4 == r)(functools.partial(run, first, second, r % 2 == 1))

    lam = _lam_value(lam_ref, lam_init)
    ot = a1_s[...] / l1_s[...] - lam * (a2_s[...] / l2_s[...])
    ot = ot * lax.rsqrt(jnp.mean(ot * ot, axis=0, keepdims=True) + EPS) * g_ref[...] * (1.0 - lam_init)
    o_ref[...] = ot.T.astype(BF16)


def _diff_attn_prompt(q, k, vt, lam_vecs, subln_g, lam_init):
    B, L, _ = q.shape
    nt, t = vt.shape[1], vt.shape[3]
    kern = functools.partial(_diff_attn_kernel, t=t, lam_init=lam_init)
    return pl.pallas_call(
        kern,
        grid=(B, HA, nt),
        in_specs=[pl.BlockSpec((4, DH_A), lambda b, h, i: (0, 0)),
                  pl.BlockSpec((None, L, DV_A), lambda b, h, i: (b, 0, h)),
                  pl.BlockSpec((None, L, DV_A), lambda b, h, i: (b, 0, h)),
                  pl.BlockSpec((None, nt, DV_A, t), lambda b, h, i: (b, 0, h, 0)),
                  pl.BlockSpec((None, DV_A, 1), lambda b, h, i: (h, 0, 0))],
        out_specs=pl.BlockSpec((None, t, DV_A), lambda b, h, i: (b, i, h)),
        out_shape=jax.ShapeDtypeStruct((B, L, W_A), BF16),
        scratch_shapes=[pltpu.VMEM((1, t), F32), pltpu.VMEM((1, t), F32), pltpu.VMEM((DV_A, t), F32),
                        pltpu.VMEM((1, t), F32), pltpu.VMEM((1, t), F32), pltpu.VMEM((DV_A, t), F32)]
                       + [pltpu.VMEM((t, t), F32)] * 4 + [pltpu.VMEM((1, t), F32)] * 4,
        compiler_params=_cparams("parallel", "parallel", "arbitrary"),
        name="diff_attn",
    )(lam_vecs, q, k, vt, subln_g.reshape(HA, DV_A, 1))


def _diff_attn_sample_kernel(lam_ref, q_ref, ck_ref, cv_ref, k_ref, v_ref, g_ref, o_ref, *, lam_init):
    lam = _lam_value(lam_ref, lam_init)
    for h in range(HA):
        cols = slice(h * DV_A, (h + 1) * DV_A)
        q1, q2 = _split_halves(q_ref[:, cols])
        ck, cv = ck_ref[:, cols], cv_ref[:, cols]
        k, v = k_ref[:, cols], v_ref[:, cols]

        def one_map(qm):
            sc = _dot_nt(qm, ck)
            sn = _dot_nt(qm, k)
            m = jnp.maximum(jnp.max(sc, axis=-1, keepdims=True), jnp.max(sn, axis=-1, keepdims=True))
            pc = jnp.exp2(sc - m)
            pn = jnp.exp2(sn - m)
            l = jnp.sum(pc, axis=-1, keepdims=True) + jnp.sum(pn, axis=-1, keepdims=True)
            return (_dot(pc.astype(BF16), cv) + _dot(pn.astype(BF16), v)) / l

        o = one_map(q1) - lam * one_map(q2)
        o_ref[:, cols] = _subln(o, g_ref[h], lam_init).astype(BF16)


def _diff_attn_sample(q, ck, cv, k, v, lam_vecs, subln_g, lam_init):
    B, T, _ = q.shape
    P = ck.shape[1]
    kern = functools.partial(_diff_attn_sample_kernel, lam_init=lam_init)
    new = pl.BlockSpec((None, T, W_A), lambda b: (b, 0, 0))
    old = pl.BlockSpec((None, P, W_A), lambda b: (b, 0, 0))
    return pl.pallas_call(
        kern,
        grid=(B,),
        in_specs=[pl.BlockSpec((4, DH_A), lambda b: (0, 0)), new, old, old, new, new,
                  pl.BlockSpec((HA, 1, DV_A), lambda b: (0, 0, 0))],
        out_specs=new,
        out_shape=jax.ShapeDtypeStruct((B, T, W_A), BF16),
        compiler_params=_cparams("parallel"),
        name="diff_attn_sample",
    )(lam_vecs, q, ck, cv, k, v, subln_g.reshape(HA, 1, DV_A))


def _rel_bias_kernel(tab_ref, o_ref, *, rows, cols, off0, off_step, masked, transposed):
    h = pl.program_id(0)
    d = pl.program_id(1)
    off = off0 + d * off_step
    strip = min(rows, CHUNK)
    for r0 in range(0, rows, strip):
        r = r0 + lax.broadcasted_iota(jnp.int32, (strip, cols), 0)
        c = lax.broadcasted_iota(jnp.int32, (strip, cols), 1)
        qi, ki = (c, r) if transposed else (r, c)
        q_lo, q_hi = (0, cols - 1) if transposed else (r0, r0 + strip - 1)
        k_lo, k_hi = (r0, r0 + strip - 1) if transposed else (0, cols - 1)
        idx = jnp.clip(off + qi - ki, -REL_CLIP, REL_CLIP) + REL_CLIP
        lo = jnp.clip(off + q_lo - k_hi, -REL_CLIP, REL_CLIP) + REL_CLIP
        hi = jnp.clip(off + q_hi - k_lo, -REL_CLIP, REL_CLIP) + REL_CLIP

        def body(j, acc, idx=idx):
            return jnp.where(idx == j, tab_ref[j * HB + h], acc)

        bias = lax.fori_loop(lo, hi + 1, body, jnp.zeros((strip, cols), F32)) * LOG2E
        if masked:
            dc = d * (off_step // CHUNK) + qi // CHUNK - ki // CHUNK
            bias = jnp.where((dc >= 0) & (dc <= BAND_CHUNKS), bias, NEG)
        o_ref[r0:r0 + strip, :] = bias


def _rel_bias(table, n_off, rows, cols, off0, off_step, masked, transposed):
    kern = functools.partial(_rel_bias_kernel, rows=rows, cols=cols, off0=off0,
                             off_step=off_step, masked=masked, transposed=transposed)
    return pl.pallas_call(
        kern,
        grid=(HB, n_off),
        in_specs=[pl.BlockSpec(memory_space=pltpu.SMEM)],
        out_specs=pl.BlockSpec((None, None, rows, cols), lambda h, d: (h, d, 0, 0)),
        out_shape=jax.ShapeDtypeStruct((HB, n_off, rows, cols), F32),
        compiler_params=_cparams("parallel", "parallel"),
        name="rel_bias",
    )(table.reshape(N_REL * HB))


def _pair_softmax_out(q, ks, vs, bias_fn):
    lane = lax.broadcasted_iota(jnp.int32, (q.shape[0], LANES), 1)
    outs = []
    for hh, qm in enumerate(_split_halves(q)):
        ss = [_dot_nt(qm, k) + bias_fn(hh, j) for j, k in enumerate(ks)]
        m = functools.reduce(jnp.maximum, [jnp.max(s, axis=-1, keepdims=True) for s in ss])
        ps = [jnp.exp2(s - m) for s in ss]
        l = functools.reduce(jnp.add, [jnp.sum(p, axis=-1, keepdims=True) for p in ps])
        o = functools.reduce(jnp.add, [_dot(p.astype(BF16), v) for p, v in zip(ps, vs)])
        outs.append(o / l)
    return jnp.where(lane < 64, outs[0], outs[1])


BAND_TILE = 4 * CHUNK
N_BAND_TILES = BAND // BAND_TILE + 1


def _band_attn_kernel(q_ref, k0_ref, k1_ref, k2_ref, v0_ref, v1_ref, v2_ref, b_ref, o_ref, sa_s, sb_s):
    i = pl.program_id(1)
    k_refs = (k0_ref, k1_ref, k2_ref)
    vt_refs = (v0_ref, v1_ref, v2_ref)
    row = lax.broadcasted_iota(jnp.int32, (LANES, BAND_TILE), 0)

    def produce(h, buf, guarded):
        cols = slice((h // 2) * LANES, (h // 2 + 1) * LANES)
        qm = _split_halves(q_ref[:, cols])[h % 2]
        for d, k_ref in enumerate(k_refs):
            s = _dot_nt(k_ref[:, cols], qm) + b_ref[h, d]
            if guarded and d > 0:
                s = jnp.where(i >= d, s, NEG)
            buf[d] = s

    def consume(h, buf):
        cols = slice((h // 2) * LANES, (h // 2 + 1) * LANES)
        ss = [buf[d] for d in range(N_BAND_TILES)]
        m = functools.reduce(jnp.maximum, [jnp.max(s, axis=0, keepdims=True) for s in ss])
        ps = [jnp.exp2(s - m) for s in ss]
        l = functools.reduce(jnp.add, [jnp.sum(p, axis=0, keepdims=True) for p in ps])
        ot = functools.reduce(jnp.add, [_dot(r[cols, :], p.astype(BF16)) for r, p in zip(vt_refs, ps)])
        return ot / l

    def all_heads(guarded):
        bufs = (sa_s, sb_s)
        produce(0, bufs[0], guarded)
        prev = None
        for h in range(HB):
            if h + 1 < HB:
                produce(h + 1, bufs[(h + 1) % 2], guarded)
            ot = consume(h, bufs[h % 2])
            if h % 2 == 1:
                cols = slice((h // 2) * LANES, (h // 2 + 1) * LANES)
                o_ref[:, cols] = jnp.where(row < DH_B, prev, ot).T.astype(BF16)
            prev = ot

    @pl.when(i >= N_BAND_TILES - 1)
    def _():
        all_heads(False)

    @pl.when(i < N_BAND_TILES - 1)
    def _():
        all_heads(True)


def _band_attn_prompt(q, k, vt, bias_t):
    B, L, _ = q.shape
    t = BAND_TILE
    per = vt.shape[3] // t
    qspec = pl.BlockSpec((None, t, W_B), lambda b, i: (b, i, 0))
    kspec = lambda d: pl.BlockSpec((None, t, W_B), lambda b, i: (b, jnp.maximum(i - d, 0), 0))
    vspec = lambda d: pl.BlockSpec(
        (None, None, W_B, t),
        lambda b, i: (b, jnp.maximum(i - d, 0) // per, 0, jnp.maximum(i - d, 0) % per))
    return pl.pallas_call(
        _band_attn_kernel,
        grid=(B, L // t),
        in_specs=[qspec, kspec(0), kspec(1), kspec(2), vspec(0), vspec(1), vspec(2),
                  pl.BlockSpec((HB, N_BAND_TILES, t, t), lambda b, i: (0, 0, 0, 0),
                               pipeline_mode=pl.Buffered(1))],
        out_specs=qspec,
        out_shape=jax.ShapeDtypeStruct((B, L, W_B), BF16),
        scratch_shapes=[pltpu.VMEM((N_BAND_TILES, t, t), F32)] * 2,
        compiler_params=_cparams("parallel", "parallel"),
        name="band_attn",
    )(q, k, k, k, vt, vt, vt, bias_t)


def _band_attn_sample_kernel(q_ref, ck_ref, cv_ref, k_ref, v_ref, bc_ref, bn_ref, o_ref):
    for hp in range(HB // 2):
        cols = slice(hp * LANES, (hp + 1) * LANES)
        ks = [ck_ref[:, cols], k_ref[:, cols]]
        vs = [cv_ref[:, cols], v_ref[:, cols]]

        def bias_fn(hh, j, hp=hp):
            return (bc_ref if j == 0 else bn_ref)[2 * hp + hh, 0]

        o_ref[:, cols] = _pair_softmax_out(q_ref[:, cols], ks, vs, bias_fn).astype(BF16)


def _band_attn_sample(q, ck, cv, k, v, bias_c, bias_n):
    B, T, _ = q.shape
    P = ck.shape[1]
    new = pl.BlockSpec((None, T, W_B), lambda b: (b, 0, 0))
    old = pl.BlockSpec((None, P, W_B), lambda b: (b, 0, 0))
    return pl.pallas_call(
        _band_attn_sample_kernel,
        grid=(B,),
        in_specs=[new, old, old, new, new,
                  pl.BlockSpec((HB, 1, T, P), lambda b: (0, 0, 0, 0)),
                  pl.BlockSpec((HB, 1, T, T), lambda b: (0, 0, 0, 0))],
        out_specs=new,
        out_shape=jax.ShapeDtypeStruct((B, T, W_B), BF16),
        compiler_params=_cparams("parallel"),
        name="band_attn_sample",
    )(q, ck, cv, k, v, bias_c, bias_n)


N_SLAB = N_STATE // MXU_DIM


def _ssm_kernel(u_ref, wxr_ref, wxi_ref, ar_ref, ai_ref, h0r_ref, h0i_ref, wcr_ref, wci_ref,
                d_ref, wglu_ref, oc_ref, hro_ref, hio_ref, xr_s, xi_s, hr_s, hi_s, io_s, *, tt, nb, half):
    i = pl.program_id(0)

    @pl.when(i == 0)
    def _():
        hr_s[...] = h0r_ref[...]
        hi_s[...] = h0i_ref[...]

    n_ch = W_C // LANES
    for b in range(nb):
        for s in range(n_ch):
            c0 = b * W_C + s * LANES
            io_s[s, pl.ds(b, tt, stride=nb), :] = u_ref[:, c0:c0 + LANES]
    u = jnp.concatenate([io_s[s] for s in range(n_ch)], axis=-1)
    ub = u.astype(BF16)
    for j in range(N_SLAB):
        us = ub[:, LANES * (j // 2):LANES * (j // 2 + 1)]
        xr_s[:, MXU_DIM * j:MXU_DIM * (j + 1)] = _dot(us, wxr_ref[j])
        xi_s[:, MXU_DIM * j:MXU_DIM * (j + 1)] = _dot(us, wxi_ref[j])

    for c in range(N_STATE // half):
        cols = slice(c * half, (c + 1) * half)
        ar = jnp.broadcast_to(ar_ref[:, cols], (nb, half))
        ai = jnp.broadcast_to(ai_ref[:, cols], (nb, half))

        def step(t, carry, cols=cols, ar=ar, ai=ai):
            hr, hi = carry
            r0 = pl.multiple_of(t * nb, nb)
            nhr = ar * hr - ai * hi + xr_s[pl.ds(r0, nb), cols]
            nhi = ar * hi + ai * hr + xi_s[pl.ds(r0, nb), cols]
            xr_s[pl.ds(r0, nb), cols] = nhr
            xi_s[pl.ds(r0, nb), cols] = nhi
            return nhr, nhi

        hr, hi = lax.fori_loop(0, tt, step, (hr_s[:, cols], hi_s[:, cols]), unroll=4)
        hr_s[:, cols] = hr
        hi_s[:, cols] = hi

    hro_ref[...] = hr_s[...]
    hio_ref[...] = hi_s[...]

    ys = []
    for s in range(W_C // LANES):
        acc = None
        for j in (2 * s, 2 * s + 1):
            hrb = xr_s[:, MXU_DIM * j:MXU_DIM * (j + 1)].astype(BF16)
            hib = xi_s[:, MXU_DIM * j:MXU_DIM * (j + 1)].astype(BF16)
            part = _dot(hrb, wcr_ref[j]) + _dot(hib, wci_ref[j])
            acc = part if acc is None else acc + part
        ys.append(acc)
    y = jnp.concatenate(ys, axis=-1) + d_ref[...] * u
    ge = 0.5 * y * (1.0 + jnp.tanh(math.sqrt(2.0 / math.pi) * (y + 0.044715 * (y * y * y))))
    gl = _dot(ge.astype(BF16), wglu_ref[...])
    oc = gl[:, :W_C] * _sigmoid(gl[:, W_C:])
    for s in range(n_ch):
        io_s[s] = oc[:, s * LANES:(s + 1) * LANES]
    for b in range(nb):
        for s in range(n_ch):
            c0 = b * W_C + s * LANES
            oc_ref[:, c0:c0 + LANES] = io_s[s, pl.ds(b, tt, stride=nb), :].astype(BF16)


def _ssm(u_tm, wxr, wxi, ar, ai, h0r, h0i, wcr, wci, d, wglu, tt):
    L = u_tm.shape[0]
    nb = h0r.shape[0]
    rows = tt * nb
    half = N_STATE // 2
    kern = functools.partial(_ssm_kernel, tt=tt, nb=nb, half=half)
    full = lambda shape: pl.BlockSpec(shape, lambda i: (0,) * len(shape))
    return pl.pallas_call(
        kern,
        grid=(L // tt,),
        in_specs=[pl.BlockSpec((tt, nb * W_C), lambda i: (i, 0)),
                  full((N_SLAB, LANES, MXU_DIM)), full((N_SLAB, LANES, MXU_DIM)),
                  full((1, N_STATE)), full((1, N_STATE)),
                  full((nb, N_STATE)), full((nb, N_STATE)),
                  full((N_SLAB, MXU_DIM, LANES)), full((N_SLAB, MXU_DIM, LANES)),
                  full((1, W_C)), full((W_C, 2 * W_C))],
        out_specs=[pl.BlockSpec((tt, nb * W_C), lambda i: (i, 0)),
                   full((nb, N_STATE)), full((nb, N_STATE))],
        out_shape=[jax.ShapeDtypeStruct((L, nb * W_C), BF16),
                   jax.ShapeDtypeStruct((nb, N_STATE), F32),
                   jax.ShapeDtypeStruct((nb, N_STATE), F32)],
        scratch_shapes=[pltpu.VMEM((rows, N_STATE), F32), pltpu.VMEM((rows, N_STATE), F32),
                        pltpu.VMEM((nb, N_STATE), F32), pltpu.VMEM((nb, N_STATE), F32),
                        pltpu.VMEM((W_C // LANES, rows, LANES), F32)],
        compiler_params=_cparams("arbitrary"),
        name="ssm",
    )(u_tm, wxr, wxi, ar, ai, h0r, h0i, wcr, wci, d, wglu)


def _ssm_weights(w):
    ar, ai = w['ssm_a_re'].astype(F32), w['ssm_a_im'].astype(F32)
    dt = jnp.exp(w['ssm_log_dt'].astype(F32))[:, None]
    mag = jnp.exp(ar * dt)
    abar_r, abar_i = mag * jnp.cos(ai * dt), mag * jnp.sin(ai * dt)
    den = ar * ar + ai * ai
    nr, ni = abar_r - 1.0, abar_i
    z_r = (nr * ar + ni * ai) / den
    z_i = (ni * ar - nr * ai) / den
    br, bi = w['ssm_b_re'].astype(F32), w['ssm_b_im'].astype(F32)
    bb_r = z_r[..., None] * br - z_i[..., None] * bi
    bb_i = z_r[..., None] * bi + z_i[..., None] * br

    gps = MXU_DIM // P_STATE
    eye = jnp.eye(gps, dtype=F32)

    def x_slabs(bb):
        b4 = bb.reshape(N_SLAB, gps, P_STATE, GC)
        blk = jnp.einsum('jgpc,gh->jhcgp', b4, eye).reshape(N_SLAB, gps * GC, MXU_DIM)
        zero = jnp.zeros_like(blk)
        even = jnp.concatenate([blk, zero], axis=1)
        odd = jnp.concatenate([zero, blk], axis=1)
        sel = (jnp.arange(N_SLAB) % 2 == 0)[:, None, None]
        return jnp.where(sel, even, odd).astype(BF16)

    def y_slabs(cc):
        c4 = cc.reshape(N_SLAB, gps, GC, P_STATE)
        blk = jnp.einsum('jgcp,gh->jgphc', c4, eye).reshape(N_SLAB, MXU_DIM, gps * GC)
        zero = jnp.zeros_like(blk)
        even = jnp.concatenate([blk, zero], axis=2)
        odd = jnp.concatenate([zero, blk], axis=2)
        sel = (jnp.arange(N_SLAB) % 2 == 0)[:, None, None]
        return jnp.where(sel, even, odd).astype(BF16)

    return dict(wxr=x_slabs(bb_r), wxi=x_slabs(bb_i),
                ar=abar_r.reshape(1, N_STATE), ai=abar_i.reshape(1, N_STATE),
                wcr=y_slabs(w['ssm_c_re'].astype(F32)), wci=y_slabs(-w['ssm_c_im'].astype(F32)))


def _merge_out_kernel(x_ref, oa_ref, ob_ref, oc_ref, g_ref, wg_ref, wa_ref, wb_ref, wc_ref, wo_ref, o_ref):
    x = x_ref[...]
    h = _rms(x, g_ref[...]).astype(BF16)
    merged = None
    for j, (br_ref, w_ref) in enumerate(((oa_ref, wa_ref), (ob_ref, wb_ref), (oc_ref, wc_ref))):
        gate = _sigmoid(_dot(h, wg_ref[:, j * D_MODEL:(j + 1) * D_MODEL]))
        term = gate * _dot(br_ref[...], w_ref[...])
        merged = term if merged is None else merged + term
    o_ref[...] = x + _dot(merged.astype(BF16), wo_ref[...])


def _merge_out(x, oa, ob, oc_tm, g, wg, wa, wb, wc, wo, tm):
    B, L, D = x.shape
    full = lambda shape: pl.BlockSpec(shape, lambda b, t: (0,) * len(shape))
    br = pl.BlockSpec((None, tm, 512), lambda b, t: (b, t, 0))
    return pl.pallas_call(
        _merge_out_kernel,
        grid=(B, L // tm),
        in_specs=[pl.BlockSpec((None, tm, D), lambda b, t: (b, t, 0)), br, br,
                  pl.BlockSpec((tm, 512), lambda b, t: (t, b)),
                  full((1, D)), full((D, 3 * D)), full((W_A, D)), full((W_B, D)), full((W_C, D)), full((D, D))],
        out_specs=pl.BlockSpec((None, tm, D), lambda b, t: (b, t, 0)),
        out_shape=jax.ShapeDtypeStruct((B, L, D), F32),
        compiler_params=_cparams("parallel", "parallel"),
        name="merge_out",
    )(x, oa, ob, oc_tm, g, wg, wa, wb, wc, wo)


FF_CHUNK = 2 * MXU_DIM


def _swiglu(h, w1_ref, w3_ref, w2_ref):
    n_ff = w1_ref.shape[1]
    acc = None
    for c0 in range(0, n_ff, FF_CHUNK):
        c1 = min(c0 + FF_CHUNK, n_ff)
        a = _dot(h, w1_ref[:, c0:c1])
        b = _dot(h, w3_ref[:, c0:c1])
        part = _dot((a * _sigmoid(a) * b).astype(BF16), w2_ref[c0:c1, :])
        acc = part if acc is None else acc + part
    return acc


def _ffn_kernel(x_ref, g_ref, w1_ref, w3_ref, w2_ref, fg_ref, o_ref, *, final_norm):
    x = x_ref[...]
    h = _rms(x, g_ref[...]).astype(BF16)
    y = x + _swiglu(h, w1_ref, w3_ref, w2_ref)
    o_ref[...] = _rms(y, fg_ref[...]) if final_norm else y


def _ffn(x, g, w1, w3, w2, final_g, final_norm, tm):
    B, L, D = x.shape
    full = lambda shape: pl.BlockSpec(shape, lambda b, t: (0,) * len(shape), pipeline_mode=pl.Buffered(1))
    tok = pl.BlockSpec((None, tm, D), lambda b, t: (b, t, 0))
    return pl.pallas_call(
        functools.partial(_ffn_kernel, final_norm=final_norm),
        grid=(B, L // tm),
        in_specs=[tok, full((1, D)), full((D, D_FF)), full((D, D_FF)), full((D_FF, D)), full((1, D))],
        out_specs=tok,
        out_shape=jax.ShapeDtypeStruct((B, L, D), F32),
        compiler_params=_cparams("parallel", "parallel"),
        name="ffn",
    )(x, g, w1, w3, w2, final_g)


R_I1, R_I2, R_R1, R_R2, R_G1, R_G2 = range(6)


def _moe_route_kernel(x_ref, g_ref, rwh_ref, rwl_ref, tri_ref, route_ref, cnt_ref, cnt_s):
    @pl.when(pl.program_id(0) == 0)
    def _():
        cnt_s[...] = jnp.zeros_like(cnt_s)

    lane = lax.broadcasted_iota(jnp.int32, route_ref.shape, 1)
    hf = _rms(x_ref[...], g_ref[...])
    h_hi = hf.astype(BF16)
    h_lo = (hf - h_hi.astype(F32)).astype(BF16)
    logits = _dot(h_hi, rwh_ref[...]) + (_dot(h_hi, rwl_ref[...]) + _dot(h_lo, rwh_ref[...]))
    logits = jnp.where(lane < N_EXPERTS, logits, -jnp.inf)
    v1 = jnp.max(logits, axis=-1, keepdims=True)
    i1 = jnp.min(jnp.where(logits == v1, lane, LANES), axis=-1, keepdims=True)
    rest = jnp.where(lane == i1, -jnp.inf, logits)
    v2 = jnp.max(rest, axis=-1, keepdims=True)
    i2 = jnp.min(jnp.where(rest == v2, lane, LANES), axis=-1, keepdims=True)
    ex = jnp.exp(v2 - v1)
    g1 = 1.0 / (1.0 + ex)
    g2 = ex / (1.0 + ex)

    oh1 = (lane == i1).astype(F32)
    oh2 = (lane == i2).astype(F32)
    oh = oh1 + oh2
    incl = _dot(tri_ref[...], oh.astype(BF16))
    rank = cnt_s[...] + incl - oh
    r1 = jnp.sum(oh1 * rank, axis=-1, keepdims=True)
    r2 = jnp.sum(oh2 * rank, axis=-1, keepdims=True)
    cnt_s[...] += jnp.sum(oh, axis=0, keepdims=True)

    rec = jnp.zeros(route_ref.shape, F32)
    for ln, val in ((R_I1, i1.astype(F32)), (R_I2, i2.astype(F32)), (R_R1, r1), (R_R2, r2),
                    (R_G1, g1), (R_G2, g2)):
        rec = jnp.where(lane == ln, val, rec)
    route_ref[...] = rec
    cnt_ref[...] = cnt_s[...]


def _moe_route(x, g, rw, tm):
    n, d = x.shape
    tri = (jnp.arange(tm)[:, None] >= jnp.arange(tm)[None, :]).astype(BF16)
    rw_hi = rw.astype(BF16)
    rw_lo = (rw - rw_hi.astype(F32)).astype(BF16)
    full = lambda shape: pl.BlockSpec(shape, lambda i: (0,) * len(shape))
    return pl.pallas_call(
        _moe_route_kernel,
        grid=(n // tm,),
        in_specs=[pl.BlockSpec((tm, d), lambda i: (i, 0)), full((1, d)), full((d, LANES)), full((d, LANES)),
                  full((tm, tm))],
        out_specs=[pl.BlockSpec((tm, LANES), lambda i: (i, 0)), full((1, LANES))],
        out_shape=[jax.ShapeDtypeStruct((n, LANES), F32), jax.ShapeDtypeStruct((1, LANES), F32)],
        scratch_shapes=[pltpu.VMEM((1, LANES), F32)],
        compiler_params=_cparams("arbitrary"),
        name="moe_route",
    )(x, g, rw_hi, rw_lo, tri)


def _row_copies(pos_ref, tm, make_copy):
    def body(k, c):
        for j in range(SUBLANES):
            for s in range(2):
                make_copy(s, k, j, pos_ref[0, s * tm + k * SUBLANES + j]).start(priority=s)
        return c

    lax.fori_loop(0, tm // SUBLANES, body, 0)


N_PAD_SEGS = N_EXPERTS + 1


def _moe_dispatch_kernel(pad_ref, pos_ref, x_ref, xs_ref, zrow_s, sem, zsem, *, tm):
    @pl.when(pl.program_id(0) == 0)
    def _():
        zrow_s[...] = jnp.zeros_like(zrow_s)
        zero_row = zrow_s.at[pl.ds(0, 1)]
        total = 0
        for seg in range(N_PAD_SEGS):
            start, count = pad_ref[0, seg], pad_ref[1, seg]

            def start_one(r, c, start=start):
                pltpu.make_async_copy(zero_row, xs_ref.at[pl.ds(start + r, 1)], zsem).start()
                return c

            lax.fori_loop(0, count, start_one, 0)
            total = total + count

        def wait_one(r, c):
            pltpu.make_async_copy(zero_row, xs_ref.at[pl.ds(0, 1)], zsem).wait()
            return c

        lax.fori_loop(0, total, wait_one, 0)

    _row_copies(pos_ref, tm, lambda s, k, j, p: pltpu.make_async_copy(
        x_ref.at[k, pl.ds(j, 1)], xs_ref.at[pl.ds(p, 1)], sem))
    pltpu.make_async_copy(xs_ref.at[pl.ds(0, 2 * tm)], xs_ref.at[pl.ds(0, 2 * tm)], sem).wait()


def _moe_dispatch(x, pos, pad, n_rows, tm):
    n, d = x.shape
    return pl.pallas_call(
        functools.partial(_moe_dispatch_kernel, tm=tm),
        grid=(n // tm,),
        in_specs=[pl.BlockSpec(memory_space=pltpu.SMEM),
                  pl.BlockSpec((None, 1, 2 * tm), lambda i: (i, 0, 0), memory_space=pltpu.SMEM),
                  pl.BlockSpec((tm // SUBLANES, SUBLANES, d), lambda i: (i, 0, 0))],
        out_specs=pl.BlockSpec(memory_space=pl.ANY),
        out_shape=jax.ShapeDtypeStruct((n_rows, d), F32),
        scratch_shapes=[pltpu.VMEM((SUBLANES, d), F32), pltpu.SemaphoreType.DMA(()),
                        pltpu.SemaphoreType.DMA(())],
        compiler_params=_cparams("arbitrary"),
        name="moe_dispatch",
    )(pad, pos, x.reshape(n // SUBLANES, SUBLANES, d))


def _moe_expert_kernel(e_ref, rows_ref, xs_ref, g_ref, w1_ref, w3_ref, w2_ref, ys_ref):
    rows = rows_ref[pl.program_id(0)]

    @pl.when(rows > 0)
    def _():
        h = _rms(xs_ref[...], g_ref[...]).astype(BF16)
        ys_ref[...] = _swiglu(h, w1_ref, w3_ref, w2_ref)

    @pl.when(rows == 0)
    def _():
        ys_ref[...] = jnp.zeros_like(ys_ref)


def _moe_experts(xs, g, w1, w3, w2, tile_e, tile_rows, t):
    p, d = xs.shape
    wspec = lambda shape: pl.BlockSpec((None,) + shape, lambda i, e, rows: (e[i], 0, 0))
    row = pl.BlockSpec((t, d), lambda i, e, rows: (i, 0))
    return pl.pallas_call(
        _moe_expert_kernel,
        grid_spec=pltpu.PrefetchScalarGridSpec(
            num_scalar_prefetch=2,
            grid=(p // t,),
            in_specs=[row, pl.BlockSpec((1, d), lambda i, e, rows: (0, 0)),
                      wspec((d, D_FF_E)), wspec((d, D_FF_E)), wspec((D_FF_E, d))],
            out_specs=row),
        out_shape=jax.ShapeDtypeStruct((p, d), F32),
        compiler_params=_cparams("arbitrary"),
        name="moe_experts",
    )(tile_e, tile_rows, xs, g, w1, w3, w2)


def _moe_combine_kernel(pos_ref, x_ref, route_ref, fg_ref, ys_ref, o_ref, ybuf, sem, *, tm, final_norm):
    _row_copies(pos_ref, tm, lambda s, k, j, p: pltpu.make_async_copy(
        ys_ref.at[pl.ds(p, 1)], ybuf.at[s, k, pl.ds(j, 1)], sem))
    for s in range(2):
        pltpu.make_async_copy(ybuf.at[s], ybuf.at[s], sem).wait()
    rec = route_ref[...]
    lane = lax.broadcasted_iota(jnp.int32, rec.shape, 1)
    g1 = jnp.sum(jnp.where(lane == R_G1, rec, 0.0), axis=-1, keepdims=True)
    g2 = jnp.sum(jnp.where(lane == R_G2, rec, 0.0), axis=-1, keepdims=True)
    y1 = ybuf[0].reshape(x_ref.shape)
    y2 = ybuf[1].reshape(x_ref.shape)
    y = x_ref[...] + (g1 * y1 + g2 * y2)
    o_ref[...] = _rms(y, fg_ref[...]) if final_norm else y


def _moe_combine(x, route, pos, ys, final_g, final_norm, tm):
    n, d = x.shape
    tok = pl.BlockSpec((tm, d), lambda i: (i, 0))
    return pl.pallas_call(
        functools.partial(_moe_combine_kernel, tm=tm, final_norm=final_norm),
        grid=(n // tm,),
        in_specs=[pl.BlockSpec((None, 1, 2 * tm), lambda i: (i, 0, 0), memory_space=pltpu.SMEM),
                  tok, pl.BlockSpec((tm, LANES), lambda i: (i, 0)),
                  pl.BlockSpec((1, d), lambda i: (0, 0)), pl.BlockSpec(memory_space=pl.ANY)],
        out_specs=tok,
        out_shape=jax.ShapeDtypeStruct((n, d), F32),
        scratch_shapes=[pltpu.VMEM((2, tm // SUBLANES, SUBLANES, d), F32), pltpu.SemaphoreType.DMA(())],
        compiler_params=_cparams("arbitrary"),
        name="moe_combine",
    )(pos, x, route, final_g, ys)


def _moe(x, g, rw, w1, w3, w2, final_g, final_norm):
    B, L, D = x.shape
    n = B * L
    tm = min(512, n)
    xf = x.reshape(n, D)
    route, counts = _moe_route(xf, g, rw, tm)

    idx = route[:, :4].astype(jnp.int32)
    cnt = counts[0, :N_EXPERTS].astype(jnp.int32)
    tiles_e = (cnt + tm - 1) // tm
    ends = jnp.cumsum(tiles_e)
    starts = ends - tiles_e
    n_used = ends[-1]
    n_tiles = 2 * n // tm + N_EXPERTS
    pos = jnp.stack([starts[idx[:, R_I1]] * tm + idx[:, R_R1],
                     starts[idx[:, R_I2]] * tm + idx[:, R_R2]])
    pos = pos.reshape(2, n // tm, tm).transpose(1, 0, 2).reshape(n // tm, 1, 2 * tm)
    tile = jnp.arange(n_tiles, dtype=jnp.int32)
    tile_e = jnp.searchsorted(ends, jnp.minimum(tile, n_used - 1), side='right').astype(jnp.int32)
    tile_rows = jnp.where(tile < n_used, jnp.clip(cnt[tile_e] - (tile - starts[tile_e]) * tm, 0, tm), 0)

    pad_start = jnp.concatenate([starts * tm + cnt, (n_used * tm)[None]])
    pad_count = jnp.concatenate([tiles_e * tm - cnt, ((n_tiles - n_used) * tm)[None]])
    pad = jnp.stack([pad_start, pad_count]).astype(jnp.int32)

    xs = _moe_dispatch(xf, pos, pad, n_tiles * tm, tm)
    ys = _moe_experts(xs, g, w1, w3, w2, tile_e, tile_rows.astype(jnp.int32), tm)
    out = _moe_combine(xf, route, pos, ys, final_g, final_norm, tm)
    return out.reshape(B, L, D)


def _pick(n, cands):
    for c in cands:
        if n % c == 0:
            return c
    return n


def _layer(x, l, w, cache, final_g, last, dkv, depth):
    B, L, D = x.shape
    assert B == SUBLANES, "the SSM kernel puts the streams of one time step on the 8 sublanes"
    lam_init = 0.8 - 0.6 * math.exp(-0.3 * l)
    tm = _pick(L, (512, 256, 128, 64, 32))
    n_keep = min(BAND, L)
    assert n_keep == tm, "the band cache rows must be exactly the last token tile"
    row = lambda a: a.reshape(1, -1).astype(F32)
    prompt = cache is None

    w_in = w['w_in'].astype(BF16)
    dk, dv = dkv if dkv is not None else (None, None)
    p = _norm_proj(x, row(w['norm1_g']), w_in[:, :N_QKVU], tm, prompt, l, depth, dk, dv)

    lam_vecs = jnp.stack([w['lam_q1'], w['lam_k1'], w['lam_q2'], w['lam_k2']]).astype(F32)
    subln_g = w['subln_g'].astype(F32)
    table = w['rel_bias'].astype(F32)
    if prompt:
        oa = _diff_attn_prompt(p['qa'], p['ka_b'], p['va_b'], lam_vecs, subln_g, lam_init)
        bias_t = _rel_bias(table, N_BAND_TILES, BAND_TILE, BAND_TILE, 0, BAND_TILE, True, True)
        ob = _band_attn_prompt(p['qb'], p['kb_b'], p['vb_b'], bias_t)
        h0r = jnp.zeros((B, N_STATE), F32)
        h0i = jnp.zeros((B, N_STATE), F32)
    else:
        ck_a, cv_a, ck_b, cv_b, h0_re, h0_im = cache
        P = ck_a.shape[1]
        bp = ck_b.shape[1]
        oa = _diff_attn_sample(p['qa'], ck_a.reshape(B, P, W_A).astype(BF16), cv_a.reshape(B, P, W_A).astype(BF16),
                               p['ka_b'], p['va_b'], lam_vecs, subln_g, lam_init)
        bias_c = _rel_bias(table, 1, L, bp, bp, 0, False, False)
        bias_n = _rel_bias(table, 1, L, L, 0, 0, False, False)
        ob = _band_attn_sample(p['qb'], ck_b.reshape(B, bp, W_B).astype(BF16), cv_b.reshape(B, bp, W_B).astype(BF16),
                               p['kb_b'], p['vb_b'], bias_c, bias_n)
        h0r = h0_re.reshape(B, N_STATE).astype(F32)
        h0i = h0_im.reshape(B, N_STATE).astype(F32)

    sw = _ssm_weights(w)
    oc_tm, hr, hi = _ssm(p['u_tm'], sw['wxr'], sw['wxi'], sw['ar'], sw['ai'], h0r, h0i,
                         sw['wcr'], sw['wci'], row(w['ssm_d']), w['w_glu'].astype(BF16), _pick(L, (64, 32)))

    x = _merge_out(x, oa, ob, oc_tm, row(w['norm1_g']), w_in[:, N_QKVU:],
                   w['w_br_a'].astype(BF16), w['w_br_b'].astype(BF16), w['w_br_c'].astype(BF16),
                   w['w_out'].astype(BF16), tm)

    fg = row(final_g)
    if l % 2 == 0:
        x = _ffn(x, row(w['norm2_g']), w['ffn_w1'].astype(BF16), w['ffn_w3'].astype(BF16),
                 w['ffn_w2'].astype(BF16), fg, last, tm)
    else:
        rw = jnp.pad(w['router_w'].astype(F32), ((0, 0), (0, LANES - N_EXPERTS)))
        x = _moe(x, row(w['norm2_g']), rw, w['moe_w1'].astype(BF16),
                 w['moe_w3'].astype(BF16), w['moe_w2'].astype(BF16), fg, last)

    small = (p['kb_last'].reshape(B, n_keep, HB, DH_B), p['vb_last'].reshape(B, n_keep, HB, DH_B),
             hr.reshape(B, N_GROUPS, P_STATE), hi.reshape(B, N_GROUPS, P_STATE))
    return x, (p['dk'], p['dv']), small


_PER_LAYER = ('norm1_g', 'w_in', 'lam_q1', 'lam_k1', 'lam_q2', 'lam_k2', 'subln_g', 'rel_bias',
              'ssm_a_re', 'ssm_a_im', 'ssm_log_dt', 'ssm_b_re', 'ssm_b_im', 'ssm_c_re', 'ssm_c_im',
              'ssm_d', 'w_glu', 'w_br_a', 'w_br_b', 'w_br_c', 'w_out', 'norm2_g')


def kernel(x_prompt, x_sample, cache_dk, cache_dv, cache_bk, cache_bv, state_ssm_re, state_ssm_im,
           norm1_g, w_in, lam_q1, lam_k1, lam_q2, lam_k2, subln_g, rel_bias, ssm_a_re, ssm_a_im,
           ssm_log_dt, ssm_b_re, ssm_b_im, ssm_c_re, ssm_c_im, ssm_d, w_glu, w_br_a, w_br_b, w_br_c,
           w_out, norm2_g, ffn_w1, ffn_w3, ffn_w2, router_w, moe_w1, moe_w3, moe_w2, final_g):
    stacked = dict(norm1_g=norm1_g, w_in=w_in, lam_q1=lam_q1, lam_k1=lam_k1, lam_q2=lam_q2, lam_k2=lam_k2,
                   subln_g=subln_g, rel_bias=rel_bias, ssm_a_re=ssm_a_re, ssm_a_im=ssm_a_im,
                   ssm_log_dt=ssm_log_dt, ssm_b_re=ssm_b_re, ssm_b_im=ssm_b_im, ssm_c_re=ssm_c_re,
                   ssm_c_im=ssm_c_im, ssm_d=ssm_d, w_glu=w_glu, w_br_a=w_br_a, w_br_b=w_br_b,
                   w_br_c=w_br_c, w_out=w_out, norm2_g=norm2_g)
    depth = w_in.shape[0]
    yp, ys = x_prompt, x_sample
    dkv_p, dkv_s = None, None
    small_p, small_s = [], []
    for l in range(depth):
        w = {name: stacked[name][l] for name in _PER_LAYER}
        if l % 2 == 0:
            w['ffn_w1'], w['ffn_w3'], w['ffn_w2'] = ffn_w1[l // 2], ffn_w3[l // 2], ffn_w2[l // 2]
        else:
            w['router_w'], w['moe_w1'] = router_w[l // 2], moe_w1[l // 2]
            w['moe_w3'], w['moe_w2'] = moe_w3[l // 2], moe_w2[l // 2]
        last = l == depth - 1
        yp, dkv_p, sp = _layer(yp, l, w, None, final_g, last, dkv_p, depth)
        ys, dkv_s, ss = _layer(ys, l, w, (cache_dk[l], cache_dv[l], cache_bk[l], cache_bv[l],
                                          state_ssm_re[l], state_ssm_im[l]), final_g, last, dkv_s, depth)
        small_p.append(sp)
        small_s.append(ss)
    outs = [yp, ys]
    for dkv, small in ((dkv_p, small_p), (dkv_s, small_s)):
        outs += list(dkv)
        for j in range(4):
            outs.append(jnp.stack([s[j] for s in small]))
    return tuple(outs)
```

```python
import functools
import math

import jax
import jax.numpy as jnp
from jax import lax
from jax.experimental import pallas as pl
from jax.experimental.pallas import tpu as pltpu

F32 = jnp.float32
BF16 = jnp.bfloat16

D_MODEL = 1024
CHUNK = 64
HA = 4
DH_A = 64
DV_A = 2 * DH_A
W_A = HA * DV_A
HB = 8
DH_B = 64
W_B = HB * DH_B
BAND_CHUNKS = 8
BAND = BAND_CHUNKS * CHUNK
REL_CLIP = 128
N_REL = 2 * REL_CLIP + 1
GC = 16
N_GROUPS = 32
W_C = N_GROUPS * GC
P_STATE = 64
N_STATE = N_GROUPS * P_STATE
D_FF = 11 * D_MODEL // 4
N_EXPERTS = 8
D_FF_E = D_FF // 2
EPS = 1e-6
NEG = -1e30
LOG2E = math.log2(math.e)
N_QKVU = 7 * 512

LANES = 128
SUBLANES = 8
MXU_DIM = 256
VMEM_LIMIT = 56 * 1024 * 1024


def _cparams(*sem):
    return pltpu.CompilerParams(dimension_semantics=sem, vmem_limit_bytes=VMEM_LIMIT)


def _rms(x, g):
    return x * lax.rsqrt(jnp.mean(x * x, axis=-1, keepdims=True) + EPS) * g


def _sigmoid(x):
    return 1.0 / (1.0 + jnp.exp(-x))


def _dot(a, b):
    return jnp.dot(a, b, preferred_element_type=F32)


def _dot_nt(a, b):
    return lax.dot_general(a, b, (((1,), (1,)), ((), ())), preferred_element_type=F32)


def _norm_proj_kernel(*refs, transpose_v, n_alias, layer):
    x_ref, g_ref, w_ref = refs[:3]
    (qa_ref, dk_ref, dv_ref, kab_ref, vab_ref,
     qb_ref, kbl_ref, vbl_ref, kbb_ref, vbb_ref, u_ref) = refs[3 + n_alias:]
    if n_alias == 0:
        for slot in range(dk_ref.shape[0]):
            if slot != layer:
                dk_ref[slot] = jnp.zeros(dk_ref.shape[1:], F32)
                dv_ref[slot] = jnp.zeros(dv_ref.shape[1:], F32)
        dk_ref, dv_ref = dk_ref.at[layer], dv_ref.at[layer]
    h = _rms(x_ref[...], g_ref[...]).astype(BF16)

    def proj(c):
        return _dot(h, w_ref[:, c * 512:(c + 1) * 512])

    def store_heads(o_ref, y):
        for hd in range(HA):
            o_ref[:, hd, :] = y[:, hd * DV_A:(hd + 1) * DV_A]

    qa_ref[...] = (proj(0) * (DH_A ** -0.5 * LOG2E)).astype(BF16)
    ka = proj(1)
    store_heads(dk_ref, ka)
    kab_ref[...] = ka.astype(BF16)
    va = proj(2)
    store_heads(dv_ref, va)
    vab_ref[...] = (va.T if transpose_v else va).astype(BF16)
    qb_ref[...] = (proj(3) * (DH_B ** -0.5 * LOG2E)).astype(BF16)
    kb = proj(4)
    kbl_ref[...] = kb
    kbb_ref[...] = kb.astype(BF16)
    vb = proj(5)
    vbl_ref[...] = vb
    vbb_ref[...] = (vb.T if transpose_v else vb).astype(BF16)
    u_ref[...] = proj(6)


def _norm_proj(x, g, w_qkvu, tm, transpose_v, layer, depth, dk, dv):
    B, L, D = x.shape
    nt = L // tm
    tok = jax.ShapeDtypeStruct((B, L, 512), BF16)
    tspec = pl.BlockSpec((None, tm, 512), lambda b, t: (b, t, 0))
    vshape, vspec = tok, tspec
    if transpose_v:
        vshape = jax.ShapeDtypeStruct((B, nt, 512, tm), BF16)
        vspec = pl.BlockSpec((None, None, 512, tm), lambda b, t: (b, t, 0, 0))
    cache = jax.ShapeDtypeStruct((depth, B, L, HA, DV_A), F32)
    cspec = pl.BlockSpec((None, None, tm, HA, DV_A), lambda b, t: (layer, b, t, 0, 0))
    last = jax.ShapeDtypeStruct((B, tm, 512), F32)
    lspec = pl.BlockSpec((None, tm, 512), lambda b, t: (b, 0, 0))
    out_shape = [tok, cache, cache, tok, vshape, tok, last, last, tok, vshape,
                 jax.ShapeDtypeStruct((L, B * 512), F32)]
    out_specs = [tspec, cspec, cspec, tspec, vspec, tspec, lspec, lspec, tspec, vspec,
                 pl.BlockSpec((tm, 512), lambda b, t: (t, b))]
    in_specs = [pl.BlockSpec((None, tm, D), lambda b, t: (b, t, 0)),
                pl.BlockSpec((1, D), lambda b, t: (0, 0)),
                pl.BlockSpec((D, N_QKVU), lambda b, t: (0, 0))]
    args = [x, g, w_qkvu]
    aliases = {}
    if dk is None:
        whole = pl.BlockSpec((depth, None, tm, HA, DV_A), lambda b, t: (0, b, t, 0, 0))
        out_specs[1] = out_specs[2] = whole
    else:
        in_specs += [pl.BlockSpec(memory_space=pl.ANY)] * 2
        args += [dk, dv]
        aliases = {3: 1, 4: 2}
    outs = pl.pallas_call(
        functools.partial(_norm_proj_kernel, transpose_v=transpose_v, n_alias=len(aliases), layer=layer),
        grid=(B, nt),
        in_specs=in_specs,
        out_specs=out_specs,
        out_shape=out_shape,
        input_output_aliases=aliases,
        compiler_params=_cparams("parallel", "arbitrary"),
        name="norm_proj",
    )(*args)
    names = ('qa', 'dk', 'dv', 'ka_b', 'va_b', 'qb', 'kb_last', 'vb_last', 'kb_b', 'vb_b', 'u_tm')
    return dict(zip(names, outs))


def _lam_value(lam_ref, lam_init):
    lv = lam_ref[...]
    e1 = jnp.exp(jnp.sum(lv[0:1, :] * lv[1:2, :], axis=-1, keepdims=True))
    e2 = jnp.exp(jnp.sum(lv[2:3, :] * lv[3:4, :], axis=-1, keepdims=True))
    return e1 - e2 + lam_init


def _subln(o, g, lam_init):
    return _rms(o, g) * (1.0 - lam_init)


def _split_halves(q):
    lane = lax.broadcasted_iota(jnp.int32, q.shape, 1)
    zero = jnp.zeros_like(q)
    return jnp.where(lane < 64, q, zero), jnp.where(lane >= 64, q, zero)


def _diff_attn_kernel(lam_ref, q_ref, k_ref, vt_ref, g_ref, o_ref,
                      m1_s, l1_s, a1_s, m2_s, l2_s, a2_s, sa1_s, sa2_s, sb1_s, sb2_s,
                      xa1_s, xa2_s, xb1_s, xb2_s, *, t, lam_init):
    qi = pl.program_id(2)
    nt = pl.num_programs(2)

    def q_halves(i):
        return _split_halves(q_ref[pl.ds(pl.multiple_of(i * t, t), t), :])

    q_cur = q_halves(qi)
    q_nxt = q_halves(jnp.minimum(qi + 1, nt - 1))

    m1_s[...] = jnp.full_like(m1_s, -jnp.inf)
    m2_s[...] = jnp.full_like(m2_s, -jnp.inf)
    l1_s[...] = jnp.zeros_like(l1_s)
    l2_s[...] = jnp.zeros_like(l2_s)
    a1_s[...] = jnp.zeros_like(a1_s)
    a2_s[...] = jnp.zeros_like(a2_s)

    def update(st, mx, vt, m_s, l_s, a_s):
        m_prev = m_s[...]
        m_new = jnp.maximum(m_prev, mx)
        alpha = jnp.exp2(m_prev - m_new)
        p = jnp.exp2(st - m_new)
        l_s[...] = alpha * l_s[...] + jnp.sum(p, axis=0, keepdims=True)
        a_s[...] = alpha * a_s[...] + _dot(vt, p.astype(BF16))
        m_s[...] = m_new

    buf_a, buf_b = (sa1_s, sa2_s, xa1_s, xa2_s), (sb1_s, sb2_s, xb1_s, xb2_s)

    def produce(kj, buf, qq):
        k = k_ref[pl.ds(pl.multiple_of(kj * t, t), t), :]
        for q, s_ref, x_ref in ((qq[0], buf[0], buf[2]), (qq[1], buf[1], buf[3])):
            st = _dot_nt(k, q)
            s_ref[...] = st
            x_ref[...] = jnp.max(st, axis=0, keepdims=True)

    def consume(kj, buf, masked):
        vt = vt_ref[kj]
        s1, s2 = buf[0][...], buf[1][...]
        if masked:
            kc = lax.broadcasted_iota(jnp.int32, (t, t), 0) // CHUNK
            qc = lax.broadcasted_iota(jnp.int32, (t, t), 1) // CHUNK
            keep = kc <= qc
            s1 = jnp.where(keep, s1, NEG)
            s2 = jnp.where(keep, s2, NEG)
            mx1 = jnp.max(s1, axis=0, keepdims=True)
            mx2 = jnp.max(s2, axis=0, keepdims=True)
        else:
            mx1, mx2 = buf[2][...], buf[3][...]
        update(s1, mx1, vt, m1_s, l1_s, a1_s)
        update(s2, mx2, vt, m2_s, l2_s, a2_s)

    @pl.when(qi == 0)
    def _():
        produce(0, buf_a, q_cur)

    def run(first, second, odd):
        def pair(jj, c):
            kj = 2 * jj
            produce(kj + 1, second, q_cur)
            consume(kj, first, False)
            produce(kj + 2, first, q_cur)
            consume(kj + 1, second, False)
            return c

        lax.fori_loop(0, qi // 2, pair, 0)
        if odd:
            produce(qi, second, q_cur)
            consume(qi - 1, first, False)
            produce(0, first, q_nxt)
            consume(qi, second, True)
        else:
            produce(0, second, q_nxt)
            consume(qi, first, True)

    for r, (first, second) in enumerate(((buf_a, buf_b), (buf_b, buf_a), (buf_b, buf_a), (buf_a, buf_b))):
        pl.when(qi % 4 == r)(functools.partial(run, first, second, r % 2 == 1))

    lam = _lam_value(lam_ref, lam_init)
    ot = a1_s[...] / l1_s[...] - lam * (a2_s[...] / l2_s[...])
    ot = ot * lax.rsqrt(jnp.mean(ot * ot, axis=0, keepdims=True) + EPS) * g_ref[...] * (1.0 - lam_init)
    o_ref[...] = ot.T.astype(BF16)


def _diff_attn_prompt(q, k, vt, lam_vecs, subln_g, lam_init):
    B, L, _ = q.shape
    nt, t = vt.shape[1], vt.shape[3]
    kern = functools.partial(_diff_attn_kernel, t=t, lam_init=lam_init)
    return pl.pallas_call(
        kern,
        grid=(B, HA, nt),
        in_specs=[pl.BlockSpec((4, DH_A), lambda b, h, i: (0, 0)),
                  pl.BlockSpec((None, L, DV_A), lambda b, h, i: (b, 0, h)),
                  pl.BlockSpec((None, L, DV_A), lambda b, h, i: (b, 0, h)),
                  pl.BlockSpec((None, nt, DV_A, t), lambda b, h, i: (b, 0, h, 0)),
                  pl.BlockSpec((None, DV_A, 1), lambda b, h, i: (h, 0, 0))],
        out_specs=pl.BlockSpec((None, t, DV_A), lambda b, h, i: (b, i, h)),
        out_shape=jax.ShapeDtypeStruct((B, L, W_A), BF16),
        scratch_shapes=[pltpu.VMEM((1, t), F32), pltpu.VMEM((1, t), F32), pltpu.VMEM((DV_A, t), F32),
                        pltpu.VMEM((1, t), F32), pltpu.VMEM((1, t), F32), pltpu.VMEM((DV_A, t), F32)]
                       + [pltpu.VMEM((t, t), F32)] * 4 + [pltpu.VMEM((1, t), F32)] * 4,
        compiler_params=_cparams("parallel", "parallel", "arbitrary"),
        name="diff_attn",
    )(lam_vecs, q, k, vt, subln_g.reshape(HA, DV_A, 1))


def _diff_attn_sample_kernel(lam_ref, q_ref, ck_ref, cv_ref, k_ref, v_ref, g_ref, o_ref, *, lam_init):
    lam = _lam_value(lam_ref, lam_init)
    for h in range(HA):
        cols = slice(h * DV_A, (h + 1) * DV_A)
        q1, q2 = _split_halves(q_ref[:, cols])
        ck, cv = ck_ref[:, cols], cv_ref[:, cols]
        k, v = k_ref[:, cols], v_ref[:, cols]

        def one_map(qm):
            sc = _dot_nt(qm, ck)
            sn = _dot_nt(qm, k)
            m = jnp.maximum(jnp.max(sc, axis=-1, keepdims=True), jnp.max(sn, axis=-1, keepdims=True))
            pc = jnp.exp2(sc - m)
            pn = jnp.exp2(sn - m)
            l = jnp.sum(pc, axis=-1, keepdims=True) + jnp.sum(pn, axis=-1, keepdims=True)
            return (_dot(pc.astype(BF16), cv) + _dot(pn.astype(BF16), v)) / l

        o = one_map(q1) - lam * one_map(q2)
        o_ref[:, cols] = _subln(o, g_ref[h], lam_init).astype(BF16)


def _diff_attn_sample(q, ck, cv, k, v, lam_vecs, subln_g, lam_init):
    B, T, _ = q.shape
    P = ck.shape[1]
    kern = functools.partial(_diff_attn_sample_kernel, lam_init=lam_init)
    new = pl.BlockSpec((None, T, W_A), lambda b: (b, 0, 0))
    old = pl.BlockSpec((None, P, W_A), lambda b: (b, 0, 0))
    return pl.pallas_call(
        kern,
        grid=(B,),
        in_specs=[pl.BlockSpec((4, DH_A), lambda b: (0, 0)), new, old, old, new, new,
                  pl.BlockSpec((HA, 1, DV_A), lambda b: (0, 0, 0))],
        out_specs=new,
        out_shape=jax.ShapeDtypeStruct((B, T, W_A), BF16),
        compiler_params=_cparams("parallel"),
        name="diff_attn_sample",
    )(lam_vecs, q, ck, cv, k, v, subln_g.reshape(HA, 1, DV_A))


def _rel_bias_kernel(tab_ref, o_ref, *, rows, cols, off0, off_step, masked, transposed):
    h = pl.program_id(0)
    d = pl.program_id(1)
    off = off0 + d * off_step
    strip = min(rows, CHUNK)
    for r0 in range(0, rows, strip):
        r = r0 + lax.broadcasted_iota(jnp.int32, (strip, cols), 0)
        c = lax.broadcasted_iota(jnp.int32, (strip, cols), 1)
        qi, ki = (c, r) if transposed else (r, c)
        q_lo, q_hi = (0, cols - 1) if transposed else (r0, r0 + strip - 1)
        k_lo, k_hi = (r0, r0 + strip - 1) if transposed else (0, cols - 1)
        idx = jnp.clip(off + qi - ki, -REL_CLIP, REL_CLIP) + REL_CLIP
        lo = jnp.clip(off + q_lo - k_hi, -REL_CLIP, REL_CLIP) + REL_CLIP
        hi = jnp.clip(off + q_hi - k_lo, -REL_CLIP, REL_CLIP) + REL_CLIP

        def body(j, acc, idx=idx):
            return jnp.where(idx == j, tab_ref[j * HB + h], acc)

        bias = lax.fori_loop(lo, hi + 1, body, jnp.zeros((strip, cols), F32)) * LOG2E
        if masked:
            dc = d * (off_step // CHUNK) + qi // CHUNK - ki // CHUNK
            bias = jnp.where((dc >= 0) & (dc <= BAND_CHUNKS), bias, NEG)
        o_ref[r0:r0 + strip, :] = bias


def _rel_bias(table, n_off, rows, cols, off0, off_step, masked, transposed):
    kern = functools.partial(_rel_bias_kernel, rows=rows, cols=cols, off0=off0,
                             off_step=off_step, masked=masked, transposed=transposed)
    return pl.pallas_call(
        kern,
        grid=(HB, n_off),
        in_specs=[pl.BlockSpec(memory_space=pltpu.SMEM)],
        out_specs=pl.BlockSpec((None, None, rows, cols), lambda h, d: (h, d, 0, 0)),
        out_shape=jax.ShapeDtypeStruct((HB, n_off, rows, cols), F32),
        compiler_params=_cparams("parallel", "parallel"),
        name="rel_bias",
    )(table.reshape(N_REL * HB))


def _pair_softmax_out(q, ks, vs, bias_fn):
    lane = lax.broadcasted_iota(jnp.int32, (q.shape[0], LANES), 1)
    outs = []
    for hh, qm in enumerate(_split_halves(q)):
        ss = [_dot_nt(qm, k) + bias_fn(hh, j) for j, k in enumerate(ks)]
        m = functools.reduce(jnp.maximum, [jnp.max(s, axis=-1, keepdims=True) for s in ss])
        ps = [jnp.exp2(s - m) for s in ss]
        l = functools.reduce(jnp.add, [jnp.sum(p, axis=-1, keepdims=True) for p in ps])
        o = functools.reduce(jnp.add, [_dot(p.astype(BF16), v) for p, v in zip(ps, vs)])
        outs.append(o / l)
    return jnp.where(lane < 64, outs[0], outs[1])


BAND_TILE = 4 * CHUNK
N_BAND_TILES = BAND // BAND_TILE + 1


def _band_attn_kernel(q_ref, k0_ref, k1_ref, k2_ref, v0_ref, v1_ref, v2_ref, b_ref, o_ref, sa_s, sb_s):
    i = pl.program_id(1)
    k_refs = (k0_ref, k1_ref, k2_ref)
    vt_refs = (v0_ref, v1_ref, v2_ref)
    row = lax.broadcasted_iota(jnp.int32, (LANES, BAND_TILE), 0)

    def produce(h, buf, guarded):
        cols = slice((h // 2) * LANES, (h // 2 + 1) * LANES)
        qm = _split_halves(q_ref[:, cols])[h % 2]
        for d, k_ref in enumerate(k_refs):
            s = _dot_nt(k_ref[:, cols], qm) + b_ref[h, d]
            if guarded and d > 0:
                s = jnp.where(i >= d, s, NEG)
            buf[d] = s

    def consume(h, buf):
        cols = slice((h // 2) * LANES, (h // 2 + 1) * LANES)
        ss = [buf[d] for d in range(N_BAND_TILES)]
        m = functools.reduce(jnp.maximum, [jnp.max(s, axis=0, keepdims=True) for s in ss])
        ps = [jnp.exp2(s - m) for s in ss]
        l = functools.reduce(jnp.add, [jnp.sum(p, axis=0, keepdims=True) for p in ps])
        ot = functools.reduce(jnp.add, [_dot(r[cols, :], p.astype(BF16)) for r, p in zip(vt_refs, ps)])
        return ot / l

    def all_heads(guarded):
        bufs = (sa_s, sb_s)
        produce(0, bufs[0], guarded)
        prev = None
        for h in range(HB):
            if h + 1 < HB:
                produce(h + 1, bufs[(h + 1) % 2], guarded)
            ot = consume(h, bufs[h % 2])
            if h % 2 == 1:
                cols = slice((h // 2) * LANES, (h // 2 + 1) * LANES)
                o_ref[:, cols] = jnp.where(row < DH_B, prev, ot).T.astype(BF16)
            prev = ot

    @pl.when(i >= N_BAND_TILES - 1)
    def _():
        all_heads(False)

    @pl.when(i < N_BAND_TILES - 1)
    def _():
        all_heads(True)


def _band_attn_prompt(q, k, vt, bias_t):
    B, L, _ = q.shape
    t = BAND_TILE
    per = vt.shape[3] // t
    qspec = pl.BlockSpec((None, t, W_B), lambda b, i: (b, i, 0))
    kspec = lambda d: pl.BlockSpec((None, t, W_B), lambda b, i: (b, jnp.maximum(i - d, 0), 0))
    vspec = lambda d: pl.BlockSpec(
        (None, None, W_B, t),
        lambda b, i: (b, jnp.maximum(i - d, 0) // per, 0, jnp.maximum(i - d, 0) % per))
    return pl.pallas_call(
        _band_attn_kernel,
        grid=(B, L // t),
        in_specs=[qspec, kspec(0), kspec(1), kspec(2), vspec(0), vspec(1), vspec(2),
                  pl.BlockSpec((HB, N_BAND_TILES, t, t), lambda b, i: (0, 0, 0, 0),
                               pipeline_mode=pl.Buffered(1))],
        out_specs=qspec,
        out_shape=jax.ShapeDtypeStruct((B, L, W_B), BF16),
        scratch_shapes=[pltpu.VMEM((N_BAND_TILES, t, t), F32)] * 2,
        compiler_params=_cparams("parallel", "parallel"),
        name="band_attn",
    )(q, k, k, k, vt, vt, vt, bias_t)


def _band_attn_sample_kernel(q_ref, ck_ref, cv_ref, k_ref, v_ref, bc_ref, bn_ref, o_ref):
    for hp in range(HB // 2):
        cols = slice(hp * LANES, (hp + 1) * LANES)
        ks = [ck_ref[:, cols], k_ref[:, cols]]
        vs = [cv_ref[:, cols], v_ref[:, cols]]

        def bias_fn(hh, j, hp=hp):
            return (bc_ref if j == 0 else bn_ref)[2 * hp + hh, 0]

        o_ref[:, cols] = _pair_softmax_out(q_ref[:, cols], ks, vs, bias_fn).astype(BF16)


def _band_attn_sample(q, ck, cv, k, v, bias_c, bias_n):
    B, T, _ = q.shape
    P = ck.shape[1]
    new = pl.BlockSpec((None, T, W_B), lambda b: (b, 0, 0))
    old = pl.BlockSpec((None, P, W_B), lambda b: (b, 0, 0))
    return pl.pallas_call(
        _band_attn_sample_kernel,
        grid=(B,),
        in_specs=[new, old, old, new, new,
                  pl.BlockSpec((HB, 1, T, P), lambda b: (0, 0, 0, 0)),
                  pl.BlockSpec((HB, 1, T, T), lambda b: (0, 0, 0, 0))],
        out_specs=new,
        out_shape=jax.ShapeDtypeStruct((B, T, W_B), BF16),
        compiler_params=_cparams("parallel"),
        name="band_attn_sample",
    )(q, ck, cv, k, v, bias_c, bias_n)


N_SLAB = N_STATE // MXU_DIM


def _ssm_kernel(u_ref, wxr_ref, wxi_ref, ar_ref, ai_ref, h0r_ref, h0i_ref, wcr_ref, wci_ref,
                d_ref, wglu_ref, oc_ref, hro_ref, hio_ref, xr_s, xi_s, hr_s, hi_s, io_s, *, tt, nb, half):
    i = pl.program_id(0)

    @pl.when(i == 0)
    def _():
        hr_s[...] = h0r_ref[...]
        hi_s[...] = h0i_ref[...]

    n_ch = W_C // LANES
    for b in range(nb):
        for s in range(n_ch):
            c0 = b * W_C + s * LANES
            io_s[s, pl.ds(b, tt, stride=nb), :] = u_ref[:, c0:c0 + LANES]
    u = jnp.concatenate([io_s[s] for s in range(n_ch)], axis=-1)
    ub = u.astype(BF16)
    for j in range(N_SLAB):
        us = ub[:, LANES * (j // 2):LANES * (j // 2 + 1)]
        xr_s[:, MXU_DIM * j:MXU_DIM * (j + 1)] = _dot(us, wxr_ref[j])
        xi_s[:, MXU_DIM * j:MXU_DIM * (j + 1)] = _dot(us, wxi_ref[j])

    for c in range(N_STATE // half):
        cols = slice(c * half, (c + 1) * half)
        ar = jnp.broadcast_to(ar_ref[:, cols], (nb, half))
        ai = jnp.broadcast_to(ai_ref[:, cols], (nb, half))

        def step(t, carry, cols=cols, ar=ar, ai=ai):
            hr, hi = carry
            r0 = pl.multiple_of(t * nb, nb)
            nhr = ar * hr - ai * hi + xr_s[pl.ds(r0, nb), cols]
            nhi = ar * hi + ai * hr + xi_s[pl.ds(r0, nb), cols]
            xr_s[pl.ds(r0, nb), cols] = nhr
            xi_s[pl.ds(r0, nb), cols] = nhi
            return nhr, nhi

        hr, hi = lax.fori_loop(0, tt, step, (hr_s[:, cols], hi_s[:, cols]), unroll=4)
        hr_s[:, cols] = hr
        hi_s[:, cols] = hi

    hro_ref[...] = hr_s[...]
    hio_ref[...] = hi_s[...]

    ys = []
    for s in range(W_C // LANES):
        acc = None
        for j in (2 * s, 2 * s + 1):
            hrb = xr_s[:, MXU_DIM * j:MXU_DIM * (j + 1)].astype(BF16)
            hib = xi_s[:, MXU_DIM * j:MXU_DIM * (j + 1)].astype(BF16)
            part = _dot(hrb, wcr_ref[j]) + _dot(hib, wci_ref[j])
            acc = part if acc is None else acc + part
        ys.append(acc)
    y = jnp.concatenate(ys, axis=-1) + d_ref[...] * u
    ge = 0.5 * y * (1.0 + jnp.tanh(math.sqrt(2.0 / math.pi) * (y + 0.044715 * (y * y * y))))
    gl = _dot(ge.astype(BF16), wglu_ref[...])
    oc = gl[:, :W_C] * _sigmoid(gl[:, W_C:])
    for s in range(n_ch):
        io_s[s] = oc[:, s * LANES:(s + 1) * LANES]
    for b in range(nb):
        for s in range(n_ch):
            c0 = b * W_C + s * LANES
            oc_ref[:, c0:c0 + LANES] = io_s[s, pl.ds(b, tt, stride=nb), :].astype(BF16)


def _ssm(u_tm, wxr, wxi, ar, ai, h0r, h0i, wcr, wci, d, wglu, tt):
    L = u_tm.shape[0]
    nb = h0r.shape[0]
    rows = tt * nb
    half = N_STATE // 2
    kern = functools.partial(_ssm_kernel, tt=tt, nb=nb, half=half)
    full = lambda shape: pl.BlockSpec(shape, lambda i: (0,) * len(shape))
    return pl.pallas_call(
        kern,
        grid=(L // tt,),
        in_specs=[pl.BlockSpec((tt, nb * W_C), lambda i: (i, 0)),
                  full((N_SLAB, LANES, MXU_DIM)), full((N_SLAB, LANES, MXU_DIM)),
                  full((1, N_STATE)), full((1, N_STATE)),
                  full((nb, N_STATE)), full((nb, N_STATE)),
                  full((N_SLAB, MXU_DIM, LANES)), full((N_SLAB, MXU_DIM, LANES)),
                  full((1, W_C)), full((W_C, 2 * W_C))],
        out_specs=[pl.BlockSpec((tt, nb * W_C), lambda i: (i, 0)),
                   full((nb, N_STATE)), full((nb, N_STATE))],
        out_shape=[jax.ShapeDtypeStruct((L, nb * W_C), BF16),
                   jax.ShapeDtypeStruct((nb, N_STATE), F32),
                   jax.ShapeDtypeStruct((nb, N_STATE), F32)],
        scratch_shapes=[pltpu.VMEM((rows, N_STATE), F32), pltpu.VMEM((rows, N_STATE), F32),
                        pltpu.VMEM((nb, N_STATE), F32), pltpu.VMEM((nb, N_STATE), F32),
                        pltpu.VMEM((W_C // LANES, rows, LANES), F32)],
        compiler_params=_cparams("arbitrary"),
        name="ssm",
    )(u_tm, wxr, wxi, ar, ai, h0r, h0i, wcr, wci, d, wglu)


def _ssm_weights(w):
    ar, ai = w['ssm_a_re'].astype(F32), w['ssm_a_im'].astype(F32)
    dt = jnp.exp(w['ssm_log_dt'].astype(F32))[:, None]
    mag = jnp.exp(ar * dt)
    abar_r, abar_i = mag * jnp.cos(ai * dt), mag * jnp.sin(ai * dt)
    den = ar * ar + ai * ai
    nr, ni = abar_r - 1.0, abar_i
    z_r = (nr * ar + ni * ai) / den
    z_i = (ni * ar - nr * ai) / den
    br, bi = w['ssm_b_re'].astype(F32), w['ssm_b_im'].astype(F32)
    bb_r = z_r[..., None] * br - z_i[..., None] * bi
    bb_i = z_r[..., None] * bi + z_i[..., None] * br

    gps = MXU_DIM // P_STATE
    eye = jnp.eye(gps, dtype=F32)

    def x_slabs(bb):
        b4 = bb.reshape(N_SLAB, gps, P_STATE, GC)
        blk = jnp.einsum('jgpc,gh->jhcgp', b4, eye).reshape(N_SLAB, gps * GC, MXU_DIM)
        zero = jnp.zeros_like(blk)
        even = jnp.concatenate([blk, zero], axis=1)
        odd = jnp.concatenate([zero, blk], axis=1)
        sel = (jnp.arange(N_SLAB) % 2 == 0)[:, None, None]
        return jnp.where(sel, even, odd).astype(BF16)

    def y_slabs(cc):
        c4 = cc.reshape(N_SLAB, gps, GC, P_STATE)
        blk = jnp.einsum('jgcp,gh->jgphc', c4, eye).reshape(N_SLAB, MXU_DIM, gps * GC)
        zero = jnp.zeros_like(blk)
        even = jnp.concatenate([blk, zero], axis=2)
        odd = jnp.concatenate([zero, blk], axis=2)
        sel = (jnp.arange(N_SLAB) % 2 == 0)[:, None, None]
        return jnp.where(sel, even, odd).astype(BF16)

    return dict(wxr=x_slabs(bb_r), wxi=x_slabs(bb_i),
                ar=abar_r.reshape(1, N_STATE), ai=abar_i.reshape(1, N_STATE),
                wcr=y_slabs(w['ssm_c_re'].astype(F32)), wci=y_slabs(-w['ssm_c_im'].astype(F32)))


def _merge_out_kernel(x_ref, oa_ref, ob_ref, oc_ref, g_ref, wg_ref, wa_ref, wb_ref, wc_ref, wo_ref, o_ref):
    x = x_ref[...]
    h = _rms(x, g_ref[...]).astype(BF16)
    merged = None
    for j, (br_ref, w_ref) in enumerate(((oa_ref, wa_ref), (ob_ref, wb_ref), (oc_ref, wc_ref))):
        gate = _sigmoid(_dot(h, wg_ref[:, j * D_MODEL:(j + 1) * D_MODEL]))
        term = gate * _dot(br_ref[...], w_ref[...])
        merged = term if merged is None else merged + term
    o_ref[...] = x + _dot(merged.astype(BF16), wo_ref[...])


def _merge_out(x, oa, ob, oc_tm, g, wg, wa, wb, wc, wo, tm):
    B, L, D = x.shape
    full = lambda shape: pl.BlockSpec(shape, lambda b, t: (0,) * len(shape))
    br = pl.BlockSpec((None, tm, 512), lambda b, t: (b, t, 0))
    return pl.pallas_call(
        _merge_out_kernel,
        grid=(B, L // tm),
        in_specs=[pl.BlockSpec((None, tm, D), lambda b, t: (b, t, 0)), br, br,
                  pl.BlockSpec((tm, 512), lambda b, t: (t, b)),
                  full((1, D)), full((D, 3 * D)), full((W_A, D)), full((W_B, D)), full((W_C, D)), full((D, D))],
        out_specs=pl.BlockSpec((None, tm, D), lambda b, t: (b, t, 0)),
        out_shape=jax.ShapeDtypeStruct((B, L, D), F32),
        compiler_params=_cparams("parallel", "parallel"),
        name="merge_out",
    )(x, oa, ob, oc_tm, g, wg, wa, wb, wc, wo)


FF_CHUNK = 2 * MXU_DIM


def _swiglu(h, w1_ref, w3_ref, w2_ref):
    n_ff = w1_ref.shape[1]
    acc = None
    for c0 in range(0, n_ff, FF_CHUNK):
        c1 = min(c0 + FF_CHUNK, n_ff)
        a = _dot(h, w1_ref[:, c0:c1])
        b = _dot(h, w3_ref[:, c0:c1])
        part = _dot((a * _sigmoid(a) * b).astype(BF16), w2_ref[c0:c1, :])
        acc = part if acc is None else acc + part
    return acc


def _ffn_kernel(x_ref, g_ref, w1_ref, w3_ref, w2_ref, fg_ref, o_ref, *, final_norm):
    x = x_ref[...]
    h = _rms(x, g_ref[...]).astype(BF16)
    y = x + _swiglu(h, w1_ref, w3_ref, w2_ref)
    o_ref[...] = _rms(y, fg_ref[...]) if final_norm else y


def _ffn(x, g, w1, w3, w2, final_g, final_norm, tm):
    B, L, D = x.shape
    full = lambda shape: pl.BlockSpec(shape, lambda b, t: (0,) * len(shape), pipeline_mode=pl.Buffered(1))
    tok = pl.BlockSpec((None, tm, D), lambda b, t: (b, t, 0))
    return pl.pallas_call(
        functools.partial(_ffn_kernel, final_norm=final_norm),
        grid=(B, L // tm),
        in_specs=[tok, full((1, D)), full((D, D_FF)), full((D, D_FF)), full((D_FF, D)), full((1, D))],
        out_specs=tok,
        out_shape=jax.ShapeDtypeStruct((B, L, D), F32),
        compiler_params=_cparams("parallel", "parallel"),
        name="ffn",
    )(x, g, w1, w3, w2, final_g)


R_I1, R_I2, R_R1, R_R2, R_G1, R_G2 = range(6)


def _moe_route_kernel(x_ref, g_ref, rw2_ref, tri_ref, route_ref, routet_ref, cnt_ref, cnt_s):
    @pl.when(pl.program_id(0) == 0)
    def _():
        cnt_s[...] = jnp.zeros_like(cnt_s)

    lane = lax.broadcasted_iota(jnp.int32, route_ref.shape, 1)
    hf = _rms(x_ref[...], g_ref[...])
    h_hi = hf.astype(BF16)
    h_lo = (hf - h_hi.astype(F32)).astype(BF16)
    hi_both = _dot(h_hi, rw2_ref[...])
    logits = hi_both[:, :LANES] + (hi_both[:, LANES:] + _dot(h_lo, rw2_ref[:, :LANES]))
    logits = jnp.where(lane < N_EXPERTS, logits, -jnp.inf)
    v1 = jnp.max(logits, axis=-1, keepdims=True)
    i1 = jnp.min(jnp.where(logits == v1, lane, LANES), axis=-1, keepdims=True)
    rest = jnp.where(lane == i1, -jnp.inf, logits)
    v2 = jnp.max(rest, axis=-1, keepdims=True)
    i2 = jnp.min(jnp.where(rest == v2, lane, LANES), axis=-1, keepdims=True)
    ex = jnp.exp(v2 - v1)
    g1 = 1.0 / (1.0 + ex)
    g2 = ex / (1.0 + ex)

    oh1 = (lane == i1).astype(F32)
    oh2 = (lane == i2).astype(F32)
    oh = oh1 + oh2
    incl = _dot(tri_ref[...], oh.astype(BF16))
    rank = cnt_s[...] + incl - oh
    r1 = jnp.sum(oh1 * rank, axis=-1, keepdims=True)
    r2 = jnp.sum(oh2 * rank, axis=-1, keepdims=True)
    cnt_s[...] += jnp.sum(oh, axis=0, keepdims=True)

    rec = jnp.zeros(route_ref.shape, F32)
    for ln, val in ((R_I1, i1.astype(F32)), (R_I2, i2.astype(F32)), (R_R1, r1), (R_R2, r2),
                    (R_G1, g1), (R_G2, g2)):
        rec = jnp.where(lane == ln, val, rec)
    route_ref[...] = rec
    routet_ref[...] = rec.T[:SUBLANES, :]
    cnt_ref[...] = cnt_s[...]


def _moe_route(x, g, rw, tm):
    n, d = x.shape
    tri = (jnp.arange(tm)[:, None] >= jnp.arange(tm)[None, :]).astype(BF16)
    rw_hi = rw.astype(BF16)
    rw_lo = (rw - rw_hi.astype(F32)).astype(BF16)
    full = lambda shape: pl.BlockSpec(shape, lambda i: (0,) * len(shape))
    return pl.pallas_call(
        _moe_route_kernel,
        grid=(n // tm,),
        in_specs=[pl.BlockSpec((tm, d), lambda i: (i, 0)), full((1, d)), full((d, 2 * LANES)), full((tm, tm))],
        out_specs=[pl.BlockSpec((tm, LANES), lambda i: (i, 0)), pl.BlockSpec((SUBLANES, tm), lambda i: (0, i)),
                   full((1, LANES))],
        out_shape=[jax.ShapeDtypeStruct((n, LANES), F32), jax.ShapeDtypeStruct((SUBLANES, n), F32),
                   jax.ShapeDtypeStruct((1, LANES), F32)],
        scratch_shapes=[pltpu.VMEM((1, LANES), F32)],
        compiler_params=_cparams("arbitrary"),
        name="moe_route",
    )(x, g, jnp.concatenate([rw_hi, rw_lo], axis=1), tri)


def _row_copies(pos_ref, tm, make_copy):
    def body(k, c):
        for j in range(SUBLANES):
            for s in range(2):
                make_copy(s, k, j, pos_ref[0, s * tm + k * SUBLANES + j]).start(priority=s)
        return c

    lax.fori_loop(0, tm // SUBLANES, body, 0)


N_PAD_SEGS = N_EXPERTS + 1


def _moe_dispatch_kernel(pad_ref, pos_ref, x_ref, xs_ref, zrow_s, sem, zsem, *, tm):
    @pl.when(pl.program_id(0) == 0)
    def _():
        zrow_s[...] = jnp.zeros_like(zrow_s)
        zero_row = zrow_s.at[pl.ds(0, 1)]
        total = 0
        for seg in range(N_PAD_SEGS):
            start, count = pad_ref[0, seg], pad_ref[1, seg]

            def start_one(r, c, start=start):
                pltpu.make_async_copy(zero_row, xs_ref.at[pl.ds(start + r, 1)], zsem).start()
                return c

            lax.fori_loop(0, count, start_one, 0)
            total = total + count

        def wait_one(r, c):
            pltpu.make_async_copy(zero_row, xs_ref.at[pl.ds(0, 1)], zsem).wait()
            return c

        lax.fori_loop(0, total, wait_one, 0)

    _row_copies(pos_ref, tm, lambda s, k, j, p: pltpu.make_async_copy(
        x_ref.at[k, pl.ds(j, 1)], xs_ref.at[pl.ds(p, 1)], sem))
    pltpu.make_async_copy(xs_ref.at[pl.ds(0, 2 * tm)], xs_ref.at[pl.ds(0, 2 * tm)], sem).wait()


def _moe_dispatch(x, pos, pad, n_rows, tm):
    n, d = x.shape
    return pl.pallas_call(
        functools.partial(_moe_dispatch_kernel, tm=tm),
        grid=(n // tm,),
        in_specs=[pl.BlockSpec(memory_space=pltpu.SMEM),
                  pl.BlockSpec((None, 1, 2 * tm), lambda i: (i, 0, 0), memory_space=pltpu.SMEM),
                  pl.BlockSpec((tm // SUBLANES, SUBLANES, d), lambda i: (i, 0, 0))],
        out_specs=pl.BlockSpec(memory_space=pl.ANY),
        out_shape=jax.ShapeDtypeStruct((n_rows, d), F32),
        scratch_shapes=[pltpu.VMEM((SUBLANES, d), F32), pltpu.SemaphoreType.DMA(()),
                        pltpu.SemaphoreType.DMA(())],
        compiler_params=_cparams("arbitrary"),
        name="moe_dispatch",
    )(pad, pos, x.reshape(n // SUBLANES, SUBLANES, d))


def _moe_expert_kernel(e_ref, rows_ref, xs_ref, g_ref, w1_ref, w3_ref, w2_ref, ys_ref):
    rows = rows_ref[pl.program_id(0)]

    @pl.when(rows > 0)
    def _():
        h = _rms(xs_ref[...], g_ref[...]).astype(BF16)
        ys_ref[...] = _swiglu(h, w1_ref, w3_ref, w2_ref)

    @pl.when(rows == 0)
    def _():
        ys_ref[...] = jnp.zeros_like(ys_ref)


def _moe_experts(xs, g, w1, w3, w2, tile_e, tile_rows, t):
    p, d = xs.shape
    wspec = lambda shape: pl.BlockSpec((None,) + shape, lambda i, e, rows: (e[i], 0, 0))
    row = pl.BlockSpec((t, d), lambda i, e, rows: (i, 0))
    return pl.pallas_call(
        _moe_expert_kernel,
        grid_spec=pltpu.PrefetchScalarGridSpec(
            num_scalar_prefetch=2,
            grid=(p // t,),
            in_specs=[row, pl.BlockSpec((1, d), lambda i, e, rows: (0, 0)),
                      wspec((d, D_FF_E)), wspec((d, D_FF_E)), wspec((D_FF_E, d))],
            out_specs=row),
        out_shape=jax.ShapeDtypeStruct((p, d), F32),
        compiler_params=_cparams("arbitrary"),
        name="moe_experts",
    )(tile_e, tile_rows, xs, g, w1, w3, w2)


def _moe_combine_kernel(pos_ref, x_ref, route_ref, fg_ref, ys_ref, o_ref, ybuf, sem, *, tm, final_norm):
    _row_copies(pos_ref, tm, lambda s, k, j, p: pltpu.make_async_copy(
        ys_ref.at[pl.ds(p, 1)], ybuf.at[s, k, pl.ds(j, 1)], sem))
    for s in range(2):
        pltpu.make_async_copy(ybuf.at[s], ybuf.at[s], sem).wait()
    rec = route_ref[...]
    lane = lax.broadcasted_iota(jnp.int32, rec.shape, 1)
    g1 = jnp.sum(jnp.where(lane == R_G1, rec, 0.0), axis=-1, keepdims=True)
    g2 = jnp.sum(jnp.where(lane == R_G2, rec, 0.0), axis=-1, keepdims=True)
    y1 = ybuf[0].reshape(x_ref.shape)
    y2 = ybuf[1].reshape(x_ref.shape)
    y = x_ref[...] + (g1 * y1 + g2 * y2)
    o_ref[...] = _rms(y, fg_ref[...]) if final_norm else y


def _moe_combine(x, route, pos, ys, final_g, final_norm, tm):
    n, d = x.shape
    tok = pl.BlockSpec((tm, d), lambda i: (i, 0))
    return pl.pallas_call(
        functools.partial(_moe_combine_kernel, tm=tm, final_norm=final_norm),
        grid=(n // tm,),
        in_specs=[pl.BlockSpec((None, 1, 2 * tm), lambda i: (i, 0, 0), memory_space=pltpu.SMEM),
                  tok, pl.BlockSpec((tm, LANES), lambda i: (i, 0)),
                  pl.BlockSpec((1, d), lambda i: (0, 0)), pl.BlockSpec(memory_space=pl.ANY)],
        out_specs=tok,
        out_shape=jax.ShapeDtypeStruct((n, d), F32),
        scratch_shapes=[pltpu.VMEM((2, tm // SUBLANES, SUBLANES, d), F32), pltpu.SemaphoreType.DMA(())],
        compiler_params=_cparams("arbitrary"),
        name="moe_combine",
    )(pos, x, route, final_g, ys)


def _moe(x, g, rw, w1, w3, w2, final_g, final_norm):
    B, L, D = x.shape
    n = B * L
    tm = min(512, n)
    xf = x.reshape(n, D)
    route, route_t, counts = _moe_route(xf, g, rw, tm)

    idx = route_t.astype(jnp.int32)
    cnt = counts[0, :N_EXPERTS].astype(jnp.int32)
    tiles_e = (cnt + tm - 1) // tm
    ends = jnp.cumsum(tiles_e)
    starts = ends - tiles_e
    n_used = ends[-1]
    n_tiles = 2 * n // tm + N_EXPERTS
    pos = jnp.stack([starts[idx[R_I1]] * tm + idx[R_R1],
                     starts[idx[R_I2]] * tm + idx[R_R2]])
    pos = pos.reshape(2, n // tm, tm).transpose(1, 0, 2).reshape(n // tm, 1, 2 * tm)
    tile = jnp.arange(n_tiles, dtype=jnp.int32)
    tile_e = jnp.searchsorted(ends, jnp.minimum(tile, n_used - 1), side='right').astype(jnp.int32)
    tile_rows = jnp.where(tile < n_used, jnp.clip(cnt[tile_e] - (tile - starts[tile_e]) * tm, 0, tm), 0)

    pad_start = jnp.concatenate([starts * tm + cnt, (n_used * tm)[None]])
    pad_count = jnp.concatenate([tiles_e * tm - cnt, ((n_tiles - n_used) * tm)[None]])
    pad = jnp.stack([pad_start, pad_count]).astype(jnp.int32)

    xs = _moe_dispatch(xf, pos, pad, n_tiles * tm, tm)
    ys = _moe_experts(xs, g, w1, w3, w2, tile_e, tile_rows.astype(jnp.int32), tm)
    out = _moe_combine(xf, route, pos, ys, final_g, final_norm, tm)
    return out.reshape(B, L, D)


def _pick(n, cands):
    for c in cands:
        if n % c == 0:
            return c
    return n


def _layer(x, l, w, cache, final_g, last, dkv, depth):
    B, L, D = x.shape
    assert B == SUBLANES, "the SSM kernel puts the streams of one time step on the 8 sublanes"
    lam_init = 0.8 - 0.6 * math.exp(-0.3 * l)
    tm = _pick(L, (512, 256, 128, 64, 32))
    n_keep = min(BAND, L)
    assert n_keep == tm, "the band cache rows must be exactly the last token tile"
    row = lambda a: a.reshape(1, -1).astype(F32)
    prompt = cache is None

    w_in = w['w_in'].astype(BF16)
    dk, dv = dkv if dkv is not None else (None, None)
    p = _norm_proj(x, row(w['norm1_g']), w_in[:, :N_QKVU], tm, prompt, l, depth, dk, dv)

    lam_vecs = jnp.stack([w['lam_q1'], w['lam_k1'], w['lam_q2'], w['lam_k2']]).astype(F32)
    subln_g = w['subln_g'].astype(F32)
    table = w['rel_bias'].astype(F32)
    if prompt:
        oa = _diff_attn_prompt(p['qa'], p['ka_b'], p['va_b'], lam_vecs, subln_g, lam_init)
        bias_t = _rel_bias(table, N_BAND_TILES, BAND_TILE, BAND_TILE, 0, BAND_TILE, True, True)
        ob = _band_attn_prompt(p['qb'], p['kb_b'], p['vb_b'], bias_t)
        h0r = jnp.zeros((B, N_STATE), F32)
        h0i = jnp.zeros((B, N_STATE), F32)
    else:
        ck_a, cv_a, ck_b, cv_b, h0_re, h0_im = cache
        P = ck_a.shape[1]
        bp = ck_b.shape[1]
        oa = _diff_attn_sample(p['qa'], ck_a.reshape(B, P, W_A).astype(BF16), cv_a.reshape(B, P, W_A).astype(BF16),
                               p['ka_b'], p['va_b'], lam_vecs, subln_g, lam_init)
        bias_c = _rel_bias(table, 1, L, bp, bp, 0, False, False)
        bias_n = _rel_bias(table, 1, L, L, 0, 0, False, False)
        ob = _band_attn_sample(p['qb'], ck_b.reshape(B, bp, W_B).astype(BF16), cv_b.reshape(B, bp, W_B).astype(BF16),
                               p['kb_b'], p['vb_b'], bias_c, bias_n)
        h0r = h0_re.reshape(B, N_STATE).astype(F32)
        h0i = h0_im.reshape(B, N_STATE).astype(F32)

    sw = _ssm_weights(w)
    oc_tm, hr, hi = _ssm(p['u_tm'], sw['wxr'], sw['wxi'], sw['ar'], sw['ai'], h0r, h0i,
                         sw['wcr'], sw['wci'], row(w['ssm_d']), w['w_glu'].astype(BF16), _pick(L, (64, 32)))

    x = _merge_out(x, oa, ob, oc_tm, row(w['norm1_g']), w_in[:, N_QKVU:],
                   w['w_br_a'].astype(BF16), w['w_br_b'].astype(BF16), w['w_br_c'].astype(BF16),
                   w['w_out'].astype(BF16), tm)

    fg = row(final_g)
    if l % 2 == 0:
        x = _ffn(x, row(w['norm2_g']), w['ffn_w1'].astype(BF16), w['ffn_w3'].astype(BF16),
                 w['ffn_w2'].astype(BF16), fg, last, tm)
    else:
        rw = jnp.pad(w['router_w'].astype(F32), ((0, 0), (0, LANES - N_EXPERTS)))
        x = _moe(x, row(w['norm2_g']), rw, w['moe_w1'].astype(BF16),
                 w['moe_w3'].astype(BF16), w['moe_w2'].astype(BF16), fg, last)

    small = (p['kb_last'].reshape(B, n_keep, HB, DH_B), p['vb_last'].reshape(B, n_keep, HB, DH_B),
             hr.reshape(B, N_GROUPS, P_STATE), hi.reshape(B, N_GROUPS, P_STATE))
    return x, (p['dk'], p['dv']), small


_PER_LAYER = ('norm1_g', 'w_in', 'lam_q1', 'lam_k1', 'lam_q2', 'lam_k2', 'subln_g', 'rel_bias',
              'ssm_a_re', 'ssm_a_im', 'ssm_log_dt', 'ssm_b_re', 'ssm_b_im', 'ssm_c_re', 'ssm_c_im',
              'ssm_d', 'w_glu', 'w_br_a', 'w_br_b', 'w_br_c', 'w_out', 'norm2_g')


def kernel(x_prompt, x_sample, cache_dk, cache_dv, cache_bk, cache_bv, state_ssm_re, state_ssm_im,
           norm1_g, w_in, lam_q1, lam_k1, lam_q2, lam_k2, subln_g, rel_bias, ssm_a_re, ssm_a_im,
           ssm_log_dt, ssm_b_re, ssm_b_im, ssm_c_re, ssm_c_im, ssm_d, w_glu, w_br_a, w_br_b, w_br_c,
           w_out, norm2_g, ffn_w1, ffn_w3, ffn_w2, router_w, moe_w1, moe_w3, moe_w2, final_g):
    stacked = dict(norm1_g=norm1_g, w_in=w_in, lam_q1=lam_q1, lam_k1=lam_k1, lam_q2=lam_q2, lam_k2=lam_k2,
                   subln_g=subln_g, rel_bias=rel_bias, ssm_a_re=ssm_a_re, ssm_a_im=ssm_a_im,
                   ssm_log_dt=ssm_log_dt, ssm_b_re=ssm_b_re, ssm_b_im=ssm_b_im, ssm_c_re=ssm_c_re,
                   ssm_c_im=ssm_c_im, ssm_d=ssm_d, w_glu=w_glu, w_br_a=w_br_a, w_br_b=w_br_b,
                   w_br_c=w_br_c, w_out=w_out, norm2_g=norm2_g)
    depth = w_in.shape[0]
    yp, ys = x_prompt, x_sample
    dkv_p, dkv_s = None, None
    small_p, small_s = [], []
    for l in range(depth):
        w = {name: stacked[name][l] for name in _PER_LAYER}
        if l % 2 == 0:
            w['ffn_w1'], w['ffn_w3'], w['ffn_w2'] = ffn_w1[l // 2], ffn_w3[l // 2], ffn_w2[l // 2]
        else:
            w['router_w'], w['moe_w1'] = router_w[l // 2], moe_w1[l // 2]
            w['moe_w3'], w['moe_w2'] = moe_w3[l // 2], moe_w2[l // 2]
        last = l == depth - 1
        yp, dkv_p, sp = _layer(yp, l, w, None, final_g, last, dkv_p, depth)
        ys, dkv_s, ss = _layer(ys, l, w, (cache_dk[l], cache_dv[l], cache_bk[l], cache_bv[l],
                                          state_ssm_re[l], state_ssm_im[l]), final_g, last, dkv_s, depth)
        small_p.append(sp)
        small_s.append(ss)
    outs = [yp, ys]
    for dkv, small in ((dkv_p, small_p), (dkv_s, small_s)):
        outs += list(dkv)
        for j in range(4):
            outs.append(jnp.stack([s[j] for s in small]))
    return tuple(outs)
```

```python
import functools
import math

import jax
import jax.numpy as jnp
from jax import lax
from jax.experimental import pallas as pl
from jax.experimental.pallas import tpu as pltpu

F32 = jnp.float32
BF16 = jnp.bfloat16

D_MODEL = 1024
CHUNK = 64
HA = 4
DH_A = 64
DV_A = 2 * DH_A
W_A = HA * DV_A
HB = 8
DH_B = 64
W_B = HB * DH_B
BAND_CHUNKS = 8
BAND = BAND_CHUNKS * CHUNK
REL_CLIP = 128
N_REL = 2 * REL_CLIP + 1
GC = 16
N_GROUPS = 32
W_C = N_GROUPS * GC
P_STATE = 64
N_STATE = N_GROUPS * P_STATE
D_FF = 11 * D_MODEL // 4
N_EXPERTS = 8
D_FF_E = D_FF // 2
EPS = 1e-6
NEG = -1e30
LOG2E = math.log2(math.e)
N_QKVU = 7 * 512

LANES = 128
SUBLANES = 8
MXU_DIM = 256
VMEM_LIMIT = 56 * 1024 * 1024


def _cparams(*sem):
    return pltpu.CompilerParams(dimension_semantics=sem, vmem_limit_bytes=VMEM_LIMIT)


def _rms(x, g):
    return x * lax.rsqrt(jnp.mean(x * x, axis=-1, keepdims=True) + EPS) * g


def _sigmoid(x):
    return 1.0 / (1.0 + jnp.exp(-x))


def _dot(a, b):
    return jnp.dot(a, b, preferred_element_type=F32)


def _dot_nt(a, b):
    return lax.dot_general(a, b, (((1,), (1,)), ((), ())), preferred_element_type=F32)


def _norm_proj_kernel(*refs, transpose_v, n_alias, layer):
    x_ref, g_ref, w_ref = refs[:3]
    (qa_ref, dk_ref, dv_ref, kab_ref, vab_ref,
     qb_ref, kbl_ref, vbl_ref, kbb_ref, vbb_ref, u_ref) = refs[3 + n_alias:]
    if n_alias == 0:
        for slot in range(dk_ref.shape[0]):
            if slot != layer:
                dk_ref[slot] = jnp.zeros(dk_ref.shape[1:], F32)
                dv_ref[slot] = jnp.zeros(dv_ref.shape[1:], F32)
        dk_ref, dv_ref = dk_ref.at[layer], dv_ref.at[layer]
    h = _rms(x_ref[...], g_ref[...]).astype(BF16)

    def proj(c):
        return _dot(h, w_ref[:, c * 512:(c + 1) * 512])

    def store_heads(o_ref, y):
        for hd in range(HA):
            o_ref[:, hd, :] = y[:, hd * DV_A:(hd + 1) * DV_A]

    qa_ref[...] = (proj(0) * (DH_A ** -0.5 * LOG2E)).astype(BF16)
    ka = proj(1)
    store_heads(dk_ref, ka)
    kab_ref[...] = ka.astype(BF16)
    va = proj(2)
    store_heads(dv_ref, va)
    vab_ref[...] = (va.T if transpose_v else va).astype(BF16)
    qb_ref[...] = (proj(3) * (DH_B ** -0.5 * LOG2E)).astype(BF16)
    kb = proj(4)
    kbl_ref[...] = kb
    kbb_ref[...] = kb.astype(BF16)
    vb = proj(5)
    vbl_ref[...] = vb
    vbb_ref[...] = (vb.T if transpose_v else vb).astype(BF16)
    u_ref[...] = proj(6)


def _norm_proj(x, g, w_qkvu, tm, transpose_v, layer, depth, dk, dv):
    B, L, D = x.shape
    nt = L // tm
    tok = jax.ShapeDtypeStruct((B, L, 512), BF16)
    tspec = pl.BlockSpec((None, tm, 512), lambda b, t: (b, t, 0))
    vshape, vspec = tok, tspec
    if transpose_v:
        vshape = jax.ShapeDtypeStruct((B, nt, 512, tm), BF16)
        vspec = pl.BlockSpec((None, None, 512, tm), lambda b, t: (b, t, 0, 0))
    cache = jax.ShapeDtypeStruct((depth, B, L, HA, DV_A), F32)
    cspec = pl.BlockSpec((None, None, tm, HA, DV_A), lambda b, t: (layer, b, t, 0, 0))
    last = jax.ShapeDtypeStruct((B, tm, 512), F32)
    lspec = pl.BlockSpec((None, tm, 512), lambda b, t: (b, 0, 0))
    out_shape = [tok, cache, cache, tok, vshape, tok, last, last, tok, vshape,
                 jax.ShapeDtypeStruct((L, B * 512), F32)]
    out_specs = [tspec, cspec, cspec, tspec, vspec, tspec, lspec, lspec, tspec, vspec,
                 pl.BlockSpec((tm, 512), lambda b, t: (t, b))]
    in_specs = [pl.BlockSpec((None, tm, D), lambda b, t: (b, t, 0)),
                pl.BlockSpec((1, D), lambda b, t: (0, 0)),
                pl.BlockSpec((D, N_QKVU), lambda b, t: (0, 0))]
    args = [x, g, w_qkvu]
    aliases = {}
    if dk is None:
        whole = pl.BlockSpec((depth, None, tm, HA, DV_A), lambda b, t: (0, b, t, 0, 0))
        out_specs[1] = out_specs[2] = whole
    else:
        in_specs += [pl.BlockSpec(memory_space=pl.ANY)] * 2
        args += [dk, dv]
        aliases = {3: 1, 4: 2}
    outs = pl.pallas_call(
        functools.partial(_norm_proj_kernel, transpose_v=transpose_v, n_alias=len(aliases), layer=layer),
        grid=(B, nt),
        in_specs=in_specs,
        out_specs=out_specs,
        out_shape=out_shape,
        input_output_aliases=aliases,
        compiler_params=_cparams("parallel", "arbitrary"),
        name="norm_proj",
    )(*args)
    names = ('qa', 'dk', 'dv', 'ka_b', 'va_b', 'qb', 'kb_last', 'vb_last', 'kb_b', 'vb_b', 'u_tm')
    return dict(zip(names, outs))


def _lam_value(lam_ref, lam_init):
    lv = lam_ref[...]
    e1 = jnp.exp(jnp.sum(lv[0:1, :] * lv[1:2, :], axis=-1, keepdims=True))
    e2 = jnp.exp(jnp.sum(lv[2:3, :] * lv[3:4, :], axis=-1, keepdims=True))
    return e1 - e2 + lam_init


def _subln(o, g, lam_init):
    return _rms(o, g) * (1.0 - lam_init)


def _split_halves(q):
    lane = lax.broadcasted_iota(jnp.int32, q.shape, 1)
    zero = jnp.zeros_like(q)
    return jnp.where(lane < 64, q, zero), jnp.where(lane >= 64, q, zero)


def _diff_attn_kernel(lam_ref, q_ref, k_ref, vt_ref, g_ref, o_ref,
                      m1_s, l1_s, a1_s, m2_s, l2_s, a2_s, sa1_s, sa2_s, sb1_s, sb2_s,
                      xa1_s, xa2_s, xb1_s, xb2_s, *, t, lam_init):
    qi = pl.program_id(2)
    nt = pl.num_programs(2)

    def q_halves(i):
        return _split_halves(q_ref[pl.ds(pl.multiple_of(i * t, t), t), :])

    q_cur = q_halves(qi)
    q_nxt = q_halves(jnp.minimum(qi + 1, nt - 1))

    m1_s[...] = jnp.full_like(m1_s, -jnp.inf)
    m2_s[...] = jnp.full_like(m2_s, -jnp.inf)
    l1_s[...] = jnp.zeros_like(l1_s)
    l2_s[...] = jnp.zeros_like(l2_s)
    a1_s[...] = jnp.zeros_like(a1_s)
    a2_s[...] = jnp.zeros_like(a2_s)

    def update(st, mx, vt, m_s, l_s, a_s):
        m_prev = m_s[...]
        m_new = jnp.maximum(m_prev, mx)
        alpha = jnp.exp2(m_prev - m_new)
        p = jnp.exp2(st - m_new)
        l_s[...] = alpha * l_s[...] + jnp.sum(p, axis=0, keepdims=True)
        a_s[...] = alpha * a_s[...] + _dot(vt, p.astype(BF16))
        m_s[...] = m_new

    buf_a, buf_b = (sa1_s, sa2_s, xa1_s, xa2_s), (sb1_s, sb2_s, xb1_s, xb2_s)

    def produce(kj, buf, qq):
        k = k_ref[pl.ds(pl.multiple_of(kj * t, t), t), :]
        for q, s_ref, x_ref in ((qq[0], buf[0], buf[2]), (qq[1], buf[1], buf[3])):
            st = _dot_nt(k, q)
            s_ref[...] = st
            x_ref[...] = jnp.max(st, axis=0, keepdims=True)

    def consume(kj, buf, masked):
        vt = vt_ref[kj]
        s1, s2 = buf[0][...], buf[1][...]
        if masked:
            kc = lax.broadcasted_iota(jnp.int32, (t, t), 0) // CHUNK
            qc = lax.broadcasted_iota(jnp.int32, (t, t), 1) // CHUNK
            keep = kc <= qc
            s1 = jnp.where(keep, s1, NEG)
            s2 = jnp.where(keep, s2, NEG)
            mx1 = jnp.max(s1, axis=0, keepdims=True)
            mx2 = jnp.max(s2, axis=0, keepdims=True)
        else:
            mx1, mx2 = buf[2][...], buf[3][...]
        update(s1, mx1, vt, m1_s, l1_s, a1_s)
        update(s2, mx2, vt, m2_s, l2_s, a2_s)

    @pl.when(qi == 0)
    def _():
        produce(0, buf_a, q_cur)

    def run(first, second, rem):
        def pair(kj):
            produce(kj + 1, second, q_cur)
            consume(kj, first, False)
            produce(kj + 2, first, q_cur)
            consume(kj + 1, second, False)

        def quad(m, c):
            pair(4 * m)
            pair(4 * m + 2)
            return c

        lax.fori_loop(0, qi // 4, quad, 0)
        if rem >= 2:
            pair((qi // 4) * 4)
        odd = rem % 2 == 1
        if odd:
            produce(qi, second, q_cur)
            consume(qi - 1, first, False)
            produce(0, first, q_nxt)
            consume(qi, second, True)
        else:
            produce(0, second, q_nxt)
            consume(qi, first, True)

    for r, (first, second) in enumerate(((buf_a, buf_b), (buf_b, buf_a), (buf_b, buf_a), (buf_a, buf_b))):
        pl.when(qi % 4 == r)(functools.partial(run, first, second, r))

    lam = _lam_value(lam_ref, lam_init)
    ot = a1_s[...] / l1_s[...] - lam * (a2_s[...] / l2_s[...])
    ot = ot * lax.rsqrt(jnp.mean(ot * ot, axis=0, keepdims=True) + EPS) * g_ref[...] * (1.0 - lam_init)
    o_ref[...] = ot.T.astype(BF16)


def _diff_attn_prompt(q, k, vt, lam_vecs, subln_g, lam_init):
    B, L, _ = q.shape
    nt, t = vt.shape[1], vt.shape[3]
    kern = functools.partial(_diff_attn_kernel, t=t, lam_init=lam_init)
    return pl.pallas_call(
        kern,
        grid=(B, HA, nt),
        in_specs=[pl.BlockSpec((4, DH_A), lambda b, h, i: (0, 0)),
                  pl.BlockSpec((None, L, DV_A), lambda b, h, i: (b, 0, h)),
                  pl.BlockSpec((None, L, DV_A), lambda b, h, i: (b, 0, h)),
                  pl.BlockSpec((None, nt, DV_A, t), lambda b, h, i: (b, 0, h, 0)),
                  pl.BlockSpec((None, DV_A, 1), lambda b, h, i: (h, 0, 0))],
        out_specs=pl.BlockSpec((None, t, DV_A), lambda b, h, i: (b, i, h)),
        out_shape=jax.ShapeDtypeStruct((B, L, W_A), BF16),
        scratch_shapes=[pltpu.VMEM((1, t), F32), pltpu.VMEM((1, t), F32), pltpu.VMEM((DV_A, t), F32),
                        pltpu.VMEM((1, t), F32), pltpu.VMEM((1, t), F32), pltpu.VMEM((DV_A, t), F32)]
                       + [pltpu.VMEM((t, t), F32)] * 4 + [pltpu.VMEM((1, t), F32)] * 4,
        compiler_params=_cparams("parallel", "parallel", "arbitrary"),
        name="diff_attn",
    )(lam_vecs, q, k, vt, subln_g.reshape(HA, DV_A, 1))


def _diff_attn_sample_kernel(lam_ref, q_ref, ck_ref, cv_ref, k_ref, v_ref, g_ref, o_ref, *, lam_init):
    lam = _lam_value(lam_ref, lam_init)
    for h in range(HA):
        cols = slice(h * DV_A, (h + 1) * DV_A)
        q1, q2 = _split_halves(q_ref[:, cols])
        ck, cv = ck_ref[:, cols], cv_ref[:, cols]
        k, v = k_ref[:, cols], v_ref[:, cols]

        def one_map(qm):
            sc = _dot_nt(qm, ck)
            sn = _dot_nt(qm, k)
            m = jnp.maximum(jnp.max(sc, axis=-1, keepdims=True), jnp.max(sn, axis=-1, keepdims=True))
            pc = jnp.exp2(sc - m)
            pn = jnp.exp2(sn - m)
            l = jnp.sum(pc, axis=-1, keepdims=True) + jnp.sum(pn, axis=-1, keepdims=True)
            return (_dot(pc.astype(BF16), cv) + _dot(pn.astype(BF16), v)) / l

        o = one_map(q1) - lam * one_map(q2)
        o_ref[:, cols] = _subln(o, g_ref[h], lam_init).astype(BF16)


def _diff_attn_sample(q, ck, cv, k, v, lam_vecs, subln_g, lam_init):
    B, T, _ = q.shape
    P = ck.shape[1]
    kern = functools.partial(_diff_attn_sample_kernel, lam_init=lam_init)
    new = pl.BlockSpec((None, T, W_A), lambda b: (b, 0, 0))
    old = pl.BlockSpec((None, P, W_A), lambda b: (b, 0, 0))
    return pl.pallas_call(
        kern,
        grid=(B,),
        in_specs=[pl.BlockSpec((4, DH_A), lambda b: (0, 0)), new, old, old, new, new,
                  pl.BlockSpec((HA, 1, DV_A), lambda b: (0, 0, 0))],
        out_specs=new,
        out_shape=jax.ShapeDtypeStruct((B, T, W_A), BF16),
        compiler_params=_cparams("parallel"),
        name="diff_attn_sample",
    )(lam_vecs, q, ck, cv, k, v, subln_g.reshape(HA, 1, DV_A))


def _rel_bias_kernel(tab_ref, o_ref, *, rows, cols, off0, off_step, masked, transposed):
    h = pl.program_id(0)
    d = pl.program_id(1)
    off = off0 + d * off_step
    strip = min(rows, CHUNK)
    for r0 in range(0, rows, strip):
        r = r0 + lax.broadcasted_iota(jnp.int32, (strip, cols), 0)
        c = lax.broadcasted_iota(jnp.int32, (strip, cols), 1)
        qi, ki = (c, r) if transposed else (r, c)
        q_lo, q_hi = (0, cols - 1) if transposed else (r0, r0 + strip - 1)
        k_lo, k_hi = (r0, r0 + strip - 1) if transposed else (0, cols - 1)
        idx = jnp.clip(off + qi - ki, -REL_CLIP, REL_CLIP) + REL_CLIP
        lo = jnp.clip(off + q_lo - k_hi, -REL_CLIP, REL_CLIP) + REL_CLIP
        hi = jnp.clip(off + q_hi - k_lo, -REL_CLIP, REL_CLIP) + REL_CLIP

        def body(j, acc, idx=idx):
            return jnp.where(idx == j, tab_ref[j * HB + h], acc)

        bias = lax.fori_loop(lo, hi + 1, body, jnp.zeros((strip, cols), F32)) * LOG2E
        if masked:
            dc = d * (off_step // CHUNK) + qi // CHUNK - ki // CHUNK
            bias = jnp.where((dc >= 0) & (dc <= BAND_CHUNKS), bias, NEG)
        o_ref[r0:r0 + strip, :] = bias


def _rel_bias(table, n_off, rows, cols, off0, off_step, masked, transposed):
    kern = functools.partial(_rel_bias_kernel, rows=rows, cols=cols, off0=off0,
                             off_step=off_step, masked=masked, transposed=transposed)
    return pl.pallas_call(
        kern,
        grid=(HB, n_off),
        in_specs=[pl.BlockSpec(memory_space=pltpu.SMEM)],
        out_specs=pl.BlockSpec((None, None, rows, cols), lambda h, d: (h, d, 0, 0)),
        out_shape=jax.ShapeDtypeStruct((HB, n_off, rows, cols), F32),
        compiler_params=_cparams("parallel", "parallel"),
        name="rel_bias",
    )(table.reshape(N_REL * HB))


def _pair_softmax_out(q, ks, vs, bias_fn):
    lane = lax.broadcasted_iota(jnp.int32, (q.shape[0], LANES), 1)
    outs = []
    for hh, qm in enumerate(_split_halves(q)):
        ss = [_dot_nt(qm, k) + bias_fn(hh, j) for j, k in enumerate(ks)]
        m = functools.reduce(jnp.maximum, [jnp.max(s, axis=-1, keepdims=True) for s in ss])
        ps = [jnp.exp2(s - m) for s in ss]
        l = functools.reduce(jnp.add, [jnp.sum(p, axis=-1, keepdims=True) for p in ps])
        o = functools.reduce(jnp.add, [_dot(p.astype(BF16), v) for p, v in zip(ps, vs)])
        outs.append(o / l)
    return jnp.where(lane < 64, outs[0], outs[1])


BAND_TILE = 4 * CHUNK
N_BAND_TILES = BAND // BAND_TILE + 1


def _band_attn_kernel(q_ref, k0_ref, k1_ref, k2_ref, v0_ref, v1_ref, v2_ref, b_ref, o_ref, sa_s, sb_s):
    i = pl.program_id(1)
    k_refs = (k0_ref, k1_ref, k2_ref)
    vt_refs = (v0_ref, v1_ref, v2_ref)
    row = lax.broadcasted_iota(jnp.int32, (LANES, BAND_TILE), 0)

    def produce(h, buf, guarded):
        cols = slice((h // 2) * LANES, (h // 2 + 1) * LANES)
        qm = _split_halves(q_ref[:, cols])[h % 2]
        for d, k_ref in enumerate(k_refs):
            s = _dot_nt(k_ref[:, cols], qm) + b_ref[h, d]
            if guarded and d > 0:
                s = jnp.where(i >= d, s, NEG)
            buf[d] = s

    def consume(h, buf):
        cols = slice((h // 2) * LANES, (h // 2 + 1) * LANES)
        ss = [buf[d] for d in range(N_BAND_TILES)]
        m = functools.reduce(jnp.maximum, [jnp.max(s, axis=0, keepdims=True) for s in ss])
        ps = [jnp.exp2(s - m) for s in ss]
        l = functools.reduce(jnp.add, [jnp.sum(p, axis=0, keepdims=True) for p in ps])
        ot = functools.reduce(jnp.add, [_dot(r[cols, :], p.astype(BF16)) for r, p in zip(vt_refs, ps)])
        return ot / l

    def all_heads(guarded):
        bufs = (sa_s, sb_s)
        produce(0, bufs[0], guarded)
        prev = None
        for h in range(HB):
            if h + 1 < HB:
                produce(h + 1, bufs[(h + 1) % 2], guarded)
            ot = consume(h, bufs[h % 2])
            if h % 2 == 1:
                cols = slice((h // 2) * LANES, (h // 2 + 1) * LANES)
                o_ref[:, cols] = jnp.where(row < DH_B, prev, ot).T.astype(BF16)
            prev = ot

    @pl.when(i >= N_BAND_TILES - 1)
    def _():
        all_heads(False)

    @pl.when(i < N_BAND_TILES - 1)
    def _():
        all_heads(True)


def _band_attn_prompt(q, k, vt, bias_t):
    B, L, _ = q.shape
    t = BAND_TILE
    per = vt.shape[3] // t
    qspec = pl.BlockSpec((None, t, W_B), lambda b, i: (b, i, 0))
    kspec = lambda d: pl.BlockSpec((None, t, W_B), lambda b, i: (b, jnp.maximum(i - d, 0), 0))
    vspec = lambda d: pl.BlockSpec(
        (None, None, W_B, t),
        lambda b, i: (b, jnp.maximum(i - d, 0) // per, 0, jnp.maximum(i - d, 0) % per))
    return pl.pallas_call(
        _band_attn_kernel,
        grid=(B, L // t),
        in_specs=[qspec, kspec(0), kspec(1), kspec(2), vspec(0), vspec(1), vspec(2),
                  pl.BlockSpec((HB, N_BAND_TILES, t, t), lambda b, i: (0, 0, 0, 0),
                               pipeline_mode=pl.Buffered(1))],
        out_specs=qspec,
        out_shape=jax.ShapeDtypeStruct((B, L, W_B), BF16),
        scratch_shapes=[pltpu.VMEM((N_BAND_TILES, t, t), F32)] * 2,
        compiler_params=_cparams("parallel", "parallel"),
        name="band_attn",
    )(q, k, k, k, vt, vt, vt, bias_t)


def _band_attn_sample_kernel(q_ref, ck_ref, cv_ref, k_ref, v_ref, bc_ref, bn_ref, o_ref):
    for hp in range(HB // 2):
        cols = slice(hp * LANES, (hp + 1) * LANES)
        ks = [ck_ref[:, cols], k_ref[:, cols]]
        vs = [cv_ref[:, cols], v_ref[:, cols]]

        def bias_fn(hh, j, hp=hp):
            return (bc_ref if j == 0 else bn_ref)[2 * hp + hh, 0]

        o_ref[:, cols] = _pair_softmax_out(q_ref[:, cols], ks, vs, bias_fn).astype(BF16)


def _band_attn_sample(q, ck, cv, k, v, bias_c, bias_n):
    B, T, _ = q.shape
    P = ck.shape[1]
    new = pl.BlockSpec((None, T, W_B), lambda b: (b, 0, 0))
    old = pl.BlockSpec((None, P, W_B), lambda b: (b, 0, 0))
    return pl.pallas_call(
        _band_attn_sample_kernel,
        grid=(B,),
        in_specs=[new, old, old, new, new,
                  pl.BlockSpec((HB, 1, T, P), lambda b: (0, 0, 0, 0)),
                  pl.BlockSpec((HB, 1, T, T), lambda b: (0, 0, 0, 0))],
        out_specs=new,
        out_shape=jax.ShapeDtypeStruct((B, T, W_B), BF16),
        compiler_params=_cparams("parallel"),
        name="band_attn_sample",
    )(q, ck, cv, k, v, bias_c, bias_n)


N_SLAB = N_STATE // MXU_DIM


def _ssm_kernel(u_ref, wxr_ref, wxi_ref, ar_ref, ai_ref, h0r_ref, h0i_ref, wcr_ref, wci_ref,
                d_ref, wglu_ref, oc_ref, hro_ref, hio_ref, xr_s, xi_s, hr_s, hi_s, io_s, *, tt, nb, half):
    i = pl.program_id(0)

    @pl.when(i == 0)
    def _():
        hr_s[...] = h0r_ref[...]
        hi_s[...] = h0i_ref[...]

    n_ch = W_C // LANES
    for b in range(nb):
        for s in range(n_ch):
            c0 = b * W_C + s * LANES
            io_s[s, pl.ds(b, tt, stride=nb), :] = u_ref[:, c0:c0 + LANES]
    u = jnp.concatenate([io_s[s] for s in range(n_ch)], axis=-1)
    ub = u.astype(BF16)
    for j in range(N_SLAB):
        us = ub[:, LANES * (j // 2):LANES * (j // 2 + 1)]
        xr_s[:, MXU_DIM * j:MXU_DIM * (j + 1)] = _dot(us, wxr_ref[j])
        xi_s[:, MXU_DIM * j:MXU_DIM * (j + 1)] = _dot(us, wxi_ref[j])

    for c in range(N_STATE // half):
        cols = slice(c * half, (c + 1) * half)
        ar = jnp.broadcast_to(ar_ref[:, cols], (nb, half))
        ai = jnp.broadcast_to(ai_ref[:, cols], (nb, half))

        def step(t, carry, cols=cols, ar=ar, ai=ai):
            hr, hi = carry
            r0 = pl.multiple_of(t * nb, nb)
            nhr = ar * hr - ai * hi + xr_s[pl.ds(r0, nb), cols]
            nhi = ar * hi + ai * hr + xi_s[pl.ds(r0, nb), cols]
            xr_s[pl.ds(r0, nb), cols] = nhr
            xi_s[pl.ds(r0, nb), cols] = nhi
            return nhr, nhi

        hr, hi = lax.fori_loop(0, tt, step, (hr_s[:, cols], hi_s[:, cols]), unroll=4)
        hr_s[:, cols] = hr
        hi_s[:, cols] = hi

    hro_ref[...] = hr_s[...]
    hio_ref[...] = hi_s[...]

    ys = []
    for s in range(W_C // LANES):
        acc = None
        for j in (2 * s, 2 * s + 1):
            hrb = xr_s[:, MXU_DIM * j:MXU_DIM * (j + 1)].astype(BF16)
            hib = xi_s[:, MXU_DIM * j:MXU_DIM * (j + 1)].astype(BF16)
            part = _dot(hrb, wcr_ref[j]) + _dot(hib, wci_ref[j])
            acc = part if acc is None else acc + part
        ys.append(acc)
    y = jnp.concatenate(ys, axis=-1) + d_ref[...] * u
    ge = 0.5 * y * (1.0 + jnp.tanh(math.sqrt(2.0 / math.pi) * (y + 0.044715 * (y * y * y))))
    gl = _dot(ge.astype(BF16), wglu_ref[...])
    oc = gl[:, :W_C] * _sigmoid(gl[:, W_C:])
    for s in range(n_ch):
        io_s[s] = oc[:, s * LANES:(s + 1) * LANES]
    for b in range(nb):
        for s in range(n_ch):
            c0 = b * W_C + s * LANES
            oc_ref[:, c0:c0 + LANES] = io_s[s, pl.ds(b, tt, stride=nb), :].astype(BF16)


def _ssm(u_tm, wxr, wxi, ar, ai, h0r, h0i, wcr, wci, d, wglu, tt):
    L = u_tm.shape[0]
    nb = h0r.shape[0]
    rows = tt * nb
    half = N_STATE // 2
    kern = functools.partial(_ssm_kernel, tt=tt, nb=nb, half=half)
    full = lambda shape: pl.BlockSpec(shape, lambda i: (0,) * len(shape))
    return pl.pallas_call(
        kern,
        grid=(L // tt,),
        in_specs=[pl.BlockSpec((tt, nb * W_C), lambda i: (i, 0)),
                  full((N_SLAB, LANES, MXU_DIM)), full((N_SLAB, LANES, MXU_DIM)),
                  full((1, N_STATE)), full((1, N_STATE)),
                  full((nb, N_STATE)), full((nb, N_STATE)),
                  full((N_SLAB, MXU_DIM, LANES)), full((N_SLAB, MXU_DIM, LANES)),
                  full((1, W_C)), full((W_C, 2 * W_C))],
        out_specs=[pl.BlockSpec((tt, nb * W_C), lambda i: (i, 0)),
                   full((nb, N_STATE)), full((nb, N_STATE))],
        out_shape=[jax.ShapeDtypeStruct((L, nb * W_C), BF16),
                   jax.ShapeDtypeStruct((nb, N_STATE), F32),
                   jax.ShapeDtypeStruct((nb, N_STATE), F32)],
        scratch_shapes=[pltpu.VMEM((rows, N_STATE), F32), pltpu.VMEM((rows, N_STATE), F32),
                        pltpu.VMEM((nb, N_STATE), F32), pltpu.VMEM((nb, N_STATE), F32),
                        pltpu.VMEM((W_C // LANES, rows, LANES), F32)],
        compiler_params=_cparams("arbitrary"),
        name="ssm",
    )(u_tm, wxr, wxi, ar, ai, h0r, h0i, wcr, wci, d, wglu)


def _ssm_weights(w):
    ar, ai = w['ssm_a_re'].astype(F32), w['ssm_a_im'].astype(F32)
    dt = jnp.exp(w['ssm_log_dt'].astype(F32))[:, None]
    mag = jnp.exp(ar * dt)
    abar_r, abar_i = mag * jnp.cos(ai * dt), mag * jnp.sin(ai * dt)
    den = ar * ar + ai * ai
    nr, ni = abar_r - 1.0, abar_i
    z_r = (nr * ar + ni * ai) / den
    z_i = (ni * ar - nr * ai) / den
    br, bi = w['ssm_b_re'].astype(F32), w['ssm_b_im'].astype(F32)
    bb_r = z_r[..., None] * br - z_i[..., None] * bi
    bb_i = z_r[..., None] * bi + z_i[..., None] * br

    gps = MXU_DIM // P_STATE
    eye = jnp.eye(gps, dtype=F32)

    def x_slabs(bb):
        b4 = bb.reshape(N_SLAB, gps, P_STATE, GC)
        blk = jnp.einsum('jgpc,gh->jhcgp', b4, eye).reshape(N_SLAB, gps * GC, MXU_DIM)
        zero = jnp.zeros_like(blk)
        even = jnp.concatenate([blk, zero], axis=1)
        odd = jnp.concatenate([zero, blk], axis=1)
        sel = (jnp.arange(N_SLAB) % 2 == 0)[:, None, None]
        return jnp.where(sel, even, odd).astype(BF16)

    def y_slabs(cc):
        c4 = cc.reshape(N_SLAB, gps, GC, P_STATE)
        blk = jnp.einsum('jgcp,gh->jgphc', c4, eye).reshape(N_SLAB, MXU_DIM, gps * GC)
        zero = jnp.zeros_like(blk)
        even = jnp.concatenate([blk, zero], axis=2)
        odd = jnp.concatenate([zero, blk], axis=2)
        sel = (jnp.arange(N_SLAB) % 2 == 0)[:, None, None]
        return jnp.where(sel, even, odd).astype(BF16)

    return dict(wxr=x_slabs(bb_r), wxi=x_slabs(bb_i),
                ar=abar_r.reshape(1, N_STATE), ai=abar_i.reshape(1, N_STATE),
                wcr=y_slabs(w['ssm_c_re'].astype(F32)), wci=y_slabs(-w['ssm_c_im'].astype(F32)))


def _merge_out_kernel(x_ref, oa_ref, ob_ref, oc_ref, g_ref, wg_ref, wa_ref, wb_ref, wc_ref, wo_ref, o_ref):
    x = x_ref[...]
    h = _rms(x, g_ref[...]).astype(BF16)
    merged = None
    for j, (br_ref, w_ref) in enumerate(((oa_ref, wa_ref), (ob_ref, wb_ref), (oc_ref, wc_ref))):
        gate = _sigmoid(_dot(h, wg_ref[:, j * D_MODEL:(j + 1) * D_MODEL]))
        term = gate * _dot(br_ref[...], w_ref[...])
        merged = term if merged is None else merged + term
    o_ref[...] = x + _dot(merged.astype(BF16), wo_ref[...])


def _merge_out(x, oa, ob, oc_tm, g, wg, wa, wb, wc, wo, tm):
    B, L, D = x.shape
    full = lambda shape: pl.BlockSpec(shape, lambda b, t: (0,) * len(shape))
    br = pl.BlockSpec((None, tm, 512), lambda b, t: (b, t, 0))
    return pl.pallas_call(
        _merge_out_kernel,
        grid=(B, L // tm),
        in_specs=[pl.BlockSpec((None, tm, D), lambda b, t: (b, t, 0)), br, br,
                  pl.BlockSpec((tm, 512), lambda b, t: (t, b)),
                  full((1, D)), full((D, 3 * D)), full((W_A, D)), full((W_B, D)), full((W_C, D)), full((D, D))],
        out_specs=pl.BlockSpec((None, tm, D), lambda b, t: (b, t, 0)),
        out_shape=jax.ShapeDtypeStruct((B, L, D), F32),
        compiler_params=_cparams("parallel", "parallel"),
        name="merge_out",
    )(x, oa, ob, oc_tm, g, wg, wa, wb, wc, wo)


FF_CHUNK = 2 * MXU_DIM


def _swiglu(h, w1_ref, w3_ref, w2_ref):
    n_ff = w1_ref.shape[1]
    acc = None
    for c0 in range(0, n_ff, FF_CHUNK):
        c1 = min(c0 + FF_CHUNK, n_ff)
        a = _dot(h, w1_ref[:, c0:c1])
        b = _dot(h, w3_ref[:, c0:c1])
        part = _dot((a * _sigmoid(a) * b).astype(BF16), w2_ref[c0:c1, :])
        acc = part if acc is None else acc + part
    return acc


def _ffn_kernel(x_ref, g_ref, w1_ref, w3_ref, w2_ref, fg_ref, o_ref, *, final_norm):
    x = x_ref[...]
    h = _rms(x, g_ref[...]).astype(BF16)
    y = x + _swiglu(h, w1_ref, w3_ref, w2_ref)
    o_ref[...] = _rms(y, fg_ref[...]) if final_norm else y


def _ffn(x, g, w1, w3, w2, final_g, final_norm, tm):
    B, L, D = x.shape
    full = lambda shape: pl.BlockSpec(shape, lambda b, t: (0,) * len(shape), pipeline_mode=pl.Buffered(1))
    tok = pl.BlockSpec((None, tm, D), lambda b, t: (b, t, 0))
    return pl.pallas_call(
        functools.partial(_ffn_kernel, final_norm=final_norm),
        grid=(B, L // tm),
        in_specs=[tok, full((1, D)), full((D, D_FF)), full((D, D_FF)), full((D_FF, D)), full((1, D))],
        out_specs=tok,
        out_shape=jax.ShapeDtypeStruct((B, L, D), F32),
        compiler_params=_cparams("parallel", "parallel"),
        name="ffn",
    )(x, g, w1, w3, w2, final_g)


R_I1, R_I2, R_R1, R_R2, R_G1, R_G2 = range(6)


def _moe_route_kernel(x_ref, g_ref, rw2_ref, tri_ref, route_ref, routet_ref, cnt_ref, cnt_s):
    @pl.when(pl.program_id(0) == 0)
    def _():
        cnt_s[...] = jnp.zeros_like(cnt_s)

    lane = lax.broadcasted_iota(jnp.int32, route_ref.shape, 1)
    hf = _rms(x_ref[...], g_ref[...])
    h_hi = hf.astype(BF16)
    h_lo = (hf - h_hi.astype(F32)).astype(BF16)
    hi_both = _dot(h_hi, rw2_ref[...])
    logits = hi_both[:, :LANES] + (hi_both[:, LANES:] + _dot(h_lo, rw2_ref[:, :LANES]))
    logits = jnp.where(lane < N_EXPERTS, logits, -jnp.inf)
    v1 = jnp.max(logits, axis=-1, keepdims=True)
    i1 = jnp.min(jnp.where(logits == v1, lane, LANES), axis=-1, keepdims=True)
    rest = jnp.where(lane == i1, -jnp.inf, logits)
    v2 = jnp.max(rest, axis=-1, keepdims=True)
    i2 = jnp.min(jnp.where(rest == v2, lane, LANES), axis=-1, keepdims=True)
    ex = jnp.exp(v2 - v1)
    g1 = 1.0 / (1.0 + ex)
    g2 = ex / (1.0 + ex)

    oh1 = (lane == i1).astype(F32)
    oh2 = (lane == i2).astype(F32)
    oh = oh1 + oh2
    incl = _dot(tri_ref[...], oh.astype(BF16))
    rank = cnt_s[...] + incl - oh
    r1 = jnp.sum(oh1 * rank, axis=-1, keepdims=True)
    r2 = jnp.sum(oh2 * rank, axis=-1, keepdims=True)
    cnt_s[...] += jnp.sum(oh, axis=0, keepdims=True)

    rec = jnp.zeros(route_ref.shape, F32)
    for ln, val in ((R_I1, i1.astype(F32)), (R_I2, i2.astype(F32)), (R_R1, r1), (R_R2, r2),
                    (R_G1, g1), (R_G2, g2)):
        rec = jnp.where(lane == ln, val, rec)
    route_ref[...] = rec
    routet_ref[...] = rec.T[:SUBLANES, :]
    cnt_ref[...] = cnt_s[...]


def _moe_route(x, g, rw, tm):
    n, d = x.shape
    tri = (jnp.arange(tm)[:, None] >= jnp.arange(tm)[None, :]).astype(BF16)
    rw_hi = rw.astype(BF16)
    rw_lo = (rw - rw_hi.astype(F32)).astype(BF16)
    full = lambda shape: pl.BlockSpec(shape, lambda i: (0,) * len(shape))
    return pl.pallas_call(
        _moe_route_kernel,
        grid=(n // tm,),
        in_specs=[pl.BlockSpec((tm, d), lambda i: (i, 0)), full((1, d)), full((d, 2 * LANES)), full((tm, tm))],
        out_specs=[pl.BlockSpec((tm, LANES), lambda i: (i, 0)), pl.BlockSpec((SUBLANES, tm), lambda i: (0, i)),
                   full((1, LANES))],
        out_shape=[jax.ShapeDtypeStruct((n, LANES), F32), jax.ShapeDtypeStruct((SUBLANES, n), F32),
                   jax.ShapeDtypeStruct((1, LANES), F32)],
        scratch_shapes=[pltpu.VMEM((1, LANES), F32)],
        compiler_params=_cparams("arbitrary"),
        name="moe_route",
    )(x, g, jnp.concatenate([rw_hi, rw_lo], axis=1), tri)


def _row_copies(pos_ref, tm, make_copy):
    def body(k, c):
        for j in range(SUBLANES):
            for s in range(2):
                make_copy(s, k, j, pos_ref[0, s * tm + k * SUBLANES + j]).start(priority=s)
        return c

    lax.fori_loop(0, tm // SUBLANES, body, 0)


N_PAD_SEGS = N_EXPERTS + 1


def _moe_dispatch_kernel(pad_ref, pos_ref, x_ref, xs_ref, zrow_s, sem, zsem, *, tm):
    @pl.when(pl.program_id(0) == 0)
    def _():
        zrow_s[...] = jnp.zeros_like(zrow_s)
        zero_row = zrow_s.at[pl.ds(0, 1)]
        total = 0
        for seg in range(N_PAD_SEGS):
            start, count = pad_ref[0, seg], pad_ref[1, seg]

            def start_one(r, c, start=start):
                pltpu.make_async_copy(zero_row, xs_ref.at[pl.ds(start + r, 1)], zsem).start()
                return c

            lax.fori_loop(0, count, start_one, 0)
            total = total + count

        def wait_one(r, c):
            pltpu.make_async_copy(zero_row, xs_ref.at[pl.ds(0, 1)], zsem).wait()
            return c

        lax.fori_loop(0, total, wait_one, 0)

    _row_copies(pos_ref, tm, lambda s, k, j, p: pltpu.make_async_copy(
        x_ref.at[k, pl.ds(j, 1)], xs_ref.at[pl.ds(p, 1)], sem))
    pltpu.make_async_copy(xs_ref.at[pl.ds(0, 2 * tm)], xs_ref.at[pl.ds(0, 2 * tm)], sem).wait()


def _moe_dispatch(x, pos, pad, n_rows, tm):
    n, d = x.shape
    return pl.pallas_call(
        functools.partial(_moe_dispatch_kernel, tm=tm),
        grid=(n // tm,),
        in_specs=[pl.BlockSpec(memory_space=pltpu.SMEM),
                  pl.BlockSpec((None, 1, 2 * tm), lambda i: (i, 0, 0), memory_space=pltpu.SMEM),
                  pl.BlockSpec((tm // SUBLANES, SUBLANES, d), lambda i: (i, 0, 0))],
        out_specs=pl.BlockSpec(memory_space=pl.ANY),
        out_shape=jax.ShapeDtypeStruct((n_rows, d), F32),
        scratch_shapes=[pltpu.VMEM((SUBLANES, d), F32), pltpu.SemaphoreType.DMA(()),
                        pltpu.SemaphoreType.DMA(())],
        compiler_params=_cparams("arbitrary"),
        name="moe_dispatch",
    )(pad, pos, x.reshape(n // SUBLANES, SUBLANES, d))


def _moe_expert_kernel(e_ref, rows_ref, xs_ref, g_ref, w1_ref, w3_ref, w2_ref, ys_ref):
    rows = rows_ref[pl.program_id(0)]

    @pl.when(rows > 0)
    def _():
        h = _rms(xs_ref[...], g_ref[...]).astype(BF16)
        ys_ref[...] = _swiglu(h, w1_ref, w3_ref, w2_ref)

    @pl.when(rows == 0)
    def _():
        ys_ref[...] = jnp.zeros_like(ys_ref)


def _moe_experts(xs, g, w1, w3, w2, tile_e, tile_rows, t):
    p, d = xs.shape
    wspec = lambda shape: pl.BlockSpec((None,) + shape, lambda i, e, rows: (e[i], 0, 0))
    row = pl.BlockSpec((t, d), lambda i, e, rows: (i, 0))
    return pl.pallas_call(
        _moe_expert_kernel,
        grid_spec=pltpu.PrefetchScalarGridSpec(
            num_scalar_prefetch=2,
            grid=(p // t,),
            in_specs=[row, pl.BlockSpec((1, d), lambda i, e, rows: (0, 0)),
                      wspec((d, D_FF_E)), wspec((d, D_FF_E)), wspec((D_FF_E, d))],
            out_specs=row),
        out_shape=jax.ShapeDtypeStruct((p, d), F32),
        compiler_params=_cparams("arbitrary"),
        name="moe_experts",
    )(tile_e, tile_rows, xs, g, w1, w3, w2)


def _moe_combine_kernel(pos_ref, x_ref, route_ref, fg_ref, ys_ref, o_ref, ybuf, sem, *, tm, final_norm):
    _row_copies(pos_ref, tm, lambda s, k, j, p: pltpu.make_async_copy(
        ys_ref.at[pl.ds(p, 1)], ybuf.at[s, k, pl.ds(j, 1)], sem))
    for s in range(2):
        pltpu.make_async_copy(ybuf.at[s], ybuf.at[s], sem).wait()
    rec = route_ref[...]
    lane = lax.broadcasted_iota(jnp.int32, rec.shape, 1)
    g1 = jnp.sum(jnp.where(lane == R_G1, rec, 0.0), axis=-1, keepdims=True)
    g2 = jnp.sum(jnp.where(lane == R_G2, rec, 0.0), axis=-1, keepdims=True)
    y1 = ybuf[0].reshape(x_ref.shape)
    y2 = ybuf[1].reshape(x_ref.shape)
    y = x_ref[...] + (g1 * y1 + g2 * y2)
    o_ref[...] = _rms(y, fg_ref[...]) if final_norm else y


def _moe_combine(x, route, pos, ys, final_g, final_norm, tm):
    n, d = x.shape
    tok = pl.BlockSpec((tm, d), lambda i: (i, 0))
    return pl.pallas_call(
        functools.partial(_moe_combine_kernel, tm=tm, final_norm=final_norm),
        grid=(n // tm,),
        in_specs=[pl.BlockSpec((None, 1, 2 * tm), lambda i: (i, 0, 0), memory_space=pltpu.SMEM),
                  tok, pl.BlockSpec((tm, LANES), lambda i: (i, 0)),
                  pl.BlockSpec((1, d), lambda i: (0, 0)), pl.BlockSpec(memory_space=pl.ANY)],
        out_specs=tok,
        out_shape=jax.ShapeDtypeStruct((n, d), F32),
        scratch_shapes=[pltpu.VMEM((2, tm // SUBLANES, SUBLANES, d), F32), pltpu.SemaphoreType.DMA(())],
        compiler_params=_cparams("arbitrary"),
        name="moe_combine",
    )(pos, x, route, final_g, ys)


def _moe(x, g, rw, w1, w3, w2, final_g, final_norm):
    B, L, D = x.shape
    n = B * L
    tm = min(512, n)
    xf = x.reshape(n, D)
    route, route_t, counts = _moe_route(xf, g, rw, tm)

    idx = route_t.astype(jnp.int32)
    cnt = counts[0, :N_EXPERTS].astype(jnp.int32)
    tiles_e = (cnt + tm - 1) // tm
    ends = jnp.cumsum(tiles_e)
    starts = ends - tiles_e
    n_used = ends[-1]
    n_tiles = 2 * n // tm + N_EXPERTS
    pos = jnp.stack([starts[idx[R_I1]] * tm + idx[R_R1],
                     starts[idx[R_I2]] * tm + idx[R_R2]])
    pos = pos.reshape(2, n // tm, tm).transpose(1, 0, 2).reshape(n // tm, 1, 2 * tm)
    tile = jnp.arange(n_tiles, dtype=jnp.int32)
    tile_e = jnp.searchsorted(ends, jnp.minimum(tile, n_used - 1), side='right').astype(jnp.int32)
    tile_rows = jnp.where(tile < n_used, jnp.clip(cnt[tile_e] - (tile - starts[tile_e]) * tm, 0, tm), 0)

    pad_start = jnp.concatenate([starts * tm + cnt, (n_used * tm)[None]])
    pad_count = jnp.concatenate([tiles_e * tm - cnt, ((n_tiles - n_used) * tm)[None]])
    pad = jnp.stack([pad_start, pad_count]).astype(jnp.int32)

    xs = _moe_dispatch(xf, pos, pad, n_tiles * tm, tm)
    ys = _moe_experts(xs, g, w1, w3, w2, tile_e, tile_rows.astype(jnp.int32), tm)
    out = _moe_combine(xf, route, pos, ys, final_g, final_norm, tm)
    return out.reshape(B, L, D)


def _pick(n, cands):
    for c in cands:
        if n % c == 0:
            return c
    return n


def _layer(x, l, w, cache, final_g, last, dkv, depth):
    B, L, D = x.shape
    assert B == SUBLANES, "the SSM kernel puts the streams of one time step on the 8 sublanes"
    lam_init = 0.8 - 0.6 * math.exp(-0.3 * l)
    tm = _pick(L, (512, 256, 128, 64, 32))
    n_keep = min(BAND, L)
    assert n_keep == tm, "the band cache rows must be exactly the last token tile"
    row = lambda a: a.reshape(1, -1).astype(F32)
    prompt = cache is None

    w_in = w['w_in'].astype(BF16)
    dk, dv = dkv if dkv is not None else (None, None)
    p = _norm_proj(x, row(w['norm1_g']), w_in[:, :N_QKVU], tm, prompt, l, depth, dk, dv)

    lam_vecs = jnp.stack([w['lam_q1'], w['lam_k1'], w['lam_q2'], w['lam_k2']]).astype(F32)
    subln_g = w['subln_g'].astype(F32)
    table = w['rel_bias'].astype(F32)
    if prompt:
        oa = _diff_attn_prompt(p['qa'], p['ka_b'], p['va_b'], lam_vecs, subln_g, lam_init)
        bias_t = _rel_bias(table, N_BAND_TILES, BAND_TILE, BAND_TILE, 0, BAND_TILE, True, True)
        ob = _band_attn_prompt(p['qb'], p['kb_b'], p['vb_b'], bias_t)
        h0r = jnp.zeros((B, N_STATE), F32)
        h0i = jnp.zeros((B, N_STATE), F32)
    else:
        ck_a, cv_a, ck_b, cv_b, h0_re, h0_im = cache
        P = ck_a.shape[1]
        bp = ck_b.shape[1]
        oa = _diff_attn_sample(p['qa'], ck_a.reshape(B, P, W_A).astype(BF16), cv_a.reshape(B, P, W_A).astype(BF16),
                               p['ka_b'], p['va_b'], lam_vecs, subln_g, lam_init)
        bias_c = _rel_bias(table, 1, L, bp, bp, 0, False, False)
        bias_n = _rel_bias(table, 1, L, L, 0, 0, False, False)
        ob = _band_attn_sample(p['qb'], ck_b.reshape(B, bp, W_B).astype(BF16), cv_b.reshape(B, bp, W_B).astype(BF16),
                               p['kb_b'], p['vb_b'], bias_c, bias_n)
        h0r = h0_re.reshape(B, N_STATE).astype(F32)
        h0i = h0_im.reshape(B, N_STATE).astype(F32)

    sw = _ssm_weights(w)
    oc_tm, hr, hi = _ssm(p['u_tm'], sw['wxr'], sw['wxi'], sw['ar'], sw['ai'], h0r, h0i,
                         sw['wcr'], sw['wci'], row(w['ssm_d']), w['w_glu'].astype(BF16), _pick(L, (64, 32)))

    x = _merge_out(x, oa, ob, oc_tm, row(w['norm1_g']), w_in[:, N_QKVU:],
                   w['w_br_a'].astype(BF16), w['w_br_b'].astype(BF16), w['w_br_c'].astype(BF16),
                   w['w_out'].astype(BF16), tm)

    fg = row(final_g)
    if l % 2 == 0:
        x = _ffn(x, row(w['norm2_g']), w['ffn_w1'].astype(BF16), w['ffn_w3'].astype(BF16),
                 w['ffn_w2'].astype(BF16), fg, last, tm)
    else:
        rw = jnp.pad(w['router_w'].astype(F32), ((0, 0), (0, LANES - N_EXPERTS)))
        x = _moe(x, row(w['norm2_g']), rw, w['moe_w1'].astype(BF16),
                 w['moe_w3'].astype(BF16), w['moe_w2'].astype(BF16), fg, last)

    small = (p['kb_last'].reshape(B, n_keep, HB, DH_B), p['vb_last'].reshape(B, n_keep, HB, DH_B),
             hr.reshape(B, N_GROUPS, P_STATE), hi.reshape(B, N_GROUPS, P_STATE))
    return x, (p['dk'], p['dv']), small


_PER_LAYER = ('norm1_g', 'w_in', 'lam_q1', 'lam_k1', 'lam_q2', 'lam_k2', 'subln_g', 'rel_bias',
              'ssm_a_re', 'ssm_a_im', 'ssm_log_dt', 'ssm_b_re', 'ssm_b_im', 'ssm_c_re', 'ssm_c_im',
              'ssm_d', 'w_glu', 'w_br_a', 'w_br_b', 'w_br_c', 'w_out', 'norm2_g')


def kernel(x_prompt, x_sample, cache_dk, cache_dv, cache_bk, cache_bv, state_ssm_re, state_ssm_im,
           norm1_g, w_in, lam_q1, lam_k1, lam_q2, lam_k2, subln_g, rel_bias, ssm_a_re, ssm_a_im,
           ssm_log_dt, ssm_b_re, ssm_b_im, ssm_c_re, ssm_c_im, ssm_d, w_glu, w_br_a, w_br_b, w_br_c,
           w_out, norm2_g, ffn_w1, ffn_w3, ffn_w2, router_w, moe_w1, moe_w3, moe_w2, final_g):
    stacked = dict(norm1_g=norm1_g, w_in=w_in, lam_q1=lam_q1, lam_k1=lam_k1, lam_q2=lam_q2, lam_k2=lam_k2,
                   subln_g=subln_g, rel_bias=rel_bias, ssm_a_re=ssm_a_re, ssm_a_im=ssm_a_im,
                   ssm_log_dt=ssm_log_dt, ssm_b_re=ssm_b_re, ssm_b_im=ssm_b_im, ssm_c_re=ssm_c_re,
                   ssm_c_im=ssm_c_im, ssm_d=ssm_d, w_glu=w_glu, w_br_a=w_br_a, w_br_b=w_br_b,
                   w_br_c=w_br_c, w_out=w_out, norm2_g=norm2_g)
    depth = w_in.shape[0]
    yp, ys = x_prompt, x_sample
    dkv_p, dkv_s = None, None
    small_p, small_s = [], []
    for l in range(depth):
        w = {name: stacked[name][l] for name in _PER_LAYER}
        if l % 2 == 0:
            w['ffn_w1'], w['ffn_w3'], w['ffn_w2'] = ffn_w1[l // 2], ffn_w3[l // 2], ffn_w2[l // 2]
        else:
            w['router_w'], w['moe_w1'] = router_w[l // 2], moe_w1[l // 2]
            w['moe_w3'], w['moe_w2'] = moe_w3[l // 2], moe_w2[l // 2]
        last = l == depth - 1
        yp, dkv_p, sp = _layer(yp, l, w, None, final_g, last, dkv_p, depth)
        ys, dkv_s, ss = _layer(ys, l, w, (cache_dk[l], cache_dv[l], cache_bk[l], cache_bv[l],
                                          state_ssm_re[l], state_ssm_im[l]), final_g, last, dkv_s, depth)
        small_p.append(sp)
        small_s.append(ss)
    outs = [yp, ys]
    for dkv, small in ((dkv_p, small_p), (dkv_s, small_s)):
        outs += list(dkv)
        for j in range(4):
            outs.append(jnp.stack([s[j] for s in small]))
    return tuple(outs)
```

```python
import functools
import math

import jax
import jax.numpy as jnp
from jax import lax
from jax.experimental import pallas as pl
from jax.experimental.pallas import tpu as pltpu

F32 = jnp.float32
BF16 = jnp.bfloat16

D_MODEL = 1024
CHUNK = 64
HA = 4
DH_A = 64
DV_A = 2 * DH_A
W_A = HA * DV_A
HB = 8
DH_B = 64
W_B = HB * DH_B
BAND_CHUNKS = 8
BAND = BAND_CHUNKS * CHUNK
REL_CLIP = 128
N_REL = 2 * REL_CLIP + 1
GC = 16
N_GROUPS = 32
W_C = N_GROUPS * GC
P_STATE = 64
N_STATE = N_GROUPS * P_STATE
D_FF = 11 * D_MODEL // 4
N_EXPERTS = 8
D_FF_E = D_FF // 2
EPS = 1e-6
NEG = -1e30
LOG2E = math.log2(math.e)
N_QKVU = 7 * 512

LANES = 128
SUBLANES = 8
MXU_DIM = 256
VMEM_LIMIT = 56 * 1024 * 1024


def _cparams(*sem):
    return pltpu.CompilerParams(dimension_semantics=sem, vmem_limit_bytes=VMEM_LIMIT)


def _rms(x, g):
    return x * lax.rsqrt(jnp.mean(x * x, axis=-1, keepdims=True) + EPS) * g


def _sigmoid(x):
    return 1.0 / (1.0 + jnp.exp(-x))


def _dot(a, b):
    return jnp.dot(a, b, preferred_element_type=F32)


def _dot_nt(a, b):
    return lax.dot_general(a, b, (((1,), (1,)), ((), ())), preferred_element_type=F32)


def _norm_proj_kernel(*refs, transpose_v, n_alias, layer):
    x_ref, g_ref, w_ref = refs[:3]
    (qa_ref, dk_ref, dv_ref, kab_ref, vab_ref,
     qb_ref, kbl_ref, vbl_ref, kbb_ref, vbb_ref, u_ref) = refs[3 + n_alias:]
    if n_alias == 0:
        for slot in range(dk_ref.shape[0]):
            if slot != layer:
                dk_ref[slot] = jnp.zeros(dk_ref.shape[1:], F32)
                dv_ref[slot] = jnp.zeros(dv_ref.shape[1:], F32)
        dk_ref, dv_ref = dk_ref.at[layer], dv_ref.at[layer]
    h = _rms(x_ref[...], g_ref[...]).astype(BF16)

    def proj(c):
        return _dot(h, w_ref[:, c * 512:(c + 1) * 512])

    def store_heads(o_ref, y):
        for hd in range(HA):
            o_ref[:, hd, :] = y[:, hd * DV_A:(hd + 1) * DV_A]

    qa_ref[...] = (proj(0) * (DH_A ** -0.5 * LOG2E)).astype(BF16)
    ka = proj(1)
    store_heads(dk_ref, ka)
    kab_ref[...] = ka.astype(BF16)
    va = proj(2)
    store_heads(dv_ref, va)
    vab_ref[...] = (va.T if transpose_v else va).astype(BF16)
    qb_ref[...] = (proj(3) * (DH_B ** -0.5 * LOG2E)).astype(BF16)
    kb = proj(4)
    kbl_ref[...] = kb
    kbb_ref[...] = kb.astype(BF16)
    vb = proj(5)
    vbl_ref[...] = vb
    vbb_ref[...] = (vb.T if transpose_v else vb).astype(BF16)
    u_ref[...] = proj(6)


def _norm_proj(x, g, w_qkvu, tm, transpose_v, layer, depth, dk, dv):
    B, L, D = x.shape
    nt = L // tm
    tok = jax.ShapeDtypeStruct((B, L, 512), BF16)
    tspec = pl.BlockSpec((None, tm, 512), lambda b, t: (b, t, 0))
    vshape, vspec = tok, tspec
    if transpose_v:
        vshape = jax.ShapeDtypeStruct((B, nt, 512, tm), BF16)
        vspec = pl.BlockSpec((None, None, 512, tm), lambda b, t: (b, t, 0, 0))
    cache = jax.ShapeDtypeStruct((depth, B, L, HA, DV_A), F32)
    cspec = pl.BlockSpec((None, None, tm, HA, DV_A), lambda b, t: (layer, b, t, 0, 0))
    last = jax.ShapeDtypeStruct((B, tm, 512), F32)
    lspec = pl.BlockSpec((None, tm, 512), lambda b, t: (b, 0, 0))
    out_shape = [tok, cache, cache, tok, vshape, tok, last, last, tok, vshape,
                 jax.ShapeDtypeStruct((L, B * 512), F32)]
    out_specs = [tspec, cspec, cspec, tspec, vspec, tspec, lspec, lspec, tspec, vspec,
                 pl.BlockSpec((tm, 512), lambda b, t: (t, b))]
    in_specs = [pl.BlockSpec((None, tm, D), lambda b, t: (b, t, 0)),
                pl.BlockSpec((1, D), lambda b, t: (0, 0)),
                pl.BlockSpec((D, N_QKVU), lambda b, t: (0, 0))]
    args = [x, g, w_qkvu]
    aliases = {}
    if dk is None:
        whole = pl.BlockSpec((depth, None, tm, HA, DV_A), lambda b, t: (0, b, t, 0, 0))
        out_specs[1] = out_specs[2] = whole
    else:
        in_specs += [pl.BlockSpec(memory_space=pl.ANY)] * 2
        args += [dk, dv]
        aliases = {3: 1, 4: 2}
    outs = pl.pallas_call(
        functools.partial(_norm_proj_kernel, transpose_v=transpose_v, n_alias=len(aliases), layer=layer),
        grid=(B, nt),
        in_specs=in_specs,
        out_specs=out_specs,
        out_shape=out_shape,
        input_output_aliases=aliases,
        compiler_params=_cparams("parallel", "arbitrary"),
        name="norm_proj",
    )(*args)
    names = ('qa', 'dk', 'dv', 'ka_b', 'va_b', 'qb', 'kb_last', 'vb_last', 'kb_b', 'vb_b', 'u_tm')
    return dict(zip(names, outs))


def _lam_value(lam_ref, lam_init):
    lv = lam_ref[...]
    e1 = jnp.exp(jnp.sum(lv[0:1, :] * lv[1:2, :], axis=-1, keepdims=True))
    e2 = jnp.exp(jnp.sum(lv[2:3, :] * lv[3:4, :], axis=-1, keepdims=True))
    return e1 - e2 + lam_init


def _subln(o, g, lam_init):
    return _rms(o, g) * (1.0 - lam_init)


def _split_halves(q):
    lane = lax.broadcasted_iota(jnp.int32, q.shape, 1)
    zero = jnp.zeros_like(q)
    return jnp.where(lane < 64, q, zero), jnp.where(lane >= 64, q, zero)


def _diff_attn_kernel(lam_ref, q_ref, k_ref, vt_ref, g_ref, o_ref,
                      m1_s, l1_s, a1_s, m2_s, l2_s, a2_s, sa1_s, sa2_s, sb1_s, sb2_s,
                      xa1_s, xa2_s, xb1_s, xb2_s, *, t, lam_init):
    qi = pl.program_id(2)
    nt = pl.num_programs(2)

    def q_halves(i):
        return _split_halves(q_ref[pl.ds(pl.multiple_of(i * t, t), t), :])

    q_cur = q_halves(qi)
    q_nxt = q_halves(jnp.minimum(qi + 1, nt - 1))

    m1_s[...] = jnp.full_like(m1_s, -jnp.inf)
    m2_s[...] = jnp.full_like(m2_s, -jnp.inf)
    l1_s[...] = jnp.zeros_like(l1_s)
    l2_s[...] = jnp.zeros_like(l2_s)
    a1_s[...] = jnp.zeros_like(a1_s)
    a2_s[...] = jnp.zeros_like(a2_s)

    def update(st, mx, vt, m_s, l_s, a_s):
        m_prev = m_s[...]
        m_new = jnp.maximum(m_prev, mx)
        alpha = jnp.exp2(m_prev - m_new)
        p = jnp.exp2(st - m_new)
        l_s[...] = alpha * l_s[...] + jnp.sum(p, axis=0, keepdims=True)
        a_s[...] = alpha * a_s[...] + _dot(vt, p.astype(BF16))
        m_s[...] = m_new

    buf_a, buf_b = (sa1_s, sa2_s, xa1_s, xa2_s), (sb1_s, sb2_s, xb1_s, xb2_s)

    def produce(kj, buf, qq):
        k = k_ref[pl.ds(pl.multiple_of(kj * t, t), t), :]
        for q, s_ref, x_ref in ((qq[0], buf[0], buf[2]), (qq[1], buf[1], buf[3])):
            st = _dot_nt(k, q)
            s_ref[...] = st
            x_ref[...] = jnp.max(st, axis=0, keepdims=True)

    def consume(kj, buf, masked):
        vt = vt_ref[kj]
        s1, s2 = buf[0][...], buf[1][...]
        if masked:
            kc = lax.broadcasted_iota(jnp.int32, (t, t), 0) // CHUNK
            qc = lax.broadcasted_iota(jnp.int32, (t, t), 1) // CHUNK
            keep = kc <= qc
            s1 = jnp.where(keep, s1, NEG)
            s2 = jnp.where(keep, s2, NEG)
            mx1 = jnp.max(s1, axis=0, keepdims=True)
            mx2 = jnp.max(s2, axis=0, keepdims=True)
        else:
            mx1, mx2 = buf[2][...], buf[3][...]
        update(s1, mx1, vt, m1_s, l1_s, a1_s)
        update(s2, mx2, vt, m2_s, l2_s, a2_s)

    @pl.when(qi == 0)
    def _():
        produce(0, buf_a, q_cur)

    def finish():
        lam = _lam_value(lam_ref, lam_init)
        ot = a1_s[...] / l1_s[...] - lam * (a2_s[...] / l2_s[...])
        ot = ot * lax.rsqrt(jnp.mean(ot * ot, axis=0, keepdims=True) + EPS) * g_ref[...] * (1.0 - lam_init)
        o_ref[...] = ot.T.astype(BF16)

    def run(first, second, rem):
        def pair(kj):
            produce(kj + 1, second, q_cur)
            consume(kj, first, False)
            produce(kj + 2, first, q_cur)
            consume(kj + 1, second, False)

        def quad(m, c):
            pair(4 * m)
            pair(4 * m + 2)
            return c

        lax.fori_loop(0, qi // 4, quad, 0)
        if rem >= 2:
            pair((qi // 4) * 4)
        odd = rem % 2 == 1
        if odd:
            produce(qi, second, q_cur)
            consume(qi - 1, first, False)
            produce(0, first, q_nxt)
            consume(qi, second, True)
        else:
            consume(qi, first, True)
            produce(0, second, q_nxt)
        finish()

    for r, (first, second) in enumerate(((buf_a, buf_b), (buf_b, buf_a), (buf_b, buf_a), (buf_a, buf_b))):
        pl.when(qi % 4 == r)(functools.partial(run, first, second, r))


def _diff_attn_prompt(q, k, vt, lam_vecs, subln_g, lam_init):
    B, L, _ = q.shape
    nt, t = vt.shape[1], vt.shape[3]
    kern = functools.partial(_diff_attn_kernel, t=t, lam_init=lam_init)
    return pl.pallas_call(
        kern,
        grid=(B, HA, nt),
        in_specs=[pl.BlockSpec((4, DH_A), lambda b, h, i: (0, 0)),
                  pl.BlockSpec((None, L, DV_A), lambda b, h, i: (b, 0, h)),
                  pl.BlockSpec((None, L, DV_A), lambda b, h, i: (b, 0, h)),
                  pl.BlockSpec((None, nt, DV_A, t), lambda b, h, i: (b, 0, h, 0)),
                  pl.BlockSpec((None, DV_A, 1), lambda b, h, i: (h, 0, 0))],
        out_specs=pl.BlockSpec((None, t, DV_A), lambda b, h, i: (b, i, h)),
        out_shape=jax.ShapeDtypeStruct((B, L, W_A), BF16),
        scratch_shapes=[pltpu.VMEM((1, t), F32), pltpu.VMEM((1, t), F32), pltpu.VMEM((DV_A, t), F32),
                        pltpu.VMEM((1, t), F32), pltpu.VMEM((1, t), F32), pltpu.VMEM((DV_A, t), F32)]
                       + [pltpu.VMEM((t, t), F32)] * 4 + [pltpu.VMEM((1, t), F32)] * 4,
        compiler_params=_cparams("parallel", "parallel", "arbitrary"),
        name="diff_attn",
    )(lam_vecs, q, k, vt, subln_g.reshape(HA, DV_A, 1))


def _diff_attn_sample_kernel(lam_ref, q_ref, ck_ref, cv_ref, k_ref, v_ref, g_ref, o_ref, *, lam_init):
    lam = _lam_value(lam_ref, lam_init)
    for h in range(HA):
        cols = slice(h * DV_A, (h + 1) * DV_A)
        q1, q2 = _split_halves(q_ref[:, cols])
        ck, cv = ck_ref[:, cols], cv_ref[:, cols]
        k, v = k_ref[:, cols], v_ref[:, cols]

        def one_map(qm):
            sc = _dot_nt(qm, ck)
            sn = _dot_nt(qm, k)
            m = jnp.maximum(jnp.max(sc, axis=-1, keepdims=True), jnp.max(sn, axis=-1, keepdims=True))
            pc = jnp.exp2(sc - m)
            pn = jnp.exp2(sn - m)
            l = jnp.sum(pc, axis=-1, keepdims=True) + jnp.sum(pn, axis=-1, keepdims=True)
            return (_dot(pc.astype(BF16), cv) + _dot(pn.astype(BF16), v)) / l

        o = one_map(q1) - lam * one_map(q2)
        o_ref[:, cols] = _subln(o, g_ref[h], lam_init).astype(BF16)


def _diff_attn_sample(q, ck, cv, k, v, lam_vecs, subln_g, lam_init):
    B, T, _ = q.shape
    P = ck.shape[1]
    kern = functools.partial(_diff_attn_sample_kernel, lam_init=lam_init)
    new = pl.BlockSpec((None, T, W_A), lambda b: (b, 0, 0))
    old = pl.BlockSpec((None, P, W_A), lambda b: (b, 0, 0))
    return pl.pallas_call(
        kern,
        grid=(B,),
        in_specs=[pl.BlockSpec((4, DH_A), lambda b: (0, 0)), new, old, old, new, new,
                  pl.BlockSpec((HA, 1, DV_A), lambda b: (0, 0, 0))],
        out_specs=new,
        out_shape=jax.ShapeDtypeStruct((B, T, W_A), BF16),
        compiler_params=_cparams("parallel"),
        name="diff_attn_sample",
    )(lam_vecs, q, ck, cv, k, v, subln_g.reshape(HA, 1, DV_A))


def _rel_bias_kernel(tab_ref, o_ref, *, rows, cols, off0, off_step, masked, transposed):
    h = pl.program_id(0)
    d = pl.program_id(1)
    off = off0 + d * off_step
    strip = min(rows, CHUNK)
    for r0 in range(0, rows, strip):
        r = r0 + lax.broadcasted_iota(jnp.int32, (strip, cols), 0)
        c = lax.broadcasted_iota(jnp.int32, (strip, cols), 1)
        qi, ki = (c, r) if transposed else (r, c)
        q_lo, q_hi = (0, cols - 1) if transposed else (r0, r0 + strip - 1)
        k_lo, k_hi = (r0, r0 + strip - 1) if transposed else (0, cols - 1)
        idx = jnp.clip(off + qi - ki, -REL_CLIP, REL_CLIP) + REL_CLIP
        lo = jnp.clip(off + q_lo - k_hi, -REL_CLIP, REL_CLIP) + REL_CLIP
        hi = jnp.clip(off + q_hi - k_lo, -REL_CLIP, REL_CLIP) + REL_CLIP

        def body(j, acc, idx=idx):
            return jnp.where(idx == j, tab_ref[j * HB + h], acc)

        bias = lax.fori_loop(lo, hi + 1, body, jnp.zeros((strip, cols), F32)) * LOG2E
        if masked:
            dc = d * (off_step // CHUNK) + qi // CHUNK - ki // CHUNK
            bias = jnp.where((dc >= 0) & (dc <= BAND_CHUNKS), bias, NEG)
        o_ref[r0:r0 + strip, :] = bias


def _rel_bias(table, n_off, rows, cols, off0, off_step, masked, transposed):
    kern = functools.partial(_rel_bias_kernel, rows=rows, cols=cols, off0=off0,
                             off_step=off_step, masked=masked, transposed=transposed)
    return pl.pallas_call(
        kern,
        grid=(HB, n_off),
        in_specs=[pl.BlockSpec(memory_space=pltpu.SMEM)],
        out_specs=pl.BlockSpec((None, None, rows, cols), lambda h, d: (h, d, 0, 0)),
        out_shape=jax.ShapeDtypeStruct((HB, n_off, rows, cols), F32),
        compiler_params=_cparams("parallel", "parallel"),
        name="rel_bias",
    )(table.reshape(N_REL * HB))


def _pair_softmax_out(q, ks, vs, bias_fn):
    lane = lax.broadcasted_iota(jnp.int32, (q.shape[0], LANES), 1)
    outs = []
    for hh, qm in enumerate(_split_halves(q)):
        ss = [_dot_nt(qm, k) + bias_fn(hh, j) for j, k in enumerate(ks)]
        m = functools.reduce(jnp.maximum, [jnp.max(s, axis=-1, keepdims=True) for s in ss])
        ps = [jnp.exp2(s - m) for s in ss]
        l = functools.reduce(jnp.add, [jnp.sum(p, axis=-1, keepdims=True) for p in ps])
        o = functools.reduce(jnp.add, [_dot(p.astype(BF16), v) for p, v in zip(ps, vs)])
        outs.append(o / l)
    return jnp.where(lane < 64, outs[0], outs[1])


BAND_TILE = 4 * CHUNK
N_BAND_TILES = BAND // BAND_TILE + 1


def _band_attn_kernel(q_ref, k0_ref, k1_ref, k2_ref, v0_ref, v1_ref, v2_ref, b_ref, o_ref, sa_s, sb_s):
    i = pl.program_id(1)
    k_refs = (k0_ref, k1_ref, k2_ref)
    vt_refs = (v0_ref, v1_ref, v2_ref)
    row = lax.broadcasted_iota(jnp.int32, (LANES, BAND_TILE), 0)

    def produce(h, buf, guarded):
        cols = slice((h // 2) * LANES, (h // 2 + 1) * LANES)
        qm = _split_halves(q_ref[:, cols])[h % 2]
        for d, k_ref in enumerate(k_refs):
            s = _dot_nt(k_ref[:, cols], qm) + b_ref[h, d]
            if guarded and d > 0:
                s = jnp.where(i >= d, s, NEG)
            buf[d] = s

    def consume(h, buf):
        cols = slice((h // 2) * LANES, (h // 2 + 1) * LANES)
        ss = [buf[d] for d in range(N_BAND_TILES)]
        m = functools.reduce(jnp.maximum, [jnp.max(s, axis=0, keepdims=True) for s in ss])
        ps = [jnp.exp2(s - m) for s in ss]
        l = functools.reduce(jnp.add, [jnp.sum(p, axis=0, keepdims=True) for p in ps])
        ot = functools.reduce(jnp.add, [_dot(r[cols, :], p.astype(BF16)) for r, p in zip(vt_refs, ps)])
        return ot / l

    def all_heads(guarded):
        bufs = (sa_s, sb_s)
        produce(0, bufs[0], guarded)
        prev = None
        for h in range(HB):
            if h + 1 < HB:
                produce(h + 1, bufs[(h + 1) % 2], guarded)
            ot = consume(h, bufs[h % 2])
            if h % 2 == 1:
                cols = slice((h // 2) * LANES, (h // 2 + 1) * LANES)
                o_ref[:, cols] = jnp.where(row < DH_B, prev, ot).T.astype(BF16)
            prev = ot

    @pl.when(i >= N_BAND_TILES - 1)
    def _():
        all_heads(False)

    @pl.when(i < N_BAND_TILES - 1)
    def _():
        all_heads(True)


def _band_attn_prompt(q, k, vt, bias_t):
    B, L, _ = q.shape
    t = BAND_TILE
    per = vt.shape[3] // t
    qspec = pl.BlockSpec((None, t, W_B), lambda b, i: (b, i, 0))
    kspec = lambda d: pl.BlockSpec((None, t, W_B), lambda b, i: (b, jnp.maximum(i - d, 0), 0))
    vspec = lambda d: pl.BlockSpec(
        (None, None, W_B, t),
        lambda b, i: (b, jnp.maximum(i - d, 0) // per, 0, jnp.maximum(i - d, 0) % per))
    return pl.pallas_call(
        _band_attn_kernel,
        grid=(B, L // t),
        in_specs=[qspec, kspec(0), kspec(1), kspec(2), vspec(0), vspec(1), vspec(2),
                  pl.BlockSpec((HB, N_BAND_TILES, t, t), lambda b, i: (0, 0, 0, 0),
                               pipeline_mode=pl.Buffered(1))],
        out_specs=qspec,
        out_shape=jax.ShapeDtypeStruct((B, L, W_B), BF16),
        scratch_shapes=[pltpu.VMEM((N_BAND_TILES, t, t), F32)] * 2,
        compiler_params=_cparams("parallel", "parallel"),
        name="band_attn",
    )(q, k, k, k, vt, vt, vt, bias_t)


def _band_attn_sample_kernel(q_ref, ck_ref, cv_ref, k_ref, v_ref, bc_ref, bn_ref, o_ref):
    for hp in range(HB // 2):
        cols = slice(hp * LANES, (hp + 1) * LANES)
        ks = [ck_ref[:, cols], k_ref[:, cols]]
        vs = [cv_ref[:, cols], v_ref[:, cols]]

        def bias_fn(hh, j, hp=hp):
            return (bc_ref if j == 0 else bn_ref)[2 * hp + hh, 0]

        o_ref[:, cols] = _pair_softmax_out(q_ref[:, cols], ks, vs, bias_fn).astype(BF16)


def _band_attn_sample(q, ck, cv, k, v, bias_c, bias_n):
    B, T, _ = q.shape
    P = ck.shape[1]
    new = pl.BlockSpec((None, T, W_B), lambda b: (b, 0, 0))
    old = pl.BlockSpec((None, P, W_B), lambda b: (b, 0, 0))
    return pl.pallas_call(
        _band_attn_sample_kernel,
        grid=(B,),
        in_specs=[new, old, old, new, new,
                  pl.BlockSpec((HB, 1, T, P), lambda b: (0, 0, 0, 0)),
                  pl.BlockSpec((HB, 1, T, T), lambda b: (0, 0, 0, 0))],
        out_specs=new,
        out_shape=jax.ShapeDtypeStruct((B, T, W_B), BF16),
        compiler_params=_cparams("parallel"),
        name="band_attn_sample",
    )(q, ck, cv, k, v, bias_c, bias_n)


N_SLAB = N_STATE // MXU_DIM


def _ssm_kernel(u_ref, wxr_ref, wxi_ref, ar_ref, ai_ref, h0r_ref, h0i_ref, wcr_ref, wci_ref,
                d_ref, wglu_ref, oc_ref, hro_ref, hio_ref, xr_s, xi_s, hr_s, hi_s, io_s, *, tt, nb, half):
    i = pl.program_id(0)

    @pl.when(i == 0)
    def _():
        hr_s[...] = h0r_ref[...]
        hi_s[...] = h0i_ref[...]

    n_ch = W_C // LANES
    for b in range(nb):
        for s in range(n_ch):
            c0 = b * W_C + s * LANES
            io_s[s, pl.ds(b, tt, stride=nb), :] = u_ref[:, c0:c0 + LANES]
    u = jnp.concatenate([io_s[s] for s in range(n_ch)], axis=-1)
    ub = u.astype(BF16)
    for j in range(N_SLAB):
        us = ub[:, LANES * (j // 2):LANES * (j // 2 + 1)]
        xr_s[:, MXU_DIM * j:MXU_DIM * (j + 1)] = _dot(us, wxr_ref[j])
        xi_s[:, MXU_DIM * j:MXU_DIM * (j + 1)] = _dot(us, wxi_ref[j])

    for c in range(N_STATE // half):
        cols = slice(c * half, (c + 1) * half)
        ar = jnp.broadcast_to(ar_ref[:, cols], (nb, half))
        ai = jnp.broadcast_to(ai_ref[:, cols], (nb, half))

        def step(t, carry, cols=cols, ar=ar, ai=ai):
            hr, hi = carry
            r0 = pl.multiple_of(t * nb, nb)
            nhr = ar * hr - ai * hi + xr_s[pl.ds(r0, nb), cols]
            nhi = ar * hi + ai * hr + xi_s[pl.ds(r0, nb), cols]
            xr_s[pl.ds(r0, nb), cols] = nhr
            xi_s[pl.ds(r0, nb), cols] = nhi
            return nhr, nhi

        hr, hi = lax.fori_loop(0, tt, step, (hr_s[:, cols], hi_s[:, cols]), unroll=4)
        hr_s[:, cols] = hr
        hi_s[:, cols] = hi

    hro_ref[...] = hr_s[...]
    hio_ref[...] = hi_s[...]

    ys = []
    for s in range(W_C // LANES):
        acc = None
        for j in (2 * s, 2 * s + 1):
            hrb = xr_s[:, MXU_DIM * j:MXU_DIM * (j + 1)].astype(BF16)
            hib = xi_s[:, MXU_DIM * j:MXU_DIM * (j + 1)].astype(BF16)
            part = _dot(hrb, wcr_ref[j]) + _dot(hib, wci_ref[j])
            acc = part if acc is None else acc + part
        ys.append(acc)
    y = jnp.concatenate(ys, axis=-1) + d_ref[...] * u
    ge = 0.5 * y * (1.0 + jnp.tanh(math.sqrt(2.0 / math.pi) * (y + 0.044715 * (y * y * y))))
    gl = _dot(ge.astype(BF16), wglu_ref[...])
    oc = gl[:, :W_C] * _sigmoid(gl[:, W_C:])
    for s in range(n_ch):
        io_s[s] = oc[:, s * LANES:(s + 1) * LANES]
    for b in range(nb):
        for s in range(n_ch):
            c0 = b * W_C + s * LANES
            oc_ref[:, c0:c0 + LANES] = io_s[s, pl.ds(b, tt, stride=nb), :].astype(BF16)


def _ssm(u_tm, wxr, wxi, ar, ai, h0r, h0i, wcr, wci, d, wglu, tt):
    L = u_tm.shape[0]
    nb = h0r.shape[0]
    rows = tt * nb
    half = N_STATE // 2
    kern = functools.partial(_ssm_kernel, tt=tt, nb=nb, half=half)
    full = lambda shape: pl.BlockSpec(shape, lambda i: (0,) * len(shape))
    return pl.pallas_call(
        kern,
        grid=(L // tt,),
        in_specs=[pl.BlockSpec((tt, nb * W_C), lambda i: (i, 0)),
                  full((N_SLAB, LANES, MXU_DIM)), full((N_SLAB, LANES, MXU_DIM)),
                  full((1, N_STATE)), full((1, N_STATE)),
                  full((nb, N_STATE)), full((nb, N_STATE)),
                  full((N_SLAB, MXU_DIM, LANES)), full((N_SLAB, MXU_DIM, LANES)),
                  full((1, W_C)), full((W_C, 2 * W_C))],
        out_specs=[pl.BlockSpec((tt, nb * W_C), lambda i: (i, 0)),
                   full((nb, N_STATE)), full((nb, N_STATE))],
        out_shape=[jax.ShapeDtypeStruct((L, nb * W_C), BF16),
                   jax.ShapeDtypeStruct((nb, N_STATE), F32),
                   jax.ShapeDtypeStruct((nb, N_STATE), F32)],
        scratch_shapes=[pltpu.VMEM((rows, N_STATE), F32), pltpu.VMEM((rows, N_STATE), F32),
                        pltpu.VMEM((nb, N_STATE), F32), pltpu.VMEM((nb, N_STATE), F32),
                        pltpu.VMEM((W_C // LANES, rows, LANES), F32)],
        compiler_params=_cparams("arbitrary"),
        name="ssm",
    )(u_tm, wxr, wxi, ar, ai, h0r, h0i, wcr, wci, d, wglu)


def _ssm_weights(w):
    ar, ai = w['ssm_a_re'].astype(F32), w['ssm_a_im'].astype(F32)
    dt = jnp.exp(w['ssm_log_dt'].astype(F32))[:, None]
    mag = jnp.exp(ar * dt)
    abar_r, abar_i = mag * jnp.cos(ai * dt), mag * jnp.sin(ai * dt)
    den = ar * ar + ai * ai
    nr, ni = abar_r - 1.0, abar_i
    z_r = (nr * ar + ni * ai) / den
    z_i = (ni * ar - nr * ai) / den
    br, bi = w['ssm_b_re'].astype(F32), w['ssm_b_im'].astype(F32)
    bb_r = z_r[..., None] * br - z_i[..., None] * bi
    bb_i = z_r[..., None] * bi + z_i[..., None] * br

    gps = MXU_DIM // P_STATE
    eye = jnp.eye(gps, dtype=F32)

    def x_slabs(bb):
        b4 = bb.reshape(N_SLAB, gps, P_STATE, GC)
        blk = jnp.einsum('jgpc,gh->jhcgp', b4, eye).reshape(N_SLAB, gps * GC, MXU_DIM)
        zero = jnp.zeros_like(blk)
        even = jnp.concatenate([blk, zero], axis=1)
        odd = jnp.concatenate([zero, blk], axis=1)
        sel = (jnp.arange(N_SLAB) % 2 == 0)[:, None, None]
        return jnp.where(sel, even, odd).astype(BF16)

    def y_slabs(cc):
        c4 = cc.reshape(N_SLAB, gps, GC, P_STATE)
        blk = jnp.einsum('jgcp,gh->jgphc', c4, eye).reshape(N_SLAB, MXU_DIM, gps * GC)
        zero = jnp.zeros_like(blk)
        even = jnp.concatenate([blk, zero], axis=2)
        odd = jnp.concatenate([zero, blk], axis=2)
        sel = (jnp.arange(N_SLAB) % 2 == 0)[:, None, None]
        return jnp.where(sel, even, odd).astype(BF16)

    return dict(wxr=x_slabs(bb_r), wxi=x_slabs(bb_i),
                ar=abar_r.reshape(1, N_STATE), ai=abar_i.reshape(1, N_STATE),
                wcr=y_slabs(w['ssm_c_re'].astype(F32)), wci=y_slabs(-w['ssm_c_im'].astype(F32)))


def _merge_out_kernel(x_ref, oa_ref, ob_ref, oc_ref, g_ref, wg_ref, wa_ref, wb_ref, wc_ref, wo_ref, o_ref):
    x = x_ref[...]
    h = _rms(x, g_ref[...]).astype(BF16)
    merged = None
    for j, (br_ref, w_ref) in enumerate(((oa_ref, wa_ref), (ob_ref, wb_ref), (oc_ref, wc_ref))):
        gate = _sigmoid(_dot(h, wg_ref[:, j * D_MODEL:(j + 1) * D_MODEL]))
        term = gate * _dot(br_ref[...], w_ref[...])
        merged = term if merged is None else merged + term
    o_ref[...] = x + _dot(merged.astype(BF16), wo_ref[...])


def _merge_out(x, oa, ob, oc_tm, g, wg, wa, wb, wc, wo, tm):
    B, L, D = x.shape
    full = lambda shape: pl.BlockSpec(shape, lambda b, t: (0,) * len(shape))
    br = pl.BlockSpec((None, tm, 512), lambda b, t: (b, t, 0))
    return pl.pallas_call(
        _merge_out_kernel,
        grid=(B, L // tm),
        in_specs=[pl.BlockSpec((None, tm, D), lambda b, t: (b, t, 0)), br, br,
                  pl.BlockSpec((tm, 512), lambda b, t: (t, b)),
                  full((1, D)), full((D, 3 * D)), full((W_A, D)), full((W_B, D)), full((W_C, D)), full((D, D))],
        out_specs=pl.BlockSpec((None, tm, D), lambda b, t: (b, t, 0)),
        out_shape=jax.ShapeDtypeStruct((B, L, D), F32),
        compiler_params=_cparams("parallel", "parallel"),
        name="merge_out",
    )(x, oa, ob, oc_tm, g, wg, wa, wb, wc, wo)


FF_CHUNK = 2 * MXU_DIM


def _swiglu(h, w1_ref, w3_ref, w2_ref):
    n_ff = w1_ref.shape[1]
    acc = None
    for c0 in range(0, n_ff, FF_CHUNK):
        c1 = min(c0 + FF_CHUNK, n_ff)
        a = _dot(h, w1_ref[:, c0:c1])
        b = _dot(h, w3_ref[:, c0:c1])
        part = _dot((a * _sigmoid(a) * b).astype(BF16), w2_ref[c0:c1, :])
        acc = part if acc is None else acc + part
    return acc


def _ffn_kernel(x_ref, g_ref, w1_ref, w3_ref, w2_ref, fg_ref, o_ref, *, final_norm):
    x = x_ref[...]
    h = _rms(x, g_ref[...]).astype(BF16)
    y = x + _swiglu(h, w1_ref, w3_ref, w2_ref)
    o_ref[...] = _rms(y, fg_ref[...]) if final_norm else y


def _ffn(x, g, w1, w3, w2, final_g, final_norm, tm):
    B, L, D = x.shape
    full = lambda shape: pl.BlockSpec(shape, lambda b, t: (0,) * len(shape), pipeline_mode=pl.Buffered(1))
    tok = pl.BlockSpec((None, tm, D), lambda b, t: (b, t, 0))
    return pl.pallas_call(
        functools.partial(_ffn_kernel, final_norm=final_norm),
        grid=(B, L // tm),
        in_specs=[tok, full((1, D)), full((D, D_FF)), full((D, D_FF)), full((D_FF, D)), full((1, D))],
        out_specs=tok,
        out_shape=jax.ShapeDtypeStruct((B, L, D), F32),
        compiler_params=_cparams("parallel", "parallel"),
        name="ffn",
    )(x, g, w1, w3, w2, final_g)


R_I1, R_I2, R_R1, R_R2, R_G1, R_G2 = range(6)


def _moe_route_kernel(x_ref, g_ref, rw2_ref, tri_ref, route_ref, routet_ref, cnt_ref, cnt_s):
    @pl.when(pl.program_id(0) == 0)
    def _():
        cnt_s[...] = jnp.zeros_like(cnt_s)

    lane = lax.broadcasted_iota(jnp.int32, route_ref.shape, 1)
    hf = _rms(x_ref[...], g_ref[...])
    h_hi = hf.astype(BF16)
    h_lo = (hf - h_hi.astype(F32)).astype(BF16)
    hi_both = _dot(h_hi, rw2_ref[...])
    logits = hi_both[:, :LANES] + (hi_both[:, LANES:] + _dot(h_lo, rw2_ref[:, :LANES]))
    logits = jnp.where(lane < N_EXPERTS, logits, -jnp.inf)
    v1 = jnp.max(logits, axis=-1, keepdims=True)
    i1 = jnp.min(jnp.where(logits == v1, lane, LANES), axis=-1, keepdims=True)
    rest = jnp.where(lane == i1, -jnp.inf, logits)
    v2 = jnp.max(rest, axis=-1, keepdims=True)
    i2 = jnp.min(jnp.where(rest == v2, lane, LANES), axis=-1, keepdims=True)
    ex = jnp.exp(v2 - v1)
    g1 = 1.0 / (1.0 + ex)
    g2 = ex / (1.0 + ex)

    oh1 = (lane == i1).astype(F32)
    oh2 = (lane == i2).astype(F32)
    oh = oh1 + oh2
    incl = _dot(tri_ref[...], oh.astype(BF16))
    rank = cnt_s[...] + incl - oh
    r1 = jnp.sum(oh1 * rank, axis=-1, keepdims=True)
    r2 = jnp.sum(oh2 * rank, axis=-1, keepdims=True)
    cnt_s[...] += jnp.sum(oh, axis=0, keepdims=True)

    rec = jnp.zeros(route_ref.shape, F32)
    for ln, val in ((R_I1, i1.astype(F32)), (R_I2, i2.astype(F32)), (R_R1, r1), (R_R2, r2),
                    (R_G1, g1), (R_G2, g2)):
        rec = jnp.where(lane == ln, val, rec)
    route_ref[...] = rec
    routet_ref[...] = rec.T[:SUBLANES, :]
    cnt_ref[...] = cnt_s[...]


def _moe_route(x, g, rw, tm):
    n, d = x.shape
    tri = (jnp.arange(tm)[:, None] >= jnp.arange(tm)[None, :]).astype(BF16)
    rw_hi = rw.astype(BF16)
    rw_lo = (rw - rw_hi.astype(F32)).astype(BF16)
    full = lambda shape: pl.BlockSpec(shape, lambda i: (0,) * len(shape))
    return pl.pallas_call(
        _moe_route_kernel,
        grid=(n // tm,),
        in_specs=[pl.BlockSpec((tm, d), lambda i: (i, 0)), full((1, d)), full((d, 2 * LANES)), full((tm, tm))],
        out_specs=[pl.BlockSpec((tm, LANES), lambda i: (i, 0)), pl.BlockSpec((SUBLANES, tm), lambda i: (0, i)),
                   full((1, LANES))],
        out_shape=[jax.ShapeDtypeStruct((n, LANES), F32), jax.ShapeDtypeStruct((SUBLANES, n), F32),
                   jax.ShapeDtypeStruct((1, LANES), F32)],
        scratch_shapes=[pltpu.VMEM((1, LANES), F32)],
        compiler_params=_cparams("arbitrary"),
        name="moe_route",
    )(x, g, jnp.concatenate([rw_hi, rw_lo], axis=1), tri)


def _row_copies(pos_ref, tm, make_copy):
    def body(k, c):
        for j in range(SUBLANES):
            for s in range(2):
                make_copy(s, k, j, pos_ref[0, s * tm + k * SUBLANES + j]).start(priority=s)
        return c

    lax.fori_loop(0, tm // SUBLANES, body, 0)


N_PAD_SEGS = N_EXPERTS + 1


def _moe_dispatch_kernel(pad_ref, pos_ref, x_ref, xs_ref, zrow_s, sem, zsem, *, tm):
    @pl.when(pl.program_id(0) == 0)
    def _():
        zrow_s[...] = jnp.zeros_like(zrow_s)
        zero_row = zrow_s.at[pl.ds(0, 1)]
        total = 0
        for seg in range(N_PAD_SEGS):
            start, count = pad_ref[0, seg], pad_ref[1, seg]

            def start_one(r, c, start=start):
                pltpu.make_async_copy(zero_row, xs_ref.at[pl.ds(start + r, 1)], zsem).start()
                return c

            lax.fori_loop(0, count, start_one, 0)
            total = total + count

        def wait_one(r, c):
            pltpu.make_async_copy(zero_row, xs_ref.at[pl.ds(0, 1)], zsem).wait()
            return c

        lax.fori_loop(0, total, wait_one, 0)

    _row_copies(pos_ref, tm, lambda s, k, j, p: pltpu.make_async_copy(
        x_ref.at[k, pl.ds(j, 1)], xs_ref.at[pl.ds(p, 1)], sem))
    pltpu.make_async_copy(xs_ref.at[pl.ds(0, 2 * tm)], xs_ref.at[pl.ds(0, 2 * tm)], sem).wait()


def _moe_dispatch(x, pos, pad, n_rows, tm):
    n, d = x.shape
    return pl.pallas_call(
        functools.partial(_moe_dispatch_kernel, tm=tm),
        grid=(n // tm,),
        in_specs=[pl.BlockSpec(memory_space=pltpu.SMEM),
                  pl.BlockSpec((None, 1, 2 * tm), lambda i: (i, 0, 0), memory_space=pltpu.SMEM),
                  pl.BlockSpec((tm // SUBLANES, SUBLANES, d), lambda i: (i, 0, 0))],
        out_specs=pl.BlockSpec(memory_space=pl.ANY),
        out_shape=jax.ShapeDtypeStruct((n_rows, d), F32),
        scratch_shapes=[pltpu.VMEM((SUBLANES, d), F32), pltpu.SemaphoreType.DMA(()),
                        pltpu.SemaphoreType.DMA(())],
        compiler_params=_cparams("arbitrary"),
        name="moe_dispatch",
    )(pad, pos, x.reshape(n // SUBLANES, SUBLANES, d))


def _moe_expert_kernel(e_ref, rows_ref, xs_ref, g_ref, w1_ref, w3_ref, w2_ref, ys_ref):
    rows = rows_ref[pl.program_id(0)]

    @pl.when(rows > 0)
    def _():
        h = _rms(xs_ref[...], g_ref[...]).astype(BF16)
        ys_ref[...] = _swiglu(h, w1_ref, w3_ref, w2_ref)

    @pl.when(rows == 0)
    def _():
        ys_ref[...] = jnp.zeros_like(ys_ref)


def _moe_experts(xs, g, w1, w3, w2, tile_e, tile_rows, t):
    p, d = xs.shape
    wspec = lambda shape: pl.BlockSpec((None,) + shape, lambda i, e, rows: (e[i], 0, 0))
    row = pl.BlockSpec((t, d), lambda i, e, rows: (i, 0))
    return pl.pallas_call(
        _moe_expert_kernel,
        grid_spec=pltpu.PrefetchScalarGridSpec(
            num_scalar_prefetch=2,
            grid=(p // t,),
            in_specs=[row, pl.BlockSpec((1, d), lambda i, e, rows: (0, 0)),
                      wspec((d, D_FF_E)), wspec((d, D_FF_E)), wspec((D_FF_E, d))],
            out_specs=row),
        out_shape=jax.ShapeDtypeStruct((p, d), F32),
        compiler_params=_cparams("arbitrary"),
        name="moe_experts",
    )(tile_e, tile_rows, xs, g, w1, w3, w2)


def _moe_combine_kernel(pos_ref, x_ref, route_ref, fg_ref, ys_ref, o_ref, ybuf, sem, *, tm, final_norm):
    _row_copies(pos_ref, tm, lambda s, k, j, p: pltpu.make_async_copy(
        ys_ref.at[pl.ds(p, 1)], ybuf.at[s, k, pl.ds(j, 1)], sem))
    for s in range(2):
        pltpu.make_async_copy(ybuf.at[s], ybuf.at[s], sem).wait()
    rec = route_ref[...]
    lane = lax.broadcasted_iota(jnp.int32, rec.shape, 1)
    g1 = jnp.sum(jnp.where(lane == R_G1, rec, 0.0), axis=-1, keepdims=True)
    g2 = jnp.sum(jnp.where(lane == R_G2, rec, 0.0), axis=-1, keepdims=True)
    y1 = ybuf[0].reshape(x_ref.shape)
    y2 = ybuf[1].reshape(x_ref.shape)
    y = x_ref[...] + (g1 * y1 + g2 * y2)
    o_ref[...] = _rms(y, fg_ref[...]) if final_norm else y


def _moe_combine(x, route, pos, ys, final_g, final_norm, tm):
    n, d = x.shape
    tok = pl.BlockSpec((tm, d), lambda i: (i, 0))
    return pl.pallas_call(
        functools.partial(_moe_combine_kernel, tm=tm, final_norm=final_norm),
        grid=(n // tm,),
        in_specs=[pl.BlockSpec((None, 1, 2 * tm), lambda i: (i, 0, 0), memory_space=pltpu.SMEM),
                  tok, pl.BlockSpec((tm, LANES), lambda i: (i, 0)),
                  pl.BlockSpec((1, d), lambda i: (0, 0)), pl.BlockSpec(memory_space=pl.ANY)],
        out_specs=tok,
        out_shape=jax.ShapeDtypeStruct((n, d), F32),
        scratch_shapes=[pltpu.VMEM((2, tm // SUBLANES, SUBLANES, d), F32), pltpu.SemaphoreType.DMA(())],
        compiler_params=_cparams("arbitrary"),
        name="moe_combine",
    )(pos, x, route, final_g, ys)


def _moe(x, g, rw, w1, w3, w2, final_g, final_norm):
    B, L, D = x.shape
    n = B * L
    tm = min(512, n)
    xf = x.reshape(n, D)
    route, route_t, counts = _moe_route(xf, g, rw, tm)

    idx = route_t.astype(jnp.int32)
    cnt = counts[0, :N_EXPERTS].astype(jnp.int32)
    tiles_e = (cnt + tm - 1) // tm
    ends = jnp.cumsum(tiles_e)
    starts = ends - tiles_e
    n_used = ends[-1]
    n_tiles = 2 * n // tm + N_EXPERTS
    pos = jnp.stack([starts[idx[R_I1]] * tm + idx[R_R1],
                     starts[idx[R_I2]] * tm + idx[R_R2]])
    pos = pos.reshape(2, n // tm, tm).transpose(1, 0, 2).reshape(n // tm, 1, 2 * tm)
    tile = jnp.arange(n_tiles, dtype=jnp.int32)
    tile_e = jnp.searchsorted(ends, jnp.minimum(tile, n_used - 1), side='right').astype(jnp.int32)
    tile_rows = jnp.where(tile < n_used, jnp.clip(cnt[tile_e] - (tile - starts[tile_e]) * tm, 0, tm), 0)

    pad_start = jnp.concatenate([starts * tm + cnt, (n_used * tm)[None]])
    pad_count = jnp.concatenate([tiles_e * tm - cnt, ((n_tiles - n_used) * tm)[None]])
    pad = jnp.stack([pad_start, pad_count]).astype(jnp.int32)

    xs = _moe_dispatch(xf, pos, pad, n_tiles * tm, tm)
    ys = _moe_experts(xs, g, w1, w3, w2, tile_e, tile_rows.astype(jnp.int32), tm)
    out = _moe_combine(xf, route, pos, ys, final_g, final_norm, tm)
    return out.reshape(B, L, D)


def _pick(n, cands):
    for c in cands:
        if n % c == 0:
            return c
    return n


def _layer(x, l, w, cache, final_g, last, dkv, depth):
    B, L, D = x.shape
    assert B == SUBLANES, "the SSM kernel puts the streams of one time step on the 8 sublanes"
    lam_init = 0.8 - 0.6 * math.exp(-0.3 * l)
    tm = _pick(L, (512, 256, 128, 64, 32))
    n_keep = min(BAND, L)
    assert n_keep == tm, "the band cache rows must be exactly the last token tile"
    row = lambda a: a.reshape(1, -1).astype(F32)
    prompt = cache is None

    w_in = w['w_in'].astype(BF16)
    dk, dv = dkv if dkv is not None else (None, None)
    p = _norm_proj(x, row(w['norm1_g']), w_in[:, :N_QKVU], tm, prompt, l, depth, dk, dv)

    lam_vecs = jnp.stack([w['lam_q1'], w['lam_k1'], w['lam_q2'], w['lam_k2']]).astype(F32)
    subln_g = w['subln_g'].astype(F32)
    table = w['rel_bias'].astype(F32)
    if prompt:
        oa = _diff_attn_prompt(p['qa'], p['ka_b'], p['va_b'], lam_vecs, subln_g, lam_init)
        bias_t = _rel_bias(table, N_BAND_TILES, BAND_TILE, BAND_TILE, 0, BAND_TILE, True, True)
        ob = _band_attn_prompt(p['qb'], p['kb_b'], p['vb_b'], bias_t)
        h0r = jnp.zeros((B, N_STATE), F32)
        h0i = jnp.zeros((B, N_STATE), F32)
    else:
        ck_a, cv_a, ck_b, cv_b, h0_re, h0_im = cache
        P = ck_a.shape[1]
        bp = ck_b.shape[1]
        oa = _diff_attn_sample(p['qa'], ck_a.reshape(B, P, W_A).astype(BF16), cv_a.reshape(B, P, W_A).astype(BF16),
                               p['ka_b'], p['va_b'], lam_vecs, subln_g, lam_init)
        bias_c = _rel_bias(table, 1, L, bp, bp, 0, False, False)
        bias_n = _rel_bias(table, 1, L, L, 0, 0, False, False)
        ob = _band_attn_sample(p['qb'], ck_b.reshape(B, bp, W_B).astype(BF16), cv_b.reshape(B, bp, W_B).astype(BF16),
                               p['kb_b'], p['vb_b'], bias_c, bias_n)
        h0r = h0_re.reshape(B, N_STATE).astype(F32)
        h0i = h0_im.reshape(B, N_STATE).astype(F32)

    sw = _ssm_weights(w)
    oc_tm, hr, hi = _ssm(p['u_tm'], sw['wxr'], sw['wxi'], sw['ar'], sw['ai'], h0r, h0i,
                         sw['wcr'], sw['wci'], row(w['ssm_d']), w['w_glu'].astype(BF16), _pick(L, (64, 32)))

    x = _merge_out(x, oa, ob, oc_tm, row(w['norm1_g']), w_in[:, N_QKVU:],
                   w['w_br_a'].astype(BF16), w['w_br_b'].astype(BF16), w['w_br_c'].astype(BF16),
                   w['w_out'].astype(BF16), tm)

    fg = row(final_g)
    if l % 2 == 0:
        x = _ffn(x, row(w['norm2_g']), w['ffn_w1'].astype(BF16), w['ffn_w3'].astype(BF16),
                 w['ffn_w2'].astype(BF16), fg, last, tm)
    else:
        rw = jnp.pad(w['router_w'].astype(F32), ((0, 0), (0, LANES - N_EXPERTS)))
        x = _moe(x, row(w['norm2_g']), rw, w['moe_w1'].astype(BF16),
                 w['moe_w3'].astype(BF16), w['moe_w2'].astype(BF16), fg, last)

    small = (p['kb_last'].reshape(B, n_keep, HB, DH_B), p['vb_last'].reshape(B, n_keep, HB, DH_B),
             hr.reshape(B, N_GROUPS, P_STATE), hi.reshape(B, N_GROUPS, P_STATE))
    return x, (p['dk'], p['dv']), small


_PER_LAYER = ('norm1_g', 'w_in', 'lam_q1', 'lam_k1', 'lam_q2', 'lam_k2', 'subln_g', 'rel_bias',
              'ssm_a_re', 'ssm_a_im', 'ssm_log_dt', 'ssm_b_re', 'ssm_b_im', 'ssm_c_re', 'ssm_c_im',
              'ssm_d', 'w_glu', 'w_br_a', 'w_br_b', 'w_br_c', 'w_out', 'norm2_g')


def kernel(x_prompt, x_sample, cache_dk, cache_dv, cache_bk, cache_bv, state_ssm_re, state_ssm_im,
           norm1_g, w_in, lam_q1, lam_k1, lam_q2, lam_k2, subln_g, rel_bias, ssm_a_re, ssm_a_im,
           ssm_log_dt, ssm_b_re, ssm_b_im, ssm_c_re, ssm_c_im, ssm_d, w_glu, w_br_a, w_br_b, w_br_c,
           w_out, norm2_g, ffn_w1, ffn_w3, ffn_w2, router_w, moe_w1, moe_w3, moe_w2, final_g):
    stacked = dict(norm1_g=norm1_g, w_in=w_in, lam_q1=lam_q1, lam_k1=lam_k1, lam_q2=lam_q2, lam_k2=lam_k2,
                   subln_g=subln_g, rel_bias=rel_bias, ssm_a_re=ssm_a_re, ssm_a_im=ssm_a_im,
                   ssm_log_dt=ssm_log_dt, ssm_b_re=ssm_b_re, ssm_b_im=ssm_b_im, ssm_c_re=ssm_c_re,
                   ssm_c_im=ssm_c_im, ssm_d=ssm_d, w_glu=w_glu, w_br_a=w_br_a, w_br_b=w_br_b,
                   w_br_c=w_br_c, w_out=w_out, norm2_g=norm2_g)
    depth = w_in.shape[0]
    yp, ys = x_prompt, x_sample
    dkv_p, dkv_s = None, None
    small_p, small_s = [], []
    for l in range(depth):
        w = {name: stacked[name][l] for name in _PER_LAYER}
        if l % 2 == 0:
            w['ffn_w1'], w['ffn_w3'], w['ffn_w2'] = ffn_w1[l // 2], ffn_w3[l // 2], ffn_w2[l // 2]
        else:
            w['router_w'], w['moe_w1'] = router_w[l // 2], moe_w1[l // 2]
            w['moe_w3'], w['moe_w2'] = moe_w3[l // 2], moe_w2[l // 2]
        last = l == depth - 1
        yp, dkv_p, sp = _layer(yp, l, w, None, final_g, last, dkv_p, depth)
        ys, dkv_s, ss = _layer(ys, l, w, (cache_dk[l], cache_dv[l], cache_bk[l], cache_bv[l],
                                          state_ssm_re[l], state_ssm_im[l]), final_g, last, dkv_s, depth)
        small_p.append(sp)
        small_s.append(ss)
    outs = [yp, ys]
    for dkv, small in ((dkv_p, small_p), (dkv_s, small_s)):
        outs += list(dkv)
        for j in range(4):
            outs.append(jnp.stack([s[j] for s in small]))
    return tuple(outs)
```
